```python
import math
import jax, jax.numpy as jnp
from jax import lax
import numpy as np

D_MODEL = 1024
BATCH = 8
SEQ = 8192
DEPTH = 1
DEC_BATCH = 8
DEC_SEQ = 64
PAST_LEN = 1024

CHUNK = 64
D_MIX = D_MODEL
H_A = 4
DK_A = 128
DV_A = 128
CONV_W = 4
CONV_DIM = H_A * (2 * DK_A + DV_A)
A_WIDTH = H_A * DV_A
H_B = 8
DH_B = 64
B_WIDTH = H_B * DH_B
Q_BLOCK = 128
OFF1 = CONV_DIM
OFF2 = OFF1 + H_A
OFF3 = OFF2 + H_A
OFF4 = OFF3 + A_WIDTH
OFF5 = OFF4 + 3 * B_WIDTH
IN_COLS = OFF5 + H_B
N_KEYS = 128
N_EXPERTS = N_KEYS * N_KEYS
PEER_HEADS = 8
PEER_DQ = 256
PEER_TOPK = 16
PEER_TB = 128
PLE_DIM = 256
EPS = 1e-6

kernel_name = "hymba_gdn_fox_peer_stream_step"


def rmsnorm(x, g):
    x32 = x.astype(jnp.float32)
    y = x32 * lax.rsqrt(jnp.mean(x32 * x32, axis=-1, keepdims=True) + EPS)
    return (y * g.astype(jnp.float32)).astype(x.dtype)


def l2norm(x):
    x32 = x.astype(jnp.float32)
    return x32 * lax.rsqrt(jnp.sum(x32 * x32, axis=-1, keepdims=True) + EPS)


def causal_conv_silu(u, prefix, w):
    T = u.shape[1]
    full = jnp.concatenate([prefix.astype(u.dtype), u], axis=1)
    y = full[:, 0:T] * w[0]
    for j in range(1, CONV_W):
        y = y + full[:, j:j + T] * w[j]
    return jax.nn.silu(y), full[:, T:]


def gdn_chunk(S, q, k, v, beta, la):
    L = q.shape[2]
    g = jnp.cumsum(la, axis=-1)
    causal = jnp.tril(jnp.ones((L, L), dtype=bool))
    strict = jnp.tril(jnp.ones((L, L), dtype=bool), -1)
    decay = jnp.exp(jnp.where(causal, g[..., :, None] - g[..., None, :], -jnp.inf))
    kb = k * beta[..., None]
    M = jnp.where(strict, jnp.einsum('bhtd,bhsd->bhts', kb, k) * decay, 0.0)
    eg = jnp.exp(g)[..., None]
    rhs = jnp.concatenate([v * beta[..., None], kb * eg], axis=-1)
    sol = lax.linalg.triangular_solve(M, rhs, left_side=True, lower=True, unit_diagonal=True)
    u = sol[..., :DV_A] - jnp.einsum('bhtk,bhkv->bhtv', sol[..., DV_A:], S)
    attn = jnp.einsum('bhtd,bhsd->bhts', q, k) * decay
    o = jnp.einsum('bhtk,bhkv->bhtv', q * eg, S) + jnp.einsum('bhts,bhsv->bhtv', attn, u)
    gL = g[..., -1]
    S_new = jnp.exp(gL)[..., None, None] * S + jnp.einsum(
        'bhsk,bhsv->bhkv', k * jnp.exp(gL[..., None] - g)[..., None], u)
    return S_new, o


def gdn_recurrence(S0, q, k, v, beta, la):
    Bn, Hn, T = q.shape[:3]
    C = min(CHUNK, T)
    n = T // C

    def split(a):
        return jnp.moveaxis(a.reshape((Bn, Hn, n, C) + a.shape[3:]), 2, 0)

    S, o = lax.scan(lambda s, xs: gdn_chunk(s, *xs), S0,
                    (split(q), split(k), split(v), split(beta), split(la)))
    return jnp.moveaxis(o, 0, 2).reshape(Bn, Hn, T, DV_A), S


def gdn_mixer(z_qkv, z_b, z_a, z_g, conv_prefix, S0, conv_w, a_log, dt_bias, onorm):
    Bn, T, _ = z_qkv.shape
    c, conv_state = causal_conv_silu(z_qkv, conv_prefix, conv_w)
    c = c.astype(jnp.float32)
    q = l2norm(c[..., :H_A * DK_A].reshape(Bn, T, H_A, DK_A)) * (DK_A ** -0.5)
    k = l2norm(c[..., H_A * DK_A:2 * H_A * DK_A].reshape(Bn, T, H_A, DK_A))
    v = c[..., 2 * H_A * DK_A:].reshape(Bn, T, H_A, DV_A)
    beta = jax.nn.sigmoid(z_b.astype(jnp.float32))
    la = -jnp.exp(a_log.astype(jnp.float32)) * jax.nn.softplus(
        z_a.astype(jnp.float32) + dt_bias.astype(jnp.float32))
    tr = lambda t: jnp.swapaxes(t, 1, 2)
    o, S = gdn_recurrence(S0.astype(jnp.float32), tr(q), tr(k), tr(v), tr(beta), tr(la))
    o = rmsnorm(tr(o), onorm) * jax.nn.silu(z_g.reshape(Bn, T, H_A, DV_A).astype(jnp.float32))
    return o.reshape(Bn, T, A_WIDTH).astype(z_qkv.dtype), conv_state, S


def fox_attend(q, k, v, logf):
    Bn, T = q.shape[:2]
    Tk = k.shape[1]
    c = jnp.swapaxes(jnp.cumsum(logf.astype(jnp.float32), axis=1), 1, 2)
    qb = min(Q_BLOCK, T)
    nb = T // qb
    q_blocks = jnp.moveaxis(q.reshape(Bn, nb, qb, H_B, DH_B), 1, 0)
    c_q = jnp.moveaxis(c[:, :, Tk - T:].reshape(Bn, H_B, nb, qb), 2, 0)
    pos_q = (Tk - T + jnp.arange(T)).reshape(nb, qb)
    pos_k = jnp.arange(Tk)

    def block(args):
        qi, cqi, pqi = args
        s = jnp.einsum('bqhd,bkhd->bhqk', qi, k).astype(jnp.float32) * (DH_B ** -0.5)
        s = s + cqi[..., None] - c[:, :, None, :]
        s = jnp.where(pqi[:, None] >= pos_k[None, :], s, -jnp.inf)
        p = jax.nn.softmax(s, axis=-1)
        return jnp.einsum('bhqk,bkhd->bqhd', p.astype(v.dtype), v)

    o = lax.map(block, (q_blocks, c_q, pos_q))
    return jnp.moveaxis(o, 0, 1).reshape(Bn, T, H_B, DH_B)


def fox_mixer(z_qkv, z_f, past_k, past_v, past_logf, b_f, onorm):
    Bn, T, _ = z_qkv.shape
    q = z_qkv[..., :B_WIDTH].reshape(Bn, T, H_B, DH_B)
    k = z_qkv[..., B_WIDTH:2 * B_WIDTH].reshape(Bn, T, H_B, DH_B)
    v = z_qkv[..., 2 * B_WIDTH:].reshape(Bn, T, H_B, DH_B)
    logf = jax.nn.log_sigmoid(z_f.astype(jnp.float32) + b_f.astype(jnp.float32))
    if past_k is None:
        k_all, v_all, lf_all = k, v, logf
    else:
        k_all = jnp.concatenate([past_k.astype(k.dtype), k], axis=1)
        v_all = jnp.concatenate([past_v.astype(v.dtype), v], axis=1)
        lf_all = jnp.concatenate([past_logf.astype(jnp.float32), logf], axis=1)
    o = rmsnorm(fox_attend(q, k_all, v_all, lf_all), onorm)
    return o.reshape(Bn, T, B_WIDTH), k, v, logf


def peer(x, w_q, sub_keys, u_tab, v_tab):
    Bn, T, D = x.shape
    n = Bn * T
    nb = -(-n // PEER_TB)
    xt = jnp.pad(x.reshape(n, D), ((0, nb * PEER_TB - n), (0, 0))).reshape(nb, PEER_TB, D)
    keys32 = sub_keys.astype(jnp.float32)

    def block(xb):
        q = (xb @ w_q).reshape(PEER_TB, PEER_HEADS, 2, PEER_DQ // 2).astype(jnp.float32)
        s = jnp.einsum('thcd,cnd->thcn', q, keys32)
        sv, si = lax.top_k(s, PEER_TOPK)
        cand = sv[:, :, 0, :, None] + sv[:, :, 1, None, :]
        cv, ci = lax.top_k(cand.reshape(PEER_TB, PEER_HEADS, PEER_TOPK * PEER_TOPK), PEER_TOPK)
        i1 = jnp.take_along_axis(si[:, :, 0, :], ci // PEER_TOPK, axis=-1)
        i2 = jnp.take_along_axis(si[:, :, 1, :], ci % PEER_TOPK, axis=-1)
        e = i1 * N_KEYS + i2
        gate = jax.nn.softmax(cv, axis=-1)
        a = jnp.einsum('thkd,td->thk', jnp.take(u_tab, e, axis=0), xb).astype(jnp.float32)
        wgt = (gate * jax.nn.gelu(a, approximate=False)).astype(xb.dtype)
        return jnp.einsum('thk,thkd->td', wgt, jnp.take(v_tab, e, axis=0))

    y = lax.map(block, xt).reshape(nb * PEER_TB, D)[:n]
    return y.reshape(Bn, T, D)


def layer(h, p, conv_prefix, s0, past_k, past_v, past_logf, w):
    (w_in, conv_w, a_log, dt_bias, gdn_onorm, b_f, fox_onorm, w_out, norm_mix, norm_ffn,
     peer_wq, peer_keys, peer_u, peer_v, norm_ple, w_ple_proj, w_ple_gate) = w
    Bn = h.shape[0]
    a = rmsnorm(h, norm_mix)
    z = a @ w_in
    za_qkv, za_b, za_a, za_g, zb_qkv, zb_f = jnp.split(z, [OFF1, OFF2, OFF3, OFF4, OFF5], axis=-1)
    if conv_prefix is None:
        conv_prefix = jnp.zeros((Bn, CONV_W - 1, CONV_DIM), h.dtype)
    if s0 is None:
        s0 = jnp.zeros((Bn, H_A, DK_A, DV_A), jnp.float32)
    o_a, conv_state, S = gdn_mixer(za_qkv, za_b, za_a, za_g, conv_prefix, s0,
                                   conv_w, a_log, dt_bias, gdn_onorm)
    o_b, k_new, v_new, lf_new = fox_mixer(zb_qkv, zb_f, past_k, past_v, past_logf, b_f, fox_onorm)
    h = h + jnp.concatenate([o_a, o_b.astype(o_a.dtype)], axis=-1) @ w_out
    h = h + peer(rmsnorm(h, norm_ffn), peer_wq, peer_keys, peer_u, peer_v)
    gate = jax.nn.sigmoid((rmsnorm(h, norm_ple) @ w_ple_gate).astype(jnp.float32))
    h = h + ((p @ w_ple_proj).astype(jnp.float32) * gate).astype(h.dtype)
    return h, conv_state, S, k_new, v_new, lf_new


def setup_inputs(seed: int = 0) -> dict:
    key = jax.random.key(seed)
    ks = jax.random.split(key, 32)
    f32 = jnp.float32
    nrm = lambda k, shape, s=1.0: jax.random.normal(k, shape, f32) * s
    gain = lambda k, shape: 1.0 + 0.05 * jax.random.normal(k, shape, f32)
    dt = jnp.exp(jax.random.uniform(ks[10], (DEPTH, H_A), f32, math.log(1e-3), math.log(1e-1)))
    return {
        "x_prompt": nrm(ks[0], (BATCH, SEQ, D_MODEL)),
        "x_sample": nrm(ks[1], (DEC_BATCH, DEC_SEQ, D_MODEL)),
        "p_prompt": nrm(ks[2], (DEPTH, BATCH, SEQ, PLE_DIM)),
        "p_sample": nrm(ks[3], (DEPTH, DEC_BATCH, DEC_SEQ, PLE_DIM)),
        "cache_conv": nrm(ks[4], (DEPTH, DEC_BATCH, CONV_W - 1, CONV_DIM)),
        "state_gdn": nrm(ks[5], (DEPTH, DEC_BATCH, H_A, DK_A, DV_A), 0.1),
        "cache_fox_k": nrm(ks[6], (DEPTH, DEC_BATCH, PAST_LEN, H_B, DH_B)),
        "cache_fox_v": nrm(ks[7], (DEPTH, DEC_BATCH, PAST_LEN, H_B, DH_B)),
        "cache_fox_logf": jax.nn.log_sigmoid(nrm(ks[8], (DEPTH, DEC_BATCH, PAST_LEN, H_B)) + 3.0),
        "w_in": nrm(ks[9], (DEPTH, D_MODEL, IN_COLS), D_MODEL ** -0.5),
        "conv_w": jax.random.uniform(ks[11], (DEPTH, CONV_W, CONV_DIM), f32, -0.5, 0.5),
        "a_log": jnp.log(jax.random.uniform(ks[12], (DEPTH, H_A), f32, 1.0, 16.0)),
        "dt_bias": dt + jnp.log(-jnp.expm1(-dt)),
        "gdn_onorm": gain(ks[13], (DEPTH, DV_A)),
        "b_f": jax.random.uniform(ks[14], (DEPTH, H_B), f32, 1.0, 5.0),
        "fox_onorm": gain(ks[15], (DEPTH, DH_B)),
        "w_out": nrm(ks[16], (DEPTH, D_MIX, D_MODEL), D_MIX ** -0.5),
        "norm_mix": gain(ks[17], (DEPTH, D_MODEL)),
        "norm_ffn": gain(ks[18], (DEPTH, D_MODEL)),
        "peer_wq": nrm(ks[19], (DEPTH, D_MODEL, PEER_HEADS * PEER_DQ), D_MODEL ** -0.5),
        "peer_keys": nrm(ks[20], (DEPTH, 2, N_KEYS, PEER_DQ // 2), (PEER_DQ // 2) ** -0.5),
        "peer_u": nrm(ks[21], (DEPTH, N_EXPERTS, D_MODEL), D_MODEL ** -0.5),
        "peer_v": nrm(ks[22], (DEPTH, N_EXPERTS, D_MODEL), 0.5),
        "norm_ple": gain(ks[23], (DEPTH, D_MODEL)),
        "w_ple_proj": nrm(ks[24], (DEPTH, PLE_DIM, D_MODEL), PLE_DIM ** -0.5),
        "w_ple_gate": nrm(ks[25], (DEPTH, D_MODEL, D_MODEL), D_MODEL ** -0.5),
        "norm_final": gain(ks[26], (D_MODEL,)),
    }


def reference(x_prompt, x_sample, p_prompt, p_sample, cache_conv, state_gdn, cache_fox_k,
              cache_fox_v, cache_fox_logf, w_in, conv_w, a_log, dt_bias, gdn_onorm, b_f,
              fox_onorm, w_out, norm_mix, norm_ffn, peer_wq, peer_keys, peer_u, peer_v,
              norm_ple, w_ple_proj, w_ple_gate, norm_final):
    hp, hs = x_prompt, x_sample
    cp, sp, kp, vp, lp = [], [], [], [], []
    cs, ss, ksm, vs, ls = [], [], [], [], []
    for i in range(DEPTH):
        w = (w_in[i], conv_w[i], a_log[i], dt_bias[i], gdn_onorm[i], b_f[i], fox_onorm[i],
             w_out[i], norm_mix[i], norm_ffn[i], peer_wq[i], peer_keys[i], peer_u[i],
             peer_v[i], norm_ple[i], w_ple_proj[i], w_ple_gate[i])
        hp, c1, s1, k1, v1, l1 = layer(hp, p_prompt[i], None, None, None, None, None, w)
        hs, c2, s2, k2, v2, l2 = layer(hs, p_sample[i], cache_conv[i], state_gdn[i],
                                       cache_fox_k[i], cache_fox_v[i], cache_fox_logf[i], w)
        cp.append(c1); sp.append(s1); kp.append(k1); vp.append(v1); lp.append(l1)
        cs.append(c2); ss.append(s2); ksm.append(k2); vs.append(v2); ls.append(l2)
    y_prompt = rmsnorm(hp, norm_final)
    y_sample = rmsnorm(hs, norm_final)
    return (y_prompt, y_sample,
            jnp.stack(cp), jnp.stack(sp), jnp.stack(kp), jnp.stack(vp), jnp.stack(lp),
            jnp.stack(cs), jnp.stack(ss), jnp.stack(ksm), jnp.stack(vs), jnp.stack(ls))
```

```python
import functools
import math

import jax
import jax.numpy as jnp
from jax import lax
from jax.experimental import pallas as pl
from jax.experimental.pallas import tpu as pltpu

F32 = jnp.float32
BF16 = jnp.bfloat16

D_MODEL = 1024
H_A, DK_A, DV_A = 4, 128, 128
CONV_W = 4
CONV_DIM = H_A * (2 * DK_A + DV_A)
A_WIDTH = H_A * DV_A
H_B, DH_B = 8, 64
B_WIDTH = H_B * DH_B
OFF1 = CONV_DIM
OFF2 = OFF1 + H_A
OFF3 = OFF2 + H_A
OFF4 = OFF3 + A_WIDTH
OFF5 = OFF4 + 3 * B_WIDTH
N_KEYS = 128
PEER_HEADS = 8
PEER_TOPK = 16
PEER_SLOTS = PEER_HEADS * PEER_TOPK
PLE_DIM = 256
EPS = 1e-6
GDN_CHUNK = 64

LANES = 128
SUBLANES = 8
VMEM_LIMIT = 56 * 1024 * 1024

NEG_BIG = -1e30


def _cparams(sem, vmem=VMEM_LIMIT):
    return pltpu.CompilerParams(dimension_semantics=sem, vmem_limit_bytes=vmem)


def _split_bf16(x):
    hi = x.astype(BF16)
    lo = (x - hi.astype(F32)).astype(BF16)
    return hi, lo


def _mm(a, b, passes=1, dims=(((1,), (0,)), ((), ()))):
    dg = functools.partial(lax.dot_general, dimension_numbers=dims, preferred_element_type=F32)
    if passes == 1:
        return dg(a.astype(BF16), b.astype(BF16))
    ah, al = _split_bf16(a)
    bh, bl = _split_bf16(b)
    return dg(ah, bh) + (dg(al, bh) + dg(ah, bl))


NT = (((1,), (1,)), ((), ()))


def _rms(x, g):
    return x * lax.rsqrt(jnp.mean(x * x, axis=-1, keepdims=True) + EPS) * g


def _softplus(x):
    return jnp.maximum(x, 0.0) + jnp.log1p(jnp.exp(-jnp.abs(x)))


def _sigmoid(x):
    return 1.0 / (1.0 + jnp.exp(-x))


def _inproj_kernel(h_ref, g_ref, w_ref, ws_ref, bias_ref, alog_ref,
                   conv_ref, gate_ref, qb_ref, kb_ref, vb_ref, small_ref):
    xn = _rms(h_ref[...], g_ref[...]).astype(BF16)
    dot = functools.partial(jnp.dot, preferred_element_type=F32)
    conv_ref[...] = dot(xn, w_ref[:, 0:CONV_DIM])
    gate_ref[...] = dot(xn, w_ref[:, CONV_DIM:CONV_DIM + A_WIDTH])
    o = CONV_DIM + A_WIDTH
    qb_ref[...] = (dot(xn, w_ref[:, o:o + B_WIDTH]) * (DH_B ** -0.5)).astype(BF16)
    kb_ref[...] = dot(xn, w_ref[:, o + B_WIDTH:o + 2 * B_WIDTH])
    vb_ref[...] = dot(xn, w_ref[:, o + 2 * B_WIDTH:o + 3 * B_WIDTH])
    z = dot(xn, ws_ref[...]) + bias_ref[...]
    lane = lax.broadcasted_iota(jnp.int32, z.shape, 1)
    small_ref[...] = jnp.where(lane < H_A, _sigmoid(z),
                               jnp.where(lane < 2 * H_A, -jnp.exp(alog_ref[...]) * _softplus(z), -_softplus(-z)))


def _inproj(h, g, w_main, w_small, bias, alog, tm):
    t = h.shape[0]
    nm = w_main.shape[1]
    full = lambda shape: pl.BlockSpec(shape, lambda i: (0, 0))
    row = lambda n: pl.BlockSpec((tm, n), lambda i: (i, 0))
    return pl.pallas_call(
        _inproj_kernel,
        grid=(t // tm,),
        in_specs=[row(D_MODEL), full((1, D_MODEL)), full((D_MODEL, nm)), full((D_MODEL, LANES)),
                  full((1, LANES)), full((1, LANES))],
        out_specs=[row(CONV_DIM), row(A_WIDTH), row(B_WIDTH), row(B_WIDTH), row(B_WIDTH), row(LANES)],
        out_shape=[jax.ShapeDtypeStruct((t, CONV_DIM), F32), jax.ShapeDtypeStruct((t, A_WIDTH), F32),
                   jax.ShapeDtypeStruct((t, B_WIDTH), BF16), jax.ShapeDtypeStruct((t, B_WIDTH), F32),
                   jax.ShapeDtypeStruct((t, B_WIDTH), F32), jax.ShapeDtypeStruct((t, LANES), F32)],
        compiler_params=_cparams(("parallel",)),
        name="inproj",
    )(h, g, w_main, w_small, bias, alog)


def _split3_bf16(x):
    h1 = x.astype(BF16)
    r = x - h1.astype(F32)
    h2 = r.astype(BF16)
    h3 = (r - h2.astype(F32)).astype(BF16)
    return h1, h2, h3


def _cumsum_kernel(lf_ref, ccol_ref, crow_ref, *, chunk):
    tk = lf_ref.shape[1]
    r = lax.broadcasted_iota(jnp.int32, (chunk, chunk), 0)
    c = lax.broadcasted_iota(jnp.int32, (chunk, chunk), 1)
    tril = (r >= c).astype(BF16)
    er = lax.broadcasted_iota(jnp.int32, (SUBLANES, LANES), 0)
    ec = lax.broadcasted_iota(jnp.int32, (SUBLANES, LANES), 1)
    pick = (ec == er + SUBLANES).astype(BF16)
    dot = functools.partial(jnp.dot, preferred_element_type=F32)
    carry = jnp.zeros((1, LANES), F32)
    for i in range(tk // chunk):
        sl = pl.ds(i * chunk, chunk)
        parts = _split3_bf16(lf_ref[0, sl, :])
        cs = dot(tril, parts[0]) + (dot(tril, parts[1]) + dot(tril, parts[2])) + carry
        ccol_ref[0, sl, :] = cs
        carry = cs[chunk - 1:chunk, :]
        cparts = _split3_bf16(cs)
        dg = functools.partial(lax.dot_general, dimension_numbers=NT, preferred_element_type=F32)
        crow_ref[0, :, sl] = dg(pick, cparts[0]) + (dg(pick, cparts[1]) + dg(pick, cparts[2]))


def _fox_cumsum(lf):
    b, tk, _ = lf.shape
    chunk = tk if tk <= 2048 else 512
    return pl.pallas_call(
        functools.partial(_cumsum_kernel, chunk=chunk),
        grid=(b,),
        in_specs=[pl.BlockSpec((1, tk, LANES), lambda i: (i, 0, 0))],
        out_specs=[pl.BlockSpec((1, tk, LANES), lambda i: (i, 0, 0)),
                   pl.BlockSpec((1, SUBLANES, tk), lambda i: (i, 0, 0))],
        out_shape=[jax.ShapeDtypeStruct((b, tk, LANES), F32), jax.ShapeDtypeStruct((b, SUBLANES, tk), F32)],
        compiler_params=_cparams(("parallel",)),
        name="fox_cumsum",
    )(lf)


GDN_PASSES = 3
TN = (((0,), (0,)), ((), ()))


def _unit_lower_inverse(m, row, col):
    size = m.shape[0]
    mm = functools.partial(_mm, passes=GDN_PASSES)
    eye = (row == col).astype(F32)
    n = jnp.where((row // 8) == (col // 8), -m, 0.0)
    n2 = mm(n, n)
    n4 = mm(n2, n2)
    t = eye + n
    t = t + mm(t, n2)
    t = t + mm(t, n4)
    blk = 8
    while blk < size:
        lower = jnp.where(((row // (2 * blk)) == (col // (2 * blk))) & ((row // blk) != (col // blk)), m, 0.0)
        t = t - mm(mm(t, lower), t)
        blk *= 2
    return t


def _gdn_kernel(u_ref, small_ref, zg_ref, prefix_ref, s0_ref, cw_ref, onorm_ref,
                o_ref, s_ref, ubuf_ref):
    L = GDN_CHUNK
    mm = functools.partial(_mm, passes=GDN_PASSES)

    @pl.when(pl.program_id(1) == 0)
    def _():
        ubuf_ref[0:SUBLANES, :] = prefix_ref[0]
        s_ref[...] = s0_ref[...]

    ubuf_ref[SUBLANES:SUBLANES + L, :] = u_ref[0]
    y = ubuf_ref[SUBLANES - 3:SUBLANES - 3 + L, :] * cw_ref[0:1, :]
    for j in range(1, CONV_W):
        y = y + ubuf_ref[SUBLANES - 3 + j:SUBLANES - 3 + j + L, :] * cw_ref[j:j + 1, :]
    tail = ubuf_ref[L:L + SUBLANES, :]
    ubuf_ref[0:SUBLANES, :] = tail
    c = y * _sigmoid(y)

    row = lax.broadcasted_iota(jnp.int32, (L, L), 0)
    col = lax.broadcasted_iota(jnp.int32, (L, L), 1)
    causal = row >= col
    strict = row > col
    tril = causal.astype(BF16)
    small = small_ref[0]
    dot = functools.partial(jnp.dot, preferred_element_type=F32)
    sp = _split3_bf16(small)
    g_all = dot(tril, sp[0]) + (dot(tril, sp[1]) + dot(tril, sp[2]))
    er = lax.broadcasted_iota(jnp.int32, (SUBLANES, LANES), 0)
    ec = lax.broadcasted_iota(jnp.int32, (SUBLANES, LANES), 1)
    pick = (ec == er + H_A).astype(BF16)
    gp = _split3_bf16(g_all)
    dg = functools.partial(lax.dot_general, dimension_numbers=NT, preferred_element_type=F32)
    g_rows = dg(pick, gp[0]) + (dg(pick, gp[1]) + dg(pick, gp[2]))
    zg = zg_ref[0]

    for h in range(H_A):
        qh = c[:, h * DK_A:(h + 1) * DK_A]
        kh = c[:, H_A * DK_A + h * DK_A:H_A * DK_A + (h + 1) * DK_A]
        vh = c[:, 2 * H_A * DK_A + h * DV_A:2 * H_A * DK_A + (h + 1) * DV_A]
        qh = qh * lax.rsqrt(jnp.sum(qh * qh, axis=-1, keepdims=True) + EPS) * (DK_A ** -0.5)
        kh = kh * lax.rsqrt(jnp.sum(kh * kh, axis=-1, keepdims=True) + EPS)
        beta = small[:, h:h + 1]
        g_col = g_all[:, H_A + h:H_A + h + 1]
        diff = g_col - g_rows[h:h + 1, :]
        decay = jnp.where(causal, jnp.exp(jnp.where(causal, diff, 0.0)), 0.0)
        kb = kh * beta
        m = jnp.where(strict, mm(kb, kh, dims=NT) * decay, 0.0)
        t = _unit_lower_inverse(m, row, col)
        eg = jnp.exp(g_col)
        s_old = s_ref[0, h]
        sol_v = mm(t, vh * beta)
        sol_k = mm(t, kb * eg)
        u_new = sol_v - mm(sol_k, s_old)
        attn = mm(qh, kh, dims=NT) * decay
        o = mm(qh * eg, s_old) + mm(attn, u_new)
        g_last = g_col[L - 1:L, :]
        s_ref[0, h] = jnp.exp(g_last) * s_old + mm(kh * jnp.exp(g_last - g_col), u_new, dims=TN)
        zgh = zg[:, h * DV_A:(h + 1) * DV_A]
        o_ref[0, :, h * DV_A:(h + 1) * DV_A] = _rms(o, onorm_ref[...]) * (zgh * _sigmoid(zgh))


def _gdn(u, small, zg, prefix8, s0, conv_w, onorm):
    b, t, _ = u.shape
    L = GDN_CHUNK
    tile = lambda n: pl.BlockSpec((1, L, n), lambda i, j: (i, j, 0))
    return pl.pallas_call(
        _gdn_kernel,
        grid=(b, t // L),
        in_specs=[tile(CONV_DIM), tile(LANES), tile(A_WIDTH),
                  pl.BlockSpec((1, SUBLANES, CONV_DIM), lambda i, j: (i, 0, 0)),
                  pl.BlockSpec((1, H_A, DK_A, DV_A), lambda i, j: (i, 0, 0, 0)),
                  pl.BlockSpec((CONV_W, CONV_DIM), lambda i, j: (0, 0)),
                  pl.BlockSpec((1, DV_A), lambda i, j: (0, 0))],
        out_specs=[tile(A_WIDTH), pl.BlockSpec((1, H_A, DK_A, DV_A), lambda i, j: (i, 0, 0, 0))],
        out_shape=[jax.ShapeDtypeStruct((b, t, A_WIDTH), F32), jax.ShapeDtypeStruct((b, H_A, DK_A, DV_A), F32)],
        scratch_shapes=[pltpu.VMEM((SUBLANES + L, CONV_DIM), F32)],
        compiler_params=_cparams(("parallel", "arbitrary")),
        name="gdn",
    )(u, small, zg, prefix8, s0, conv_w, onorm)


def _fox_flash_kernel(q_ref, k_ref, v_ref, ccol_ref, crow_ref, onorm_ref, o_ref, m_ref, l_ref, acc_ref):
    qi = pl.program_id(1)
    kj = pl.program_id(2)
    tq = q_ref.shape[1]
    tk = k_ref.shape[1]

    @pl.when(kj == 0)
    def _():
        m_ref[...] = jnp.full(m_ref.shape, NEG_BIG, F32)
        l_ref[...] = jnp.zeros(l_ref.shape, F32)
        acc_ref[...] = jnp.zeros(acc_ref.shape, F32)

    def step(masked):
        if masked:
            row = lax.broadcasted_iota(jnp.int32, (tq, tk), 0)
            col = lax.broadcasted_iota(jnp.int32, (tq, tk), 1)
            keep = row >= col
        for h in range(H_B):
            hs = slice(h * DH_B, (h + 1) * DH_B)
            s = lax.dot_general(q_ref[0, :, hs], k_ref[0, :, hs].astype(BF16), NT, preferred_element_type=F32)
            s = s + (ccol_ref[0, :, SUBLANES + h:SUBLANES + h + 1] - crow_ref[0, h:h + 1, :])
            if masked:
                s = jnp.where(keep, s, NEG_BIG)
            m_old = m_ref[h]
            m_new = jnp.maximum(m_old, jnp.max(s, axis=-1, keepdims=True))
            p = jnp.exp(s - m_new)
            alpha = jnp.exp(m_old - m_new)
            l_ref[h] = alpha * l_ref[h] + jnp.sum(p, axis=-1, keepdims=True)
            acc_ref[h] = alpha * acc_ref[h] + jnp.dot(p.astype(BF16), v_ref[0, :, hs].astype(BF16),
                                                     preferred_element_type=F32)
            m_ref[h] = m_new

    @pl.when(kj < qi)
    def _():
        step(False)

    @pl.when(kj == qi)
    def _():
        step(True)
        for h in range(H_B):
            o = acc_ref[h] / l_ref[h]
            o_ref[0, :, h * DH_B:(h + 1) * DH_B] = _rms(o, onorm_ref[...])


def _fox_prompt(q, k, v, ccol, crow, onorm, tq):
    b, t, _ = q.shape
    n = t // tq
    qspec = lambda w: pl.BlockSpec((1, tq, w), lambda i, a, c: (i, a, 0))
    kspec = lambda w: pl.BlockSpec((1, tq, w), lambda i, a, c: (i, jnp.minimum(a, c), 0))
    return pl.pallas_call(
        _fox_flash_kernel,
        grid=(b, n, n),
        in_specs=[qspec(B_WIDTH), kspec(B_WIDTH), kspec(B_WIDTH), qspec(LANES),
                  pl.BlockSpec((1, SUBLANES, tq), lambda i, a, c: (i, 0, jnp.minimum(a, c))),
                  pl.BlockSpec((1, DH_B), lambda i, a, c: (0, 0))],
        out_specs=qspec(B_WIDTH),
        out_shape=jax.ShapeDtypeStruct((b, t, B_WIDTH), F32),
        scratch_shapes=[pltpu.VMEM((H_B, tq, 1), F32), pltpu.VMEM((H_B, tq, 1), F32),
                        pltpu.VMEM((H_B, tq, DH_B), F32)],
        compiler_params=_cparams(("parallel", "parallel", "arbitrary")),
        name="fox_prompt",
    )(q, k, v, ccol, crow, onorm)


def _fox_sample_kernel(q_ref, k_ref, v_ref, ccol_ref, crow_ref, onorm_ref, o_ref):
    tq = q_ref.shape[1]
    tk = k_ref.shape[1]
    row = lax.broadcasted_iota(jnp.int32, (tq, tk), 0) + (tk - tq)
    col = lax.broadcasted_iota(jnp.int32, (tq, tk), 1)
    keep = row >= col
    for h in range(H_B):
        hs = slice(h * DH_B, (h + 1) * DH_B)
        s = lax.dot_general(q_ref[0, :, hs], k_ref[0, :, hs].astype(BF16), NT, preferred_element_type=F32)
        s = s + (ccol_ref[0, :, SUBLANES + h:SUBLANES + h + 1] - crow_ref[0, h:h + 1, :])
        s = jnp.where(keep, s, NEG_BIG)
        p = jnp.exp(s - jnp.max(s, axis=-1, keepdims=True))
        l = jnp.sum(p, axis=-1, keepdims=True)
        o = jnp.dot(p.astype(BF16), v_ref[0, :, hs].astype(BF16), preferred_element_type=F32) / l
        o_ref[0, :, hs] = _rms(o, onorm_ref[...])


def _fox_sample(q, k_all, v_all, ccol_q, crow, onorm):
    b, tq, _ = q.shape
    tk = k_all.shape[1]
    blk = lambda n, w: pl.BlockSpec((1, n, w), lambda i: (i, 0, 0))
    return pl.pallas_call(
        _fox_sample_kernel,
        grid=(b,),
        in_specs=[blk(tq, B_WIDTH), blk(tk, B_WIDTH), blk(tk, B_WIDTH), blk(tq, LANES), blk(SUBLANES, tk),
                  pl.BlockSpec((1, DH_B), lambda i: (0, 0))],
        out_specs=blk(tq, B_WIDTH),
        out_shape=jax.ShapeDtypeStruct((b, tq, B_WIDTH), F32),
        compiler_params=_cparams(("parallel",)),
        name="fox_sample",
    )(q, k_all, v_all, ccol_q, crow, onorm)


def _outproj_kernel(oa_ref, ob_ref, h_ref, wo_ref, g_ref, wq_ref, h1_ref, xn_ref, qp_ref):
    dot = functools.partial(jnp.dot, preferred_element_type=F32)
    h1 = h_ref[...] + dot(oa_ref[...].astype(BF16), wo_ref[0:A_WIDTH, :]) \
        + dot(ob_ref[...].astype(BF16), wo_ref[A_WIDTH:A_WIDTH + B_WIDTH, :])
    h1_ref[...] = h1
    xn = _rms(h1, g_ref[...])
    xn_ref[...] = xn
    qp_ref[...] = dot(xn.astype(BF16), wq_ref[...])


def _outproj(oa, ob, h, w_out, g, wq, tm):
    t = h.shape[0]
    nq = wq.shape[1]
    full = lambda shape: pl.BlockSpec(shape, lambda i: (0, 0))
    row = lambda n: pl.BlockSpec((tm, n), lambda i: (i, 0))
    return pl.pallas_call(
        _outproj_kernel,
        grid=(t // tm,),
        in_specs=[row(A_WIDTH), row(B_WIDTH), row(D_MODEL), full((D_MODEL, D_MODEL)), full((1, D_MODEL)),
                  full((D_MODEL, nq))],
        out_specs=[row(D_MODEL), row(D_MODEL), row(nq)],
        out_shape=[jax.ShapeDtypeStruct((t, D_MODEL), F32), jax.ShapeDtypeStruct((t, D_MODEL), F32),
                   jax.ShapeDtypeStruct((t, nq), F32)],
        compiler_params=_cparams(("parallel",)),
        name="outproj",
    )(oa, ob, h, w_out, g, wq)


def _topk_rows(s, k, payload=None):
    n = s.shape[0]
    rid = lax.broadcasted_iota(jnp.int32, s.shape, 0)
    vals, picks = [], []
    for _ in range(k):
        m = jnp.max(s, axis=0, keepdims=True)
        idx = jnp.min(jnp.where(s == m, rid, n), axis=0, keepdims=True)
        hit = rid == idx
        vals.append(m)
        picks.append(idx if payload is None else jnp.sum(jnp.where(hit, payload, 0), axis=0, keepdims=True))
        s = jnp.where(hit, -jnp.inf, s)
    return jnp.concatenate(vals, axis=0), jnp.concatenate(picks, axis=0)


def _peer_route_kernel(qp_ref, keys_ref, e_ref, gate_ref):
    tb = qp_ref.shape[0]
    dq = LANES
    k = PEER_TOPK
    es, gs = [], []
    for h in range(PEER_HEADS):
        sv, si = [], []
        for c in range(2):
            q = qp_ref[:, (2 * h + c) * dq:(2 * h + c + 1) * dq]
            st = _mm(keys_ref[c], q, passes=3, dims=NT)
            v, i = _topk_rows(st, k)
            sv.append(v)
            si.append(i)
        cand = jnp.concatenate([sv[0][i:i + 1, :] + sv[1] for i in range(k)], axis=0)
        ecand = jnp.concatenate([si[0][i:i + 1, :] * N_KEYS + si[1] for i in range(k)], axis=0)
        cv, e = _topk_rows(cand, k, payload=ecand)
        p = jnp.exp(cv - cv[0:1, :])
        gs.append(p / jnp.sum(p, axis=0, keepdims=True))
        es.append(e)
    e_all = jnp.concatenate(es, axis=0)
    g_all = jnp.concatenate(gs, axis=0)
    for j in range(tb // LANES):
        e_ref[j * LANES:(j + 1) * LANES, :] = e_all[:, j * LANES:(j + 1) * LANES].T
        gate_ref[j * LANES:(j + 1) * LANES, :] = g_all[:, j * LANES:(j + 1) * LANES].T


def _peer_route(qp, keys, tb):
    t = qp.shape[0]
    return pl.pallas_call(
        _peer_route_kernel,
        grid=(t // tb,),
        in_specs=[pl.BlockSpec((tb, qp.shape[1]), lambda i: (i, 0)),
                  pl.BlockSpec(keys.shape, lambda i: (0, 0, 0))],
        out_specs=[pl.BlockSpec((tb, PEER_SLOTS), lambda i: (i, 0)), pl.BlockSpec((tb, PEER_SLOTS), lambda i: (i, 0))],
        out_shape=[jax.ShapeDtypeStruct((t, PEER_SLOTS), jnp.int32), jax.ShapeDtypeStruct((t, PEER_SLOTS), F32)],
        compiler_params=_cparams(("parallel",)),
        name="peer_route",
    )(qp, keys)


ROW_WORDS = D_MODEL // 2 // LANES
ROW_CHUNKS = D_MODEL // LANES


def _pack_table(tab):
    n = tab.shape[0]
    t = tab.astype(BF16).reshape(n, ROW_WORDS, 2, LANES)
    t = jnp.swapaxes(t, 2, 3)
    return lax.bitcast_convert_type(t, jnp.int32)


def _gather_rows(tab_ref, e_ref, t, slot_ref):
    for j in range(PEER_SLOTS):
        slot_ref[j * ROW_WORDS:(j + 1) * ROW_WORDS, :] = tab_ref[e_ref[t, j]]
    return pltpu.bitcast(slot_ref[...], BF16)


def _chunk_mask():
    n = PEER_SLOTS * ROW_CHUNKS
    r = lax.broadcasted_iota(jnp.int32, (ROW_CHUNKS, n), 0)
    c = lax.broadcasted_iota(jnp.int32, (ROW_CHUNKS, n), 1)
    return (c % ROW_CHUNKS == r).astype(F32)


def _gelu(a):
    return 0.5 * a * (1.0 + lax.erf(a * (2.0 ** -0.5)))


def _peer_u_kernel(e_ref, xn_ref, gate_ref, tab_ref, w_ref, slot_ref, drow_ref):
    tb = xn_ref.shape[0]
    mask = _chunk_mask()

    def body(t, carry):
        rows = _gather_rows(tab_ref, e_ref, t, slot_ref)
        xh, xl = _split_bf16(xn_ref[t])
        d = lax.dot_general(jnp.concatenate([xh, xl], axis=0), rows, NT, preferred_element_type=F32)
        e = (d[0:ROW_CHUNKS] + d[ROW_CHUNKS:2 * ROW_CHUNKS]) * mask
        drow_ref[pl.ds(t, 1), :] = jnp.sum(e, axis=0, keepdims=True)
        return carry

    lax.fori_loop(0, tb, body, 0)
    n = PEER_SLOTS * ROW_CHUNKS
    r = lax.broadcasted_iota(jnp.int32, (n, PEER_SLOTS), 0)
    c = lax.broadcasted_iota(jnp.int32, (n, PEER_SLOTS), 1)
    comp = (r // ROW_CHUNKS == c).astype(BF16)
    dp = _split3_bf16(drow_ref[...])
    dot = functools.partial(jnp.dot, preferred_element_type=F32)
    a = dot(dp[0], comp) + (dot(dp[1], comp) + dot(dp[2], comp))
    w_ref[...] = gate_ref[...] * _gelu(a)


def _peer_v_kernel(e_ref, w_ref, h_ref, tab_ref, o_ref, slot_ref, wexp_ref):
    tb = h_ref.shape[0]
    mask = _chunk_mask()
    n = PEER_SLOTS * ROW_CHUNKS
    r = lax.broadcasted_iota(jnp.int32, (PEER_SLOTS, n), 0)
    c = lax.broadcasted_iota(jnp.int32, (PEER_SLOTS, n), 1)
    expand = (c // ROW_CHUNKS == r).astype(BF16)
    wp = _split3_bf16(w_ref[...])
    dot = functools.partial(jnp.dot, preferred_element_type=F32)
    wexp_ref[...] = dot(wp[0], expand) + (dot(wp[1], expand) + dot(wp[2], expand))

    def body(t, carry):
        rows = _gather_rows(tab_ref, e_ref, t, slot_ref)
        wm = wexp_ref[pl.ds(t, 1), :] * mask
        wh, wl = _split_bf16(wm)
        d = jnp.dot(jnp.concatenate([wh, wl], axis=0), rows, preferred_element_type=F32)
        o_ref[t] = h_ref[t] + (d[0:ROW_CHUNKS] + d[ROW_CHUNKS:2 * ROW_CHUNKS])
        return carry

    lax.fori_loop(0, tb, body, 0)


def _table_spec(tab):
    return pl.BlockSpec(tab.shape, lambda i: (0, 0, 0), pipeline_mode=pl.Buffered(1))


def _peer_u(e, xn3, gate, tab, tb):
    t = e.shape[0]
    n = PEER_SLOTS * ROW_CHUNKS
    return pl.pallas_call(
        _peer_u_kernel,
        grid=(t // tb,),
        in_specs=[pl.BlockSpec((tb, PEER_SLOTS), lambda i: (i, 0), memory_space=pltpu.SMEM),
                  pl.BlockSpec((tb, ROW_CHUNKS, LANES), lambda i: (i, 0, 0)),
                  pl.BlockSpec((tb, PEER_SLOTS), lambda i: (i, 0)),
                  _table_spec(tab)],
        out_specs=pl.BlockSpec((tb, PEER_SLOTS), lambda i: (i, 0)),
        out_shape=jax.ShapeDtypeStruct((t, PEER_SLOTS), F32),
        scratch_shapes=[pltpu.VMEM((PEER_SLOTS * ROW_WORDS, LANES), jnp.int32), pltpu.VMEM((tb, n), F32)],
        compiler_params=_cparams(("arbitrary",)),
        name="peer_u",
    )(e, xn3, gate, tab)


def _peer_v(e, w, h3, tab, tb):
    t = e.shape[0]
    n = PEER_SLOTS * ROW_CHUNKS
    return pl.pallas_call(
        _peer_v_kernel,
        grid=(t // tb,),
        in_specs=[pl.BlockSpec((tb, PEER_SLOTS), lambda i: (i, 0), memory_space=pltpu.SMEM),
                  pl.BlockSpec((tb, PEER_SLOTS), lambda i: (i, 0)),
                  pl.BlockSpec((tb, ROW_CHUNKS, LANES), lambda i: (i, 0, 0)),
                  _table_spec(tab)],
        out_specs=pl.BlockSpec((tb, ROW_CHUNKS, LANES), lambda i: (i, 0, 0)),
        out_shape=jax.ShapeDtypeStruct((t, ROW_CHUNKS, LANES), F32),
        scratch_shapes=[pltpu.VMEM((PEER_SLOTS * ROW_WORDS, LANES), jnp.int32), pltpu.VMEM((tb, n), F32)],
        compiler_params=_cparams(("arbitrary",)),
        name="peer_v",
    )(e, w, h3, tab)


def _ple_kernel(h_ref, p_ref, gple_ref, wg_ref, wp_ref, gfin_ref, y_ref):
    dot = functools.partial(jnp.dot, preferred_element_type=F32)
    h = h_ref[...]
    gate = _sigmoid(dot(_rms(h, gple_ref[...]).astype(BF16), wg_ref[...]))
    h = h + dot(p_ref[...].astype(BF16), wp_ref[...]) * gate
    y_ref[...] = _rms(h, gfin_ref[...])


def _ple(h, p, gple, wg, wp, gfin, tm):
    t = h.shape[0]
    full = lambda shape: pl.BlockSpec(shape, lambda i: (0, 0))
    row = lambda n: pl.BlockSpec((tm, n), lambda i: (i, 0))
    return pl.pallas_call(
        _ple_kernel,
        grid=(t // tm,),
        in_specs=[row(D_MODEL), row(PLE_DIM), full((1, D_MODEL)), full((D_MODEL, D_MODEL)),
                  full((PLE_DIM, D_MODEL)), full((1, D_MODEL))],
        out_specs=row(D_MODEL),
        out_shape=jax.ShapeDtypeStruct((t, D_MODEL), F32),
        compiler_params=_cparams(("parallel",)),
        name="ple_final",
    )(h, p, gple, wg, wp, gfin)


def _row_tile(n, want):
    t = want
    while n % t:
        t //= 2
    return t


def _prep_weights(w_in, conv_w, a_log, dt_bias, gdn_onorm, b_f, fox_onorm, w_out, norm_mix, norm_ffn,
                  peer_wq, peer_keys, peer_u, peer_v, norm_ple, w_ple_proj, w_ple_gate, norm_final):
    w_main = jnp.concatenate([w_in[:, :OFF1], w_in[:, OFF3:OFF4], w_in[:, OFF4:OFF5]], axis=1).astype(BF16)
    w_small = jnp.concatenate([w_in[:, OFF1:OFF3], w_in[:, OFF5:],
                               jnp.zeros((D_MODEL, LANES - 2 * H_A - H_B), w_in.dtype)], axis=1).astype(BF16)
    pad = lambda v, lo: jnp.pad(v.astype(F32), (lo, LANES - lo - v.shape[0])).reshape(1, LANES)
    bias = pad(dt_bias, H_A) + pad(b_f, 2 * H_A)
    alog = pad(a_log, H_A)
    r = lambda v: v.astype(F32).reshape(1, -1)
    return dict(
        w_main=w_main, w_small=w_small, bias=bias, alog=alog, conv_w=conv_w.astype(F32),
        gdn_onorm=r(gdn_onorm), fox_onorm=r(fox_onorm), w_out=w_out.astype(BF16),
        norm_mix=r(norm_mix), norm_ffn=r(norm_ffn), wq=peer_wq.astype(BF16), keys=peer_keys.astype(F32),
        utab=_pack_table(peer_u), vtab=_pack_table(peer_v), norm_ple=r(norm_ple),
        w_proj=w_ple_proj.astype(BF16), w_gate=w_ple_gate.astype(BF16), norm_final=r(norm_final))


def _layer(x, p, prefix, s0, past_k, past_v, past_lf, w):
    b, t, _ = x.shape
    n = b * t
    tm = _row_tile(n, 512)
    h = x.reshape(n, D_MODEL)
    conv_in, zg, qb, kb, vb, small = _inproj(h, w["norm_mix"], w["w_main"], w["w_small"], w["bias"], w["alog"], tm)
    conv3 = conv_in.reshape(b, t, CONV_DIM)
    small3 = small.reshape(b, t, LANES)
    prefix8 = jnp.zeros((b, SUBLANES, CONV_DIM), F32)
    if prefix is not None:
        prefix8 = prefix8.at[:, SUBLANES - (CONV_W - 1):, :].set(prefix.astype(F32))
    if s0 is None:
        s0 = jnp.zeros((b, H_A, DK_A, DV_A), F32)
    o_a, s_new = _gdn(conv3, small3, zg.reshape(b, t, A_WIDTH), prefix8, s0.astype(F32), w["conv_w"], w["gdn_onorm"])

    q3 = qb.reshape(b, t, B_WIDTH)
    k3 = kb.reshape(b, t, B_WIDTH)
    v3 = vb.reshape(b, t, B_WIDTH)
    if past_k is None:
        ccol, crow = _fox_cumsum(small3)
        o_b = _fox_prompt(q3, k3, v3, ccol, crow, w["fox_onorm"], _row_tile(t, 512))
    else:
        pl_len = past_k.shape[1]
        lf_past = jnp.pad(past_lf.astype(F32), ((0, 0), (0, 0), (2 * H_A, LANES - 2 * H_A - H_B)))
        ccol, crow = _fox_cumsum(jnp.concatenate([lf_past, small3], axis=1))
        k_all = jnp.concatenate([past_k.reshape(b, pl_len, B_WIDTH).astype(F32), k3], axis=1)
        v_all = jnp.concatenate([past_v.reshape(b, pl_len, B_WIDTH).astype(F32), v3], axis=1)
        o_b = _fox_sample(q3, k_all, v_all, ccol[:, pl_len:, :], crow, w["fox_onorm"])

    h1, xn, qp = _outproj(o_a.reshape(n, A_WIDTH), o_b.reshape(n, B_WIDTH), h, w["w_out"], w["norm_ffn"], w["wq"], tm)
    tb = _row_tile(n, 128)
    e, gate = _peer_route(qp, w["keys"], tb)
    wgt = _peer_u(e, xn.reshape(n, ROW_CHUNKS, LANES), gate, w["utab"], tb)
    h2 = _peer_v(e, wgt, h1.reshape(n, ROW_CHUNKS, LANES), w["vtab"], tb)
    y = _ple(h2.reshape(n, D_MODEL), p.reshape(n, PLE_DIM), w["norm_ple"], w["w_gate"], w["w_proj"],
             w["norm_final"], tm)
    conv_state = conv3[:, t - (CONV_W - 1):, :]
    return (y.reshape(b, t, D_MODEL), conv_state, s_new, k3.reshape(b, t, H_B, DH_B), v3.reshape(b, t, H_B, DH_B),
            small3[:, :, 2 * H_A:2 * H_A + H_B])


def kernel(x_prompt, x_sample, p_prompt, p_sample, cache_conv, state_gdn, cache_fox_k, cache_fox_v, cache_fox_logf, w_in, conv_w, a_log, dt_bias, gdn_onorm, b_f, fox_onorm, w_out, norm_mix, norm_ffn, peer_wq, peer_keys, peer_u, peer_v, norm_ple, w_ple_proj, w_ple_gate, norm_final):
    assert w_in.shape[0] == 1, "single-layer step"
    w = _prep_weights(w_in[0], conv_w[0], a_log[0], dt_bias[0], gdn_onorm[0], b_f[0], fox_onorm[0], w_out[0],
                      norm_mix[0], norm_ffn[0], peer_wq[0], peer_keys[0], peer_u[0], peer_v[0], norm_ple[0],
                      w_ple_proj[0], w_ple_gate[0], norm_final)
    yp, c1, s1, k1, v1, l1 = _layer(x_prompt, p_prompt[0], None, None, None, None, None, w)
    ys, c2, s2, k2, v2, l2 = _layer(x_sample, p_sample[0], cache_conv[0], state_gdn[0], cache_fox_k[0],
                                    cache_fox_v[0], cache_fox_logf[0], w)
    st = lambda a: a[None]
    return (yp, ys, st(c1), st(s1), st(k1), st(v1), st(l1), st(c2), st(s2), st(k2), st(v2), st(l2))
```

```python
import functools
import math

import jax
import jax.numpy as jnp
from jax import lax
from jax.experimental import pallas as pl
from jax.experimental.pallas import tpu as pltpu

F32 = jnp.float32
BF16 = jnp.bfloat16

D_MODEL = 1024
H_A, DK_A, DV_A = 4, 128, 128
CONV_W = 4
CONV_DIM = H_A * (2 * DK_A + DV_A)
A_WIDTH = H_A * DV_A
H_B, DH_B = 8, 64
B_WIDTH = H_B * DH_B
OFF1 = CONV_DIM
OFF2 = OFF1 + H_A
OFF3 = OFF2 + H_A
OFF4 = OFF3 + A_WIDTH
OFF5 = OFF4 + 3 * B_WIDTH
N_KEYS = 128
PEER_HEADS = 8
PEER_TOPK = 16
PEER_SLOTS = PEER_HEADS * PEER_TOPK
PLE_DIM = 256
EPS = 1e-6
GDN_CHUNK = 64

LANES = 128
SUBLANES = 8
VMEM_LIMIT = 56 * 1024 * 1024

NEG_BIG = -1e30


def _cparams(sem, vmem=VMEM_LIMIT):
    return pltpu.CompilerParams(dimension_semantics=sem, vmem_limit_bytes=vmem)


def _split_bf16(x):
    hi = x.astype(BF16)
    lo = (x - hi.astype(F32)).astype(BF16)
    return hi, lo


def _mm(a, b, passes=1, dims=(((1,), (0,)), ((), ()))):
    dg = functools.partial(lax.dot_general, dimension_numbers=dims, preferred_element_type=F32)
    if passes == 1:
        return dg(a.astype(BF16), b.astype(BF16))
    ah, al = _split_bf16(a)
    bh, bl = _split_bf16(b)
    return dg(ah, bh) + (dg(al, bh) + dg(ah, bl))


NT = (((1,), (1,)), ((), ()))


def _rms(x, g):
    return x * lax.rsqrt(jnp.mean(x * x, axis=-1, keepdims=True) + EPS) * g


def _softplus(x):
    return jnp.maximum(x, 0.0) + jnp.log1p(jnp.exp(-jnp.abs(x)))


def _sigmoid(x):
    return 1.0 / (1.0 + jnp.exp(-x))


def _inproj_kernel(h_ref, g_ref, w_ref, ws_ref, bias_ref, alog_ref,
                   conv_ref, gate_ref, qb_ref, kb_ref, vb_ref, small_ref):
    xn = _rms(h_ref[...], g_ref[...]).astype(BF16)
    dot = functools.partial(jnp.dot, preferred_element_type=F32)
    conv_ref[...] = dot(xn, w_ref[:, 0:CONV_DIM])
    gate_ref[...] = dot(xn, w_ref[:, CONV_DIM:CONV_DIM + A_WIDTH])
    o = CONV_DIM + A_WIDTH
    qb_ref[...] = (dot(xn, w_ref[:, o:o + B_WIDTH]) * (DH_B ** -0.5)).astype(BF16)
    kb_ref[...] = dot(xn, w_ref[:, o + B_WIDTH:o + 2 * B_WIDTH])
    vb_ref[...] = dot(xn, w_ref[:, o + 2 * B_WIDTH:o + 3 * B_WIDTH])
    z = dot(xn, ws_ref[...]) + bias_ref[...]
    lane = lax.broadcasted_iota(jnp.int32, z.shape, 1)
    small_ref[...] = jnp.where(lane < H_A, _sigmoid(z),
                               jnp.where(lane < 2 * H_A, -jnp.exp(alog_ref[...]) * _softplus(z), -_softplus(-z)))


def _inproj(h, g, w_main, w_small, bias, alog, tm):
    t = h.shape[0]
    nm = w_main.shape[1]
    full = lambda shape: pl.BlockSpec(shape, lambda i: (0, 0))
    row = lambda n: pl.BlockSpec((tm, n), lambda i: (i, 0))
    return pl.pallas_call(
        _inproj_kernel,
        grid=(t // tm,),
        in_specs=[row(D_MODEL), full((1, D_MODEL)), full((D_MODEL, nm)), full((D_MODEL, LANES)),
                  full((1, LANES)), full((1, LANES))],
        out_specs=[row(CONV_DIM), row(A_WIDTH), row(B_WIDTH), row(B_WIDTH), row(B_WIDTH), row(LANES)],
        out_shape=[jax.ShapeDtypeStruct((t, CONV_DIM), F32), jax.ShapeDtypeStruct((t, A_WIDTH), F32),
                   jax.ShapeDtypeStruct((t, B_WIDTH), BF16), jax.ShapeDtypeStruct((t, B_WIDTH), F32),
                   jax.ShapeDtypeStruct((t, B_WIDTH), F32), jax.ShapeDtypeStruct((t, LANES), F32)],
        compiler_params=_cparams(("parallel",)),
        name="inproj",
    )(h, g, w_main, w_small, bias, alog)


def _split3_bf16(x):
    h1 = x.astype(BF16)
    r = x - h1.astype(F32)
    h2 = r.astype(BF16)
    h3 = (r - h2.astype(F32)).astype(BF16)
    return h1, h2, h3


def _cumsum_kernel(lf_ref, ccol_ref, crow_ref, *, chunk):
    tk = lf_ref.shape[1]
    r = lax.broadcasted_iota(jnp.int32, (chunk, chunk), 0)
    c = lax.broadcasted_iota(jnp.int32, (chunk, chunk), 1)
    tril = (r >= c).astype(BF16)
    er = lax.broadcasted_iota(jnp.int32, (SUBLANES, LANES), 0)
    ec = lax.broadcasted_iota(jnp.int32, (SUBLANES, LANES), 1)
    pick = (ec == er + SUBLANES).astype(BF16)
    dot = functools.partial(jnp.dot, preferred_element_type=F32)
    carry = jnp.zeros((1, LANES), F32)
    for i in range(tk // chunk):
        sl = pl.ds(i * chunk, chunk)
        parts = _split3_bf16(lf_ref[0, sl, :])
        cs = dot(tril, parts[0]) + (dot(tril, parts[1]) + dot(tril, parts[2])) + carry
        ccol_ref[0, sl, :] = cs
        carry = cs[chunk - 1:chunk, :]
        cparts = _split3_bf16(cs)
        dg = functools.partial(lax.dot_general, dimension_numbers=NT, preferred_element_type=F32)
        crow_ref[0, :, sl] = dg(pick, cparts[0]) + (dg(pick, cparts[1]) + dg(pick, cparts[2]))


def _fox_cumsum(lf):
    b, tk, _ = lf.shape
    chunk = tk if tk <= 2048 else 512
    return pl.pallas_call(
        functools.partial(_cumsum_kernel, chunk=chunk),
        grid=(b,),
        in_specs=[pl.BlockSpec((1, tk, LANES), lambda i: (i, 0, 0))],
        out_specs=[pl.BlockSpec((1, tk, LANES), lambda i: (i, 0, 0)),
                   pl.BlockSpec((1, SUBLANES, tk), lambda i: (i, 0, 0))],
        out_shape=[jax.ShapeDtypeStruct((b, tk, LANES), F32), jax.ShapeDtypeStruct((b, SUBLANES, tk), F32)],
        compiler_params=_cparams(("parallel",)),
        name="fox_cumsum",
    )(lf)


GDN_PASSES = 3
TN = (((0,), (0,)), ((), ()))


def _unit_lower_inverses(ms, row, col):
    size = ms[0].shape[0]
    mm = functools.partial(_mm, passes=GDN_PASSES)
    eye = (row == col).astype(F32)
    diag = (row // 8) == (col // 8)
    ns = [jnp.where(diag, -m, 0.0) for m in ms]
    n2s = [mm(n, n) for n in ns]
    n4s = [mm(n2, n2) for n2 in n2s]
    ts = [eye + n for n in ns]
    ts = [t + mm(t, n2) for t, n2 in zip(ts, n2s)]
    ts = [t + mm(t, n4) for t, n4 in zip(ts, n4s)]
    blk = 8
    while blk < size:
        sel = ((row // (2 * blk)) == (col // (2 * blk))) & ((row // blk) != (col // blk))
        tls = [mm(t, jnp.where(sel, m, 0.0)) for t, m in zip(ts, ms)]
        ts = [t - mm(tl, t) for t, tl in zip(ts, tls)]
        blk *= 2
    return ts


def _gdn_kernel(u_ref, small_ref, zg_ref, prefix_ref, s0_ref, cw_ref, onorm_ref,
                o_ref, s_ref, ubuf_ref):
    L = GDN_CHUNK
    nb = u_ref.shape[0]
    mm = functools.partial(_mm, passes=GDN_PASSES)
    dot = functools.partial(jnp.dot, preferred_element_type=F32)
    dg = functools.partial(lax.dot_general, dimension_numbers=NT, preferred_element_type=F32)

    @pl.when(pl.program_id(1) == 0)
    def _():
        ubuf_ref[:, 0:SUBLANES, :] = prefix_ref[...]
        s_ref[...] = s0_ref[...]

    row = lax.broadcasted_iota(jnp.int32, (L, L), 0)
    col = lax.broadcasted_iota(jnp.int32, (L, L), 1)
    causal = row >= col
    strict = row > col
    tril = causal.astype(BF16)
    er = lax.broadcasted_iota(jnp.int32, (SUBLANES, LANES), 0)
    ec = lax.broadcasted_iota(jnp.int32, (SUBLANES, LANES), 1)
    pick = (ec == er + H_A).astype(BF16)

    cs, smalls, g_alls, g_rowss = [], [], [], []
    for b in range(nb):
        ubuf_ref[b, SUBLANES:SUBLANES + L, :] = u_ref[b]
        y = ubuf_ref[b, SUBLANES - 3:SUBLANES - 3 + L, :] * cw_ref[0:1, :]
        for j in range(1, CONV_W):
            y = y + ubuf_ref[b, SUBLANES - 3 + j:SUBLANES - 3 + j + L, :] * cw_ref[j:j + 1, :]
        tail = ubuf_ref[b, L:L + SUBLANES, :]
        ubuf_ref[b, 0:SUBLANES, :] = tail
        cs.append(y * _sigmoid(y))
        small = small_ref[b]
        sp = _split3_bf16(small)
        g_all = dot(tril, sp[0]) + (dot(tril, sp[1]) + dot(tril, sp[2]))
        gp = _split3_bf16(g_all)
        smalls.append(small)
        g_alls.append(g_all)
        g_rowss.append(dg(pick, gp[0]) + (dg(pick, gp[1]) + dg(pick, gp[2])))

    chains = [(b, h) for b in range(nb) for h in range(H_A)]
    qs, ks, vs, betas, gcols, decays, kbs = [], [], [], [], [], [], []
    for b, h in chains:
        c = cs[b]
        qh = c[:, h * DK_A:(h + 1) * DK_A]
        kh = c[:, H_A * DK_A + h * DK_A:H_A * DK_A + (h + 1) * DK_A]
        qs.append(qh * lax.rsqrt(jnp.sum(qh * qh, axis=-1, keepdims=True) + EPS) * (DK_A ** -0.5))
        kh = kh * lax.rsqrt(jnp.sum(kh * kh, axis=-1, keepdims=True) + EPS)
        ks.append(kh)
        vs.append(c[:, 2 * H_A * DK_A + h * DV_A:2 * H_A * DK_A + (h + 1) * DV_A])
        beta = smalls[b][:, h:h + 1]
        g_col = g_alls[b][:, H_A + h:H_A + h + 1]
        diff = g_col - g_rowss[b][h:h + 1, :]
        betas.append(beta)
        gcols.append(g_col)
        decays.append(jnp.where(causal, jnp.exp(jnp.where(causal, diff, 0.0)), 0.0))
        kbs.append(kh * beta)
    ms = [jnp.where(strict, mm(kb, k, dims=NT) * d, 0.0) for kb, k, d in zip(kbs, ks, decays)]
    ts = _unit_lower_inverses(ms, row, col)
    egs = [jnp.exp(g) for g in gcols]
    s_olds = [s_ref[b, h] for b, h in chains]
    sol_vs = [mm(t, v * beta) for t, v, beta in zip(ts, vs, betas)]
    sol_ks = [mm(t, kb * eg) for t, kb, eg in zip(ts, kbs, egs)]
    u_news = [sv - mm(sk, s) for sv, sk, s in zip(sol_vs, sol_ks, s_olds)]
    attns = [_mm(q, k, dims=NT) * d for q, k, d in zip(qs, ks, decays)]
    outs = [_mm(q * eg, s) + _mm(a, u) for q, eg, s, a, u in zip(qs, egs, s_olds, attns, u_news)]
    for (b, h), k, g, s, u, o in zip(chains, ks, gcols, s_olds, u_news, outs):
        g_last = g[L - 1:L, :]
        s_ref[b, h] = jnp.exp(g_last) * s + mm(k * jnp.exp(g_last - g), u, dims=TN)
        zgh = zg_ref[b, :, h * DV_A:(h + 1) * DV_A]
        o_ref[b, :, h * DV_A:(h + 1) * DV_A] = _rms(o, onorm_ref[...]) * (zgh * _sigmoid(zgh))


def _gdn(u, small, zg, prefix8, s0, conv_w, onorm, nb):
    b, t, _ = u.shape
    L = GDN_CHUNK
    tile = lambda n: pl.BlockSpec((nb, L, n), lambda i, j: (i, j, 0))
    return pl.pallas_call(
        _gdn_kernel,
        grid=(b // nb, t // L),
        in_specs=[tile(CONV_DIM), tile(LANES), tile(A_WIDTH),
                  pl.BlockSpec((nb, SUBLANES, CONV_DIM), lambda i, j: (i, 0, 0)),
                  pl.BlockSpec((nb, H_A, DK_A, DV_A), lambda i, j: (i, 0, 0, 0)),
                  pl.BlockSpec((CONV_W, CONV_DIM), lambda i, j: (0, 0)),
                  pl.BlockSpec((1, DV_A), lambda i, j: (0, 0))],
        out_specs=[tile(A_WIDTH), pl.BlockSpec((nb, H_A, DK_A, DV_A), lambda i, j: (i, 0, 0, 0))],
        out_shape=[jax.ShapeDtypeStruct((b, t, A_WIDTH), F32), jax.ShapeDtypeStruct((b, H_A, DK_A, DV_A), F32)],
        scratch_shapes=[pltpu.VMEM((nb, SUBLANES + L, CONV_DIM), F32)],
        compiler_params=_cparams(("parallel", "arbitrary")),
        name="gdn",
    )(u, small, zg, prefix8, s0, conv_w, onorm)


def _fox_flash_kernel(q_ref, k_ref, v_ref, ccol_ref, crow_ref, onorm_ref, o_ref, m_ref, l_ref, acc_ref):
    qi = pl.program_id(1)
    kj = pl.program_id(2)
    tq = q_ref.shape[1]
    tk = k_ref.shape[1]

    @pl.when(kj == 0)
    def _():
        m_ref[...] = jnp.full(m_ref.shape, NEG_BIG, F32)
        l_ref[...] = jnp.zeros(l_ref.shape, F32)
        acc_ref[...] = jnp.zeros(acc_ref.shape, F32)

    def step(masked):
        if masked:
            row = lax.broadcasted_iota(jnp.int32, (tq, tk), 0)
            col = lax.broadcasted_iota(jnp.int32, (tq, tk), 1)
            keep = row >= col
        for h in range(H_B):
            hs = slice(h * DH_B, (h + 1) * DH_B)
            s = lax.dot_general(q_ref[0, :, hs], k_ref[0, :, hs].astype(BF16), NT, preferred_element_type=F32)
            s = s + (ccol_ref[0, :, SUBLANES + h:SUBLANES + h + 1] - crow_ref[0, h:h + 1, :])
            if masked:
                s = jnp.where(keep, s, NEG_BIG)
            m_old = m_ref[h]
            m_new = jnp.maximum(m_old, jnp.max(s, axis=-1, keepdims=True))
            p = jnp.exp(s - m_new)
            alpha = jnp.exp(m_old - m_new)
            l_ref[h] = alpha * l_ref[h] + jnp.sum(p, axis=-1, keepdims=True)
            acc_ref[h] = alpha * acc_ref[h] + jnp.dot(p.astype(BF16), v_ref[0, :, hs].astype(BF16),
                                                     preferred_element_type=F32)
            m_ref[h] = m_new

    @pl.when(kj < qi)
    def _():
        step(False)

    @pl.when(kj == qi)
    def _():
        step(True)
        for h in range(H_B):
            o = acc_ref[h] / l_ref[h]
            o_ref[0, :, h * DH_B:(h + 1) * DH_B] = _rms(o, onorm_ref[...])


def _fox_prompt(q, k, v, ccol, crow, onorm, tq):
    b, t, _ = q.shape
    n = t // tq
    qspec = lambda w: pl.BlockSpec((1, tq, w), lambda i, a, c: (i, a, 0))
    kspec = lambda w: pl.BlockSpec((1, tq, w), lambda i, a, c: (i, jnp.minimum(a, c), 0))
    return pl.pallas_call(
        _fox_flash_kernel,
        grid=(b, n, n),
        in_specs=[qspec(B_WIDTH), kspec(B_WIDTH), kspec(B_WIDTH), qspec(LANES),
                  pl.BlockSpec((1, SUBLANES, tq), lambda i, a, c: (i, 0, jnp.minimum(a, c))),
                  pl.BlockSpec((1, DH_B), lambda i, a, c: (0, 0))],
        out_specs=qspec(B_WIDTH),
        out_shape=jax.ShapeDtypeStruct((b, t, B_WIDTH), F32),
        scratch_shapes=[pltpu.VMEM((H_B, tq, 1), F32), pltpu.VMEM((H_B, tq, 1), F32),
                        pltpu.VMEM((H_B, tq, DH_B), F32)],
        compiler_params=_cparams(("parallel", "parallel", "arbitrary")),
        name="fox_prompt",
    )(q, k, v, ccol, crow, onorm)


def _fox_sample_kernel(q_ref, k_ref, v_ref, ccol_ref, crow_ref, onorm_ref, o_ref):
    tq = q_ref.shape[1]
    tk = k_ref.shape[1]
    row = lax.broadcasted_iota(jnp.int32, (tq, tk), 0) + (tk - tq)
    col = lax.broadcasted_iota(jnp.int32, (tq, tk), 1)
    keep = row >= col
    for h in range(H_B):
        hs = slice(h * DH_B, (h + 1) * DH_B)
        s = lax.dot_general(q_ref[0, :, hs], k_ref[0, :, hs].astype(BF16), NT, preferred_element_type=F32)
        s = s + (ccol_ref[0, :, SUBLANES + h:SUBLANES + h + 1] - crow_ref[0, h:h + 1, :])
        s = jnp.where(keep, s, NEG_BIG)
        p = jnp.exp(s - jnp.max(s, axis=-1, keepdims=True))
        l = jnp.sum(p, axis=-1, keepdims=True)
        o = jnp.dot(p.astype(BF16), v_ref[0, :, hs].astype(BF16), preferred_element_type=F32) / l
        o_ref[0, :, hs] = _rms(o, onorm_ref[...])


def _fox_sample(q, k_all, v_all, ccol_q, crow, onorm):
    b, tq, _ = q.shape
    tk = k_all.shape[1]
    blk = lambda n, w: pl.BlockSpec((1, n, w), lambda i: (i, 0, 0))
    return pl.pallas_call(
        _fox_sample_kernel,
        grid=(b,),
        in_specs=[blk(tq, B_WIDTH), blk(tk, B_WIDTH), blk(tk, B_WIDTH), blk(tq, LANES), blk(SUBLANES, tk),
                  pl.BlockSpec((1, DH_B), lambda i: (0, 0))],
        out_specs=blk(tq, B_WIDTH),
        out_shape=jax.ShapeDtypeStruct((b, tq, B_WIDTH), F32),
        compiler_params=_cparams(("parallel",)),
        name="fox_sample",
    )(q, k_all, v_all, ccol_q, crow, onorm)


def _outproj_kernel(oa_ref, ob_ref, h_ref, wo_ref, g_ref, wq_ref, h1_ref, xn_ref, qp_ref):
    dot = functools.partial(jnp.dot, preferred_element_type=F32)
    h1 = h_ref[...] + dot(oa_ref[...].astype(BF16), wo_ref[0:A_WIDTH, :]) \
        + dot(ob_ref[...].astype(BF16), wo_ref[A_WIDTH:A_WIDTH + B_WIDTH, :])
    h1_ref[...] = h1
    xn = _rms(h1, g_ref[...])
    xn_ref[...] = xn
    qp_ref[...] = dot(xn.astype(BF16), wq_ref[...])


def _outproj(oa, ob, h, w_out, g, wq, tm):
    t = h.shape[0]
    nq = wq.shape[1]
    full = lambda shape: pl.BlockSpec(shape, lambda i: (0, 0))
    row = lambda n: pl.BlockSpec((tm, n), lambda i: (i, 0))
    return pl.pallas_call(
        _outproj_kernel,
        grid=(t // tm,),
        in_specs=[row(A_WIDTH), row(B_WIDTH), row(D_MODEL), full((D_MODEL, D_MODEL)), full((1, D_MODEL)),
                  full((D_MODEL, nq))],
        out_specs=[row(D_MODEL), row(D_MODEL), row(nq)],
        out_shape=[jax.ShapeDtypeStruct((t, D_MODEL), F32), jax.ShapeDtypeStruct((t, D_MODEL), F32),
                   jax.ShapeDtypeStruct((t, nq), F32)],
        compiler_params=_cparams(("parallel",)),
        name="outproj",
    )(oa, ob, h, w_out, g, wq)


def _topk_rows(s, k, payload=None):
    n = s.shape[0]
    rid = lax.broadcasted_iota(jnp.int32, s.shape, 0).astype(F32)
    vals, picks = [], []
    for _ in range(k):
        m = jnp.max(s, axis=0, keepdims=True)
        idx = jnp.min(jnp.where(s == m, rid, float(n)), axis=0, keepdims=True)
        hit = rid == idx
        vals.append(m)
        picks.append(idx if payload is None else jnp.sum(jnp.where(hit, payload, 0.0), axis=0, keepdims=True))
        s = jnp.where(hit, -jnp.inf, s)
    return jnp.concatenate(vals, axis=0), jnp.concatenate(picks, axis=0)


PEER_PAIRS = [(i, j) for i in range(PEER_TOPK) for j in range(PEER_TOPK) if (i + 1) * (j + 1) <= PEER_TOPK]
PEER_PAIR_ROWS = -(-len(PEER_PAIRS) // SUBLANES) * SUBLANES


def _pair_selectors():
    sel = [[[1.0 if (r < len(PEER_PAIRS) and PEER_PAIRS[r][side] == i) else 0.0 for i in range(PEER_TOPK)]
            for r in range(PEER_PAIR_ROWS)] for side in range(2)]
    return jnp.asarray(sel, BF16)


def _select_rows(sel, x, exact_f32):
    dot = functools.partial(jnp.dot, preferred_element_type=F32)
    if not exact_f32:
        return dot(sel, x.astype(BF16))
    p = _split3_bf16(x)
    return dot(sel, p[0]) + dot(sel, p[1]) + dot(sel, p[2])


def _peer_route_kernel(qp_ref, keys_ref, sel_ref, e_ref, gate_ref):
    tb = qp_ref.shape[0]
    dq = LANES
    k = PEER_TOPK
    valid = lax.broadcasted_iota(jnp.int32, (PEER_PAIR_ROWS, tb), 0) < len(PEER_PAIRS)
    es, gs = [], []
    for h in range(PEER_HEADS):
        sv, si = [], []
        for c in range(2):
            q = qp_ref[:, (2 * h + c) * dq:(2 * h + c + 1) * dq]
            st = _mm(keys_ref[c], q, passes=3, dims=NT)
            v, i = _topk_rows(st, k)
            sv.append(v)
            si.append(i)
        cand = _select_rows(sel_ref[0], sv[0], True) + _select_rows(sel_ref[1], sv[1], True)
        cand = jnp.where(valid, cand, -jnp.inf)
        ecand = (_select_rows(sel_ref[0], si[0], False) * N_KEYS + _select_rows(sel_ref[1], si[1], False)) * ROW_WORDS
        cv, e = _topk_rows(cand, k, payload=ecand)
        p = jnp.exp(cv - cv[0:1, :])
        gs.append(p / jnp.sum(p, axis=0, keepdims=True))
        es.append(e)
    e_all = jnp.concatenate(es, axis=0)
    g_all = jnp.concatenate(gs, axis=0)
    for j in range(tb // LANES):
        e_ref[j * LANES:(j + 1) * LANES, :] = e_all[:, j * LANES:(j + 1) * LANES].T.astype(jnp.int32)
        gate_ref[j * LANES:(j + 1) * LANES, :] = g_all[:, j * LANES:(j + 1) * LANES].T


def _peer_route(qp, keys, tb):
    t = qp.shape[0]
    sel = _pair_selectors()
    return pl.pallas_call(
        _peer_route_kernel,
        grid=(t // tb,),
        in_specs=[pl.BlockSpec((tb, qp.shape[1]), lambda i: (i, 0)),
                  pl.BlockSpec(keys.shape, lambda i: (0, 0, 0)),
                  pl.BlockSpec(sel.shape, lambda i: (0, 0, 0))],
        out_specs=[pl.BlockSpec((tb, PEER_SLOTS), lambda i: (i, 0)), pl.BlockSpec((tb, PEER_SLOTS), lambda i: (i, 0))],
        out_shape=[jax.ShapeDtypeStruct((t, PEER_SLOTS), jnp.int32), jax.ShapeDtypeStruct((t, PEER_SLOTS), F32)],
        compiler_params=_cparams(("parallel",)),
        name="peer_route",
    )(qp, keys, sel)


ROW_WORDS = D_MODEL // 2 // LANES
ROW_CHUNKS = D_MODEL // LANES


def _pack_table(tab):
    n = tab.shape[0]
    t = tab.astype(BF16).reshape(n, ROW_WORDS, 2, LANES)
    t = jnp.swapaxes(t, 2, 3)
    return lax.bitcast_convert_type(t, jnp.int32).reshape(n * ROW_WORDS, LANES)


def _gather_rows(tab_ref, e_ref, t, slot_ref):
    ids = e_ref.at[t]
    for j in range(PEER_SLOTS):
        slot_ref[j * ROW_WORDS:(j + 1) * ROW_WORDS, :] = tab_ref[pl.ds(pl.multiple_of(ids[j], ROW_WORDS), ROW_WORDS), :]


def _slot_rows(slot_ref):
    return pltpu.bitcast(slot_ref[...], BF16)


def _two_slot_token_loop(tb, gather, compute):
    gather(0, 0)

    def body(i, carry):
        t0 = 2 * i
        gather(t0 + 1, 1)
        compute(t0, 0)
        gather(jnp.minimum(t0 + 2, tb - 1), 0)
        compute(t0 + 1, 1)
        return carry

    lax.fori_loop(0, tb // 2, body, 0)


def _chunk_mask():
    n = PEER_SLOTS * ROW_CHUNKS
    r = lax.broadcasted_iota(jnp.int32, (ROW_CHUNKS, n), 0)
    c = lax.broadcasted_iota(jnp.int32, (ROW_CHUNKS, n), 1)
    return (c % ROW_CHUNKS == r).astype(F32)


def _gelu(a):
    return 0.5 * a * (1.0 + lax.erf(a * (2.0 ** -0.5)))


def _peer_u_kernel(e_ref, xn_ref, gate_ref, tab_ref, w_ref, slot0_ref, slot1_ref, drow_ref):
    tb = xn_ref.shape[0]
    mask = _chunk_mask()
    slots = (slot0_ref, slot1_ref)

    def gather(t, s):
        _gather_rows(tab_ref, e_ref, t, slots[s])

    def compute(t, s):
        xh, xl = _split_bf16(xn_ref[t])
        d = lax.dot_general(jnp.concatenate([xh, xl], axis=0), _slot_rows(slots[s]), NT,
                            preferred_element_type=F32)
        e = (d[0:ROW_CHUNKS] + d[ROW_CHUNKS:2 * ROW_CHUNKS]) * mask
        drow_ref[pl.ds(t, 1), :] = jnp.sum(e, axis=0, keepdims=True)

    _two_slot_token_loop(tb, gather, compute)
    n = PEER_SLOTS * ROW_CHUNKS
    r = lax.broadcasted_iota(jnp.int32, (n, PEER_SLOTS), 0)
    c = lax.broadcasted_iota(jnp.int32, (n, PEER_SLOTS), 1)
    comp = (r // ROW_CHUNKS == c).astype(BF16)
    dp = _split3_bf16(drow_ref[...])
    dot = functools.partial(jnp.dot, preferred_element_type=F32)
    a = dot(dp[0], comp) + (dot(dp[1], comp) + dot(dp[2], comp))
    w_ref[...] = gate_ref[...] * _gelu(a)


def _peer_v_kernel(e_ref, w_ref, h_ref, tab_ref, o_ref, slot0_ref, slot1_ref, wexp_ref):
    tb = h_ref.shape[0]
    slots = (slot0_ref, slot1_ref)
    mask = _chunk_mask()
    n = PEER_SLOTS * ROW_CHUNKS
    r = lax.broadcasted_iota(jnp.int32, (PEER_SLOTS, n), 0)
    c = lax.broadcasted_iota(jnp.int32, (PEER_SLOTS, n), 1)
    expand = (c // ROW_CHUNKS == r).astype(BF16)
    wp = _split3_bf16(w_ref[...])
    dot = functools.partial(jnp.dot, preferred_element_type=F32)
    wexp_ref[...] = dot(wp[0], expand) + (dot(wp[1], expand) + dot(wp[2], expand))

    def gather(t, s):
        _gather_rows(tab_ref, e_ref, t, slots[s])

    def compute(t, s):
        wm = wexp_ref[pl.ds(t, 1), :] * mask
        wh, wl = _split_bf16(wm)
        d = jnp.dot(jnp.concatenate([wh, wl], axis=0), _slot_rows(slots[s]), preferred_element_type=F32)
        o_ref[t] = h_ref[t] + (d[0:ROW_CHUNKS] + d[ROW_CHUNKS:2 * ROW_CHUNKS])

    _two_slot_token_loop(tb, gather, compute)


def _table_spec(tab):
    return pl.BlockSpec(tab.shape, lambda i: (0, 0), pipeline_mode=pl.Buffered(1))


def _peer_u(e, xn3, gate, tab, tb):
    t = e.shape[0]
    n = PEER_SLOTS * ROW_CHUNKS
    return pl.pallas_call(
        _peer_u_kernel,
        grid=(t // tb,),
        in_specs=[pl.BlockSpec((tb, PEER_SLOTS), lambda i: (i, 0), memory_space=pltpu.SMEM),
                  pl.BlockSpec((tb, ROW_CHUNKS, LANES), lambda i: (i, 0, 0)),
                  pl.BlockSpec((tb, PEER_SLOTS), lambda i: (i, 0)),
                  _table_spec(tab)],
        out_specs=pl.BlockSpec((tb, PEER_SLOTS), lambda i: (i, 0)),
        out_shape=jax.ShapeDtypeStruct((t, PEER_SLOTS), F32),
        scratch_shapes=[pltpu.VMEM((PEER_SLOTS * ROW_WORDS, LANES), jnp.int32),
                        pltpu.VMEM((PEER_SLOTS * ROW_WORDS, LANES), jnp.int32), pltpu.VMEM((tb, n), F32)],
        compiler_params=_cparams(("arbitrary",)),
        name="peer_u",
    )(e, xn3, gate, tab)


def _peer_v(e, w, h3, tab, tb):
    t = e.shape[0]
    n = PEER_SLOTS * ROW_CHUNKS
    return pl.pallas_call(
        _peer_v_kernel,
        grid=(t // tb,),
        in_specs=[pl.BlockSpec((tb, PEER_SLOTS), lambda i: (i, 0), memory_space=pltpu.SMEM),
                  pl.BlockSpec((tb, PEER_SLOTS), lambda i: (i, 0)),
                  pl.BlockSpec((tb, ROW_CHUNKS, LANES), lambda i: (i, 0, 0)),
                  _table_spec(tab)],
        out_specs=pl.BlockSpec((tb, ROW_CHUNKS, LANES), lambda i: (i, 0, 0)),
        out_shape=jax.ShapeDtypeStruct((t, ROW_CHUNKS, LANES), F32),
        scratch_shapes=[pltpu.VMEM((PEER_SLOTS * ROW_WORDS, LANES), jnp.int32),
                        pltpu.VMEM((PEER_SLOTS * ROW_WORDS, LANES), jnp.int32), pltpu.VMEM((tb, n), F32)],
        compiler_params=_cparams(("arbitrary",)),
        name="peer_v",
    )(e, w, h3, tab)


def _ple_kernel(h_ref, p_ref, gple_ref, wg_ref, wp_ref, gfin_ref, y_ref):
    dot = functools.partial(jnp.dot, preferred_element_type=F32)
    h = h_ref[...]
    gate = _sigmoid(dot(_rms(h, gple_ref[...]).astype(BF16), wg_ref[...]))
    h = h + dot(p_ref[...].astype(BF16), wp_ref[...]) * gate
    y_ref[...] = _rms(h, gfin_ref[...])


def _ple(h, p, gple, wg, wp, gfin, tm):
    t = h.shape[0]
    full = lambda shape: pl.BlockSpec(shape, lambda i: (0, 0))
    row = lambda n: pl.BlockSpec((tm, n), lambda i: (i, 0))
    return pl.pallas_call(
        _ple_kernel,
        grid=(t // tm,),
        in_specs=[row(D_MODEL), row(PLE_DIM), full((1, D_MODEL)), full((D_MODEL, D_MODEL)),
                  full((PLE_DIM, D_MODEL)), full((1, D_MODEL))],
        out_specs=row(D_MODEL),
        out_shape=jax.ShapeDtypeStruct((t, D_MODEL), F32),
        compiler_params=_cparams(("parallel",)),
        name="ple_final",
    )(h, p, gple, wg, wp, gfin)


def _row_tile(n, want):
    t = want
    while n % t:
        t //= 2
    return t


def _prep_weights(w_in, conv_w, a_log, dt_bias, gdn_onorm, b_f, fox_onorm, w_out, norm_mix, norm_ffn,
                  peer_wq, peer_keys, peer_u, peer_v, norm_ple, w_ple_proj, w_ple_gate, norm_final):
    w_main = jnp.concatenate([w_in[:, :OFF1], w_in[:, OFF3:OFF4], w_in[:, OFF4:OFF5]], axis=1).astype(BF16)
    w_small = jnp.concatenate([w_in[:, OFF1:OFF3], w_in[:, OFF5:],
                               jnp.zeros((D_MODEL, LANES - 2 * H_A - H_B), w_in.dtype)], axis=1).astype(BF16)
    pad = lambda v, lo: jnp.pad(v.astype(F32), (lo, LANES - lo - v.shape[0])).reshape(1, LANES)
    bias = pad(dt_bias, H_A) + pad(b_f, 2 * H_A)
    alog = pad(a_log, H_A)
    r = lambda v: v.astype(F32).reshape(1, -1)
    return dict(
        w_main=w_main, w_small=w_small, bias=bias, alog=alog, conv_w=conv_w.astype(F32),
        gdn_onorm=r(gdn_onorm), fox_onorm=r(fox_onorm), w_out=w_out.astype(BF16),
        norm_mix=r(norm_mix), norm_ffn=r(norm_ffn), wq=peer_wq.astype(BF16), keys=peer_keys.astype(F32),
        utab=_pack_table(peer_u), vtab=_pack_table(peer_v), norm_ple=r(norm_ple),
        w_proj=w_ple_proj.astype(BF16), w_gate=w_ple_gate.astype(BF16), norm_final=r(norm_final))


def _layer(x, p, prefix, s0, past_k, past_v, past_lf, w):
    b, t, _ = x.shape
    n = b * t
    tm = _row_tile(n, 512)
    h = x.reshape(n, D_MODEL)
    conv_in, zg, qb, kb, vb, small = _inproj(h, w["norm_mix"], w["w_main"], w["w_small"], w["bias"], w["alog"], tm)
    conv3 = conv_in.reshape(b, t, CONV_DIM)
    small3 = small.reshape(b, t, LANES)
    prefix8 = jnp.zeros((b, SUBLANES, CONV_DIM), F32)
    if prefix is not None:
        prefix8 = prefix8.at[:, SUBLANES - (CONV_W - 1):, :].set(prefix.astype(F32))
    if s0 is None:
        s0 = jnp.zeros((b, H_A, DK_A, DV_A), F32)
    o_a, s_new = _gdn(conv3, small3, zg.reshape(b, t, A_WIDTH), prefix8, s0.astype(F32), w["conv_w"], w["gdn_onorm"],
                      nb=2 if b % 2 == 0 else 1)

    q3 = qb.reshape(b, t, B_WIDTH)
    k3 = kb.reshape(b, t, B_WIDTH)
    v3 = vb.reshape(b, t, B_WIDTH)
    if past_k is None:
        ccol, crow = _fox_cumsum(small3)
        o_b = _fox_prompt(q3, k3, v3, ccol, crow, w["fox_onorm"], _row_tile(t, 512))
    else:
        pl_len = past_k.shape[1]
        lf_past = jnp.pad(past_lf.astype(F32), ((0, 0), (0, 0), (2 * H_A, LANES - 2 * H_A - H_B)))
        ccol, crow = _fox_cumsum(jnp.concatenate([lf_past, small3], axis=1))
        k_all = jnp.concatenate([past_k.reshape(b, pl_len, B_WIDTH).astype(F32), k3], axis=1)
        v_all = jnp.concatenate([past_v.reshape(b, pl_len, B_WIDTH).astype(F32), v3], axis=1)
        o_b = _fox_sample(q3, k_all, v_all, ccol[:, pl_len:, :], crow, w["fox_onorm"])

    h1, xn, qp = _outproj(o_a.reshape(n, A_WIDTH), o_b.reshape(n, B_WIDTH), h, w["w_out"], w["norm_ffn"], w["wq"], tm)
    tb = _row_tile(n, 128)
    e, gate = _peer_route(qp, w["keys"], tb)
    wgt = _peer_u(e, xn.reshape(n, ROW_CHUNKS, LANES), gate, w["utab"], tb)
    h2 = _peer_v(e, wgt, h1.reshape(n, ROW_CHUNKS, LANES), w["vtab"], tb)
    y = _ple(h2.reshape(n, D_MODEL), p.reshape(n, PLE_DIM), w["norm_ple"], w["w_gate"], w["w_proj"],
             w["norm_final"], tm)
    conv_state = conv3[:, t - (CONV_W - 1):, :]
    return (y.reshape(b, t, D_MODEL), conv_state, s_new, k3.reshape(b, t, H_B, DH_B), v3.reshape(b, t, H_B, DH_B),
            small3[:, :, 2 * H_A:2 * H_A + H_B])


def kernel(x_prompt, x_sample, p_prompt, p_sample, cache_conv, state_gdn, cache_fox_k, cache_fox_v, cache_fox_logf, w_in, conv_w, a_log, dt_bias, gdn_onorm, b_f, fox_onorm, w_out, norm_mix, norm_ffn, peer_wq, peer_keys, peer_u, peer_v, norm_ple, w_ple_proj, w_ple_gate, norm_final):
    assert w_in.shape[0] == 1, "single-layer step"
    w = _prep_weights(w_in[0], conv_w[0], a_log[0], dt_bias[0], gdn_onorm[0], b_f[0], fox_onorm[0], w_out[0],
                      norm_mix[0], norm_ffn[0], peer_wq[0], peer_keys[0], peer_u[0], peer_v[0], norm_ple[0],
                      w_ple_proj[0], w_ple_gate[0], norm_final)
    ys, c2, s2, k2, v2, l2 = _layer(x_sample, p_sample[0], cache_conv[0], state_gdn[0], cache_fox_k[0],
                                    cache_fox_v[0], cache_fox_logf[0], w)
    yp, c1, s1, k1, v1, l1 = _layer(x_prompt, p_prompt[0], None, None, None, None, None, w)
    st = lambda a: a[None]
    return (yp, ys, st(c1), st(s1), st(k1), st(v1), st(l1), st(c2), st(s2), st(k2), st(v2), st(l2))
```

```python
import functools
import math

import jax
import jax.numpy as jnp
from jax import lax
from jax.experimental import pallas as pl
from jax.experimental.pallas import tpu as pltpu

F32 = jnp.float32
BF16 = jnp.bfloat16

D_MODEL = 1024
H_A, DK_A, DV_A = 4, 128, 128
CONV_W = 4
CONV_DIM = H_A * (2 * DK_A + DV_A)
A_WIDTH = H_A * DV_A
H_B, DH_B = 8, 64
B_WIDTH = H_B * DH_B
OFF1 = CONV_DIM
OFF2 = OFF1 + H_A
OFF3 = OFF2 + H_A
OFF4 = OFF3 + A_WIDTH
OFF5 = OFF4 + 3 * B_WIDTH
N_KEYS = 128
PEER_HEADS = 8
PEER_TOPK = 16
PEER_SLOTS = PEER_HEADS * PEER_TOPK
PLE_DIM = 256
EPS = 1e-6
GDN_CHUNK = 64

LANES = 128
SUBLANES = 8
VMEM_LIMIT = 56 * 1024 * 1024

NEG_BIG = -1e30


def _cparams(sem, vmem=VMEM_LIMIT):
    return pltpu.CompilerParams(dimension_semantics=sem, vmem_limit_bytes=vmem)


def _split_bf16(x):
    hi = x.astype(BF16)
    lo = (x - hi.astype(F32)).astype(BF16)
    return hi, lo


def _mm(a, b, passes=1, dims=(((1,), (0,)), ((), ()))):
    dg = functools.partial(lax.dot_general, dimension_numbers=dims, preferred_element_type=F32)
    if passes == 1:
        return dg(a.astype(BF16), b.astype(BF16))
    ah, al = _split_bf16(a)
    bh, bl = _split_bf16(b)
    return dg(ah, bh) + (dg(al, bh) + dg(ah, bl))


NT = (((1,), (1,)), ((), ()))


def _rms(x, g):
    return x * lax.rsqrt(jnp.mean(x * x, axis=-1, keepdims=True) + EPS) * g


def _softplus(x):
    return jnp.maximum(x, 0.0) + jnp.log1p(jnp.exp(-jnp.abs(x)))


def _sigmoid(x):
    return 1.0 / (1.0 + jnp.exp(-x))


def _inproj_kernel(h_ref, g_ref, w_ref, ws_ref, bias_ref, alog_ref,
                   conv_ref, gate_ref, qb_ref, kb_ref, vb_ref, small_ref):
    xn = _rms(h_ref[...], g_ref[...]).astype(BF16)
    dot = functools.partial(jnp.dot, preferred_element_type=F32)
    conv_ref[...] = dot(xn, w_ref[:, 0:CONV_DIM])
    gate_ref[...] = dot(xn, w_ref[:, CONV_DIM:CONV_DIM + A_WIDTH])
    o = CONV_DIM + A_WIDTH
    qb_ref[...] = (dot(xn, w_ref[:, o:o + B_WIDTH]) * (DH_B ** -0.5)).astype(BF16)
    kb_ref[...] = dot(xn, w_ref[:, o + B_WIDTH:o + 2 * B_WIDTH])
    vb_ref[...] = dot(xn, w_ref[:, o + 2 * B_WIDTH:o + 3 * B_WIDTH])
    z = dot(xn, ws_ref[...]) + bias_ref[...]
    lane = lax.broadcasted_iota(jnp.int32, z.shape, 1)
    small_ref[...] = jnp.where(lane < H_A, _sigmoid(z),
                               jnp.where(lane < 2 * H_A, -jnp.exp(alog_ref[...]) * _softplus(z), -_softplus(-z)))


def _inproj(h, g, w_main, w_small, bias, alog, tm):
    t = h.shape[0]
    nm = w_main.shape[1]
    full = lambda shape: pl.BlockSpec(shape, lambda i: (0, 0))
    row = lambda n: pl.BlockSpec((tm, n), lambda i: (i, 0))
    return pl.pallas_call(
        _inproj_kernel,
        grid=(t // tm,),
        in_specs=[row(D_MODEL), full((1, D_MODEL)), full((D_MODEL, nm)), full((D_MODEL, LANES)),
                  full((1, LANES)), full((1, LANES))],
        out_specs=[row(CONV_DIM), row(A_WIDTH), row(B_WIDTH), row(B_WIDTH), row(B_WIDTH), row(LANES)],
        out_shape=[jax.ShapeDtypeStruct((t, CONV_DIM), F32), jax.ShapeDtypeStruct((t, A_WIDTH), F32),
                   jax.ShapeDtypeStruct((t, B_WIDTH), BF16), jax.ShapeDtypeStruct((t, B_WIDTH), F32),
                   jax.ShapeDtypeStruct((t, B_WIDTH), F32), jax.ShapeDtypeStruct((t, LANES), F32)],
        compiler_params=_cparams(("parallel",)),
        name="inproj",
    )(h, g, w_main, w_small, bias, alog)


def _split3_bf16(x):
    h1 = x.astype(BF16)
    r = x - h1.astype(F32)
    h2 = r.astype(BF16)
    h3 = (r - h2.astype(F32)).astype(BF16)
    return h1, h2, h3


def _cumsum_kernel(lf_ref, ccol_ref, crow_ref, *, chunk):
    tk = lf_ref.shape[1]
    r = lax.broadcasted_iota(jnp.int32, (chunk, chunk), 0)
    c = lax.broadcasted_iota(jnp.int32, (chunk, chunk), 1)
    tril = (r >= c).astype(BF16)
    er = lax.broadcasted_iota(jnp.int32, (SUBLANES, LANES), 0)
    ec = lax.broadcasted_iota(jnp.int32, (SUBLANES, LANES), 1)
    pick = (ec == er + SUBLANES).astype(BF16)
    dot = functools.partial(jnp.dot, preferred_element_type=F32)
    carry = jnp.zeros((1, LANES), F32)
    for i in range(tk // chunk):
        sl = pl.ds(i * chunk, chunk)
        parts = _split3_bf16(lf_ref[0, sl, :])
        cs = dot(tril, parts[0]) + (dot(tril, parts[1]) + dot(tril, parts[2])) + carry
        ccol_ref[0, sl, :] = cs
        carry = cs[chunk - 1:chunk, :]
        cparts = _split3_bf16(cs)
        dg = functools.partial(lax.dot_general, dimension_numbers=NT, preferred_element_type=F32)
        crow_ref[0, :, sl] = dg(pick, cparts[0]) + (dg(pick, cparts[1]) + dg(pick, cparts[2]))


def _fox_cumsum(lf):
    b, tk, _ = lf.shape
    chunk = tk if tk <= 2048 else 512
    return pl.pallas_call(
        functools.partial(_cumsum_kernel, chunk=chunk),
        grid=(b,),
        in_specs=[pl.BlockSpec((1, tk, LANES), lambda i: (i, 0, 0))],
        out_specs=[pl.BlockSpec((1, tk, LANES), lambda i: (i, 0, 0)),
                   pl.BlockSpec((1, SUBLANES, tk), lambda i: (i, 0, 0))],
        out_shape=[jax.ShapeDtypeStruct((b, tk, LANES), F32), jax.ShapeDtypeStruct((b, SUBLANES, tk), F32)],
        compiler_params=_cparams(("parallel",)),
        name="fox_cumsum",
    )(lf)


GDN_PASSES = 3
TN = (((0,), (0,)), ((), ()))


def _unit_lower_inverses(ms, row, col):
    size = ms[0].shape[0]
    mm = functools.partial(_mm, passes=GDN_PASSES)
    eye = (row == col).astype(F32)
    diag = (row // 8) == (col // 8)
    ns = [jnp.where(diag, -m, 0.0) for m in ms]
    n2s = [mm(n, n) for n in ns]
    n4s = [mm(n2, n2) for n2 in n2s]
    ts = [eye + n for n in ns]
    ts = [t + mm(t, n2) for t, n2 in zip(ts, n2s)]
    ts = [t + mm(t, n4) for t, n4 in zip(ts, n4s)]
    blk = 8
    while blk < size:
        sel = ((row // (2 * blk)) == (col // (2 * blk))) & ((row // blk) != (col // blk))
        tls = [mm(t, jnp.where(sel, m, 0.0)) for t, m in zip(ts, ms)]
        ts = [t - mm(tl, t) for t, tl in zip(ts, tls)]
        blk *= 2
    return ts


def _gdn_kernel(u_ref, small_ref, zg_ref, prefix_ref, s0_ref, cw_ref, onorm_ref,
                o_ref, s_ref, ubuf_ref):
    L = GDN_CHUNK
    nb = u_ref.shape[0]
    mm = functools.partial(_mm, passes=GDN_PASSES)
    dot = functools.partial(jnp.dot, preferred_element_type=F32)
    dg = functools.partial(lax.dot_general, dimension_numbers=NT, preferred_element_type=F32)

    @pl.when(pl.program_id(1) == 0)
    def _():
        ubuf_ref[:, 0:SUBLANES, :] = prefix_ref[...]
        s_ref[...] = s0_ref[...]

    row = lax.broadcasted_iota(jnp.int32, (L, L), 0)
    col = lax.broadcasted_iota(jnp.int32, (L, L), 1)
    causal = row >= col
    strict = row > col
    tril = causal.astype(BF16)
    er = lax.broadcasted_iota(jnp.int32, (SUBLANES, LANES), 0)
    ec = lax.broadcasted_iota(jnp.int32, (SUBLANES, LANES), 1)
    pick = (ec == er + H_A).astype(BF16)

    cs, smalls, g_alls, g_rowss = [], [], [], []
    for b in range(nb):
        ubuf_ref[b, SUBLANES:SUBLANES + L, :] = u_ref[b]
        y = ubuf_ref[b, SUBLANES - 3:SUBLANES - 3 + L, :] * cw_ref[0:1, :]
        for j in range(1, CONV_W):
            y = y + ubuf_ref[b, SUBLANES - 3 + j:SUBLANES - 3 + j + L, :] * cw_ref[j:j + 1, :]
        tail = ubuf_ref[b, L:L + SUBLANES, :]
        ubuf_ref[b, 0:SUBLANES, :] = tail
        cs.append(y * _sigmoid(y))
        small = small_ref[b]
        sp = _split3_bf16(small)
        g_all = dot(tril, sp[0]) + (dot(tril, sp[1]) + dot(tril, sp[2]))
        gp = _split3_bf16(g_all)
        smalls.append(small)
        g_alls.append(g_all)
        g_rowss.append(dg(pick, gp[0]) + (dg(pick, gp[1]) + dg(pick, gp[2])))

    chains = [(b, h) for b in range(nb) for h in range(H_A)]
    qs, ks, vs, betas, gcols, decays, kbs = [], [], [], [], [], [], []
    for b, h in chains:
        c = cs[b]
        qh = c[:, h * DK_A:(h + 1) * DK_A]
        kh = c[:, H_A * DK_A + h * DK_A:H_A * DK_A + (h + 1) * DK_A]
        qs.append(qh * lax.rsqrt(jnp.sum(qh * qh, axis=-1, keepdims=True) + EPS) * (DK_A ** -0.5))
        kh = kh * lax.rsqrt(jnp.sum(kh * kh, axis=-1, keepdims=True) + EPS)
        ks.append(kh)
        vs.append(c[:, 2 * H_A * DK_A + h * DV_A:2 * H_A * DK_A + (h + 1) * DV_A])
        beta = smalls[b][:, h:h + 1]
        g_col = g_alls[b][:, H_A + h:H_A + h + 1]
        diff = g_col - g_rowss[b][h:h + 1, :]
        betas.append(beta)
        gcols.append(g_col)
        decays.append(jnp.where(causal, jnp.exp(jnp.where(causal, diff, 0.0)), 0.0))
        kbs.append(kh * beta)
    ms = [jnp.where(strict, mm(kb, k, dims=NT) * d, 0.0) for kb, k, d in zip(kbs, ks, decays)]
    ts = _unit_lower_inverses(ms, row, col)
    egs = [jnp.exp(g) for g in gcols]
    s_olds = [s_ref[b, h] for b, h in chains]
    sol_vs = [mm(t, v * beta) for t, v, beta in zip(ts, vs, betas)]
    sol_ks = [mm(t, kb * eg) for t, kb, eg in zip(ts, kbs, egs)]
    u_news = [sv - mm(sk, s) for sv, sk, s in zip(sol_vs, sol_ks, s_olds)]
    attns = [_mm(q, k, dims=NT) * d for q, k, d in zip(qs, ks, decays)]
    outs = [_mm(q * eg, s) + _mm(a, u) for q, eg, s, a, u in zip(qs, egs, s_olds, attns, u_news)]
    for (b, h), k, g, s, u, o in zip(chains, ks, gcols, s_olds, u_news, outs):
        g_last = g[L - 1:L, :]
        s_ref[b, h] = jnp.exp(g_last) * s + mm(k * jnp.exp(g_last - g), u, dims=TN)
        zgh = zg_ref[b, :, h * DV_A:(h + 1) * DV_A]
        o_ref[b, :, h * DV_A:(h + 1) * DV_A] = _rms(o, onorm_ref[...]) * (zgh * _sigmoid(zgh))


def _gdn(u, small, zg, prefix8, s0, conv_w, onorm, nb):
    b, t, _ = u.shape
    L = GDN_CHUNK
    tile = lambda n: pl.BlockSpec((nb, L, n), lambda i, j: (i, j, 0))
    return pl.pallas_call(
        _gdn_kernel,
        grid=(b // nb, t // L),
        in_specs=[tile(CONV_DIM), tile(LANES), tile(A_WIDTH),
                  pl.BlockSpec((nb, SUBLANES, CONV_DIM), lambda i, j: (i, 0, 0)),
                  pl.BlockSpec((nb, H_A, DK_A, DV_A), lambda i, j: (i, 0, 0, 0)),
                  pl.BlockSpec((CONV_W, CONV_DIM), lambda i, j: (0, 0)),
                  pl.BlockSpec((1, DV_A), lambda i, j: (0, 0))],
        out_specs=[tile(A_WIDTH), pl.BlockSpec((nb, H_A, DK_A, DV_A), lambda i, j: (i, 0, 0, 0))],
        out_shape=[jax.ShapeDtypeStruct((b, t, A_WIDTH), F32), jax.ShapeDtypeStruct((b, H_A, DK_A, DV_A), F32)],
        scratch_shapes=[pltpu.VMEM((nb, SUBLANES + L, CONV_DIM), F32)],
        compiler_params=_cparams(("parallel", "arbitrary")),
        name="gdn",
    )(u, small, zg, prefix8, s0, conv_w, onorm)


FOX_GROUP = LANES
FOX_XW = H_B * FOX_GROUP
FOX_C0 = DH_B
FOX_C1 = DH_B + 3


def _fox_prep_kernel(lf_ref, q_ref, k_ref, v_ref, qx_ref, kx_ref, vx_ref, carry_ref, place_ref):
    L = lf_ref.shape[1]
    dot = functools.partial(jnp.dot, preferred_element_type=F32)

    @pl.when(pl.program_id(1) == 0)
    def _():
        carry_ref[...] = jnp.zeros(carry_ref.shape, F32)
        r = lax.broadcasted_iota(jnp.int32, (B_WIDTH, FOX_XW), 0)
        c = lax.broadcasted_iota(jnp.int32, (B_WIDTH, FOX_XW), 1)
        d = c % FOX_GROUP
        place_ref[...] = ((d < DH_B) & (r == (c // FOX_GROUP) * DH_B + d)).astype(BF16)

    r = lax.broadcasted_iota(jnp.int32, (L, L), 0)
    c = lax.broadcasted_iota(jnp.int32, (L, L), 1)
    tril = (r >= c).astype(BF16)
    parts = _split3_bf16(lf_ref[0])
    cs = dot(tril, parts[0]) + (dot(tril, parts[1]) + dot(tril, parts[2])) + carry_ref[...]
    carry_ref[...] = cs[L - 1:L, :]
    cparts = _split3_bf16(cs)

    rr = lax.broadcasted_iota(jnp.int32, (LANES, FOX_XW), 0)
    cc = lax.broadcasted_iota(jnp.int32, (LANES, FOX_XW), 1)
    head_lane = rr == 2 * H_A + cc // FOX_GROUP
    dd = cc % FOX_GROUP
    lane1 = lax.broadcasted_iota(jnp.int32, (1, FOX_XW), 1) % FOX_GROUP
    qx = dot(q_ref[0], place_ref[...])
    kx = dot(k_ref[0].astype(BF16), place_ref[...])
    vx = dot(v_ref[0].astype(BF16), place_ref[...])
    for i in range(3):
        qx = qx + dot(cparts[i], (head_lane & (dd == FOX_C0 + i)).astype(BF16))
        kx = kx + dot(-cparts[i], (head_lane & (dd == FOX_C1 + i)).astype(BF16))
    qx = qx + ((lane1 >= FOX_C1) & (lane1 < FOX_C1 + 3)).astype(F32)
    kx = kx + ((lane1 >= FOX_C0) & (lane1 < FOX_C0 + 3)).astype(F32)
    vx = vx + (lane1 == DH_B).astype(F32)
    qx_ref[0] = qx.astype(BF16)
    kx_ref[0] = kx.astype(BF16)
    vx_ref[0] = vx.astype(BF16)


def _fox_prep(lf, q, k, v, tl):
    b, t, _ = q.shape
    blk = lambda w: pl.BlockSpec((1, tl, w), lambda i, j: (i, j, 0))
    xs = jax.ShapeDtypeStruct((b, t, FOX_XW), BF16)
    return pl.pallas_call(
        _fox_prep_kernel,
        grid=(b, t // tl),
        in_specs=[blk(LANES), blk(B_WIDTH), blk(B_WIDTH), blk(B_WIDTH)],
        out_specs=[blk(FOX_XW), blk(FOX_XW), blk(FOX_XW)],
        out_shape=[xs, xs, xs],
        scratch_shapes=[pltpu.VMEM((1, LANES), F32), pltpu.VMEM((B_WIDTH, FOX_XW), BF16)],
        compiler_params=_cparams(("parallel", "arbitrary")),
        name="fox_prep",
    )(lf, q, k, v)


def _fox_flash_kernel(qx_ref, kx_ref, vx_ref, onorm_ref, o_ref, m_ref, acc_ref):
    qi = pl.program_id(1)
    kj = pl.program_id(2)
    tq = qx_ref.shape[1]
    tk = kx_ref.shape[1]

    @pl.when(kj == 0)
    def _():
        m_ref[...] = jnp.full(m_ref.shape, NEG_BIG, F32)
        acc_ref[...] = jnp.zeros(acc_ref.shape, F32)

    def step(masked):
        if masked:
            row = lax.broadcasted_iota(jnp.int32, (tq, tk), 0)
            col = lax.broadcasted_iota(jnp.int32, (tq, tk), 1)
            keep = row >= col
        for h in range(H_B):
            hs = slice(h * FOX_GROUP, (h + 1) * FOX_GROUP)
            s = lax.dot_general(qx_ref[0, :, hs], kx_ref[0, :, hs], NT, preferred_element_type=F32)
            if masked:
                s = jnp.where(keep, s, NEG_BIG)
            m_old = m_ref[h]
            m_new = jnp.maximum(m_old, jnp.broadcast_to(jnp.max(s, axis=-1, keepdims=True), m_old.shape))
            p = jnp.concatenate([jnp.exp(s[:, j * LANES:(j + 1) * LANES] - m_new) for j in range(tk // LANES)],
                                axis=1)
            acc_ref[h] = jnp.exp(m_old - m_new) * acc_ref[h] + jnp.dot(p.astype(BF16), vx_ref[0, :, hs],
                                                                      preferred_element_type=F32)
            m_ref[h] = m_new

    @pl.when(kj < qi)
    def _():
        step(False)

    @pl.when(kj == qi)
    def _():
        step(True)
        for h in range(H_B):
            acc = acc_ref[h]
            o = acc[:, 0:DH_B] / acc[:, DH_B:DH_B + 1]
            o_ref[0, :, h * DH_B:(h + 1) * DH_B] = _rms(o, onorm_ref[...])


def _fox_prompt(qx, kx, vx, onorm, tq):
    b, t, _ = qx.shape
    n = t // tq
    qspec = lambda w: pl.BlockSpec((1, tq, w), lambda i, a, c: (i, a, 0))
    kspec = lambda w: pl.BlockSpec((1, tq, w), lambda i, a, c: (i, jnp.minimum(a, c), 0))
    return pl.pallas_call(
        _fox_flash_kernel,
        grid=(b, n, n),
        in_specs=[qspec(FOX_XW), kspec(FOX_XW), kspec(FOX_XW), pl.BlockSpec((1, DH_B), lambda i, a, c: (0, 0))],
        out_specs=qspec(B_WIDTH),
        out_shape=jax.ShapeDtypeStruct((b, t, B_WIDTH), F32),
        scratch_shapes=[pltpu.VMEM((H_B, tq, FOX_GROUP), F32), pltpu.VMEM((H_B, tq, FOX_GROUP), F32)],
        compiler_params=_cparams(("parallel", "parallel", "arbitrary")),
        name="fox_prompt",
    )(qx, kx, vx, onorm)


def _fox_sample_kernel(q_ref, k_ref, v_ref, ccol_ref, crow_ref, onorm_ref, o_ref):
    tq = q_ref.shape[1]
    tk = k_ref.shape[1]
    row = lax.broadcasted_iota(jnp.int32, (tq, tk), 0) + (tk - tq)
    col = lax.broadcasted_iota(jnp.int32, (tq, tk), 1)
    keep = row >= col
    for h in range(H_B):
        hs = slice(h * DH_B, (h + 1) * DH_B)
        s = lax.dot_general(q_ref[0, :, hs], k_ref[0, :, hs].astype(BF16), NT, preferred_element_type=F32)
        s = s + (ccol_ref[0, :, SUBLANES + h:SUBLANES + h + 1] - crow_ref[0, h:h + 1, :])
        s = jnp.where(keep, s, NEG_BIG)
        p = jnp.exp(s - jnp.max(s, axis=-1, keepdims=True))
        l = jnp.sum(p, axis=-1, keepdims=True)
        o = jnp.dot(p.astype(BF16), v_ref[0, :, hs].astype(BF16), preferred_element_type=F32) / l
        o_ref[0, :, hs] = _rms(o, onorm_ref[...])


def _fox_sample(q, k_all, v_all, ccol_q, crow, onorm):
    b, tq, _ = q.shape
    tk = k_all.shape[1]
    blk = lambda n, w: pl.BlockSpec((1, n, w), lambda i: (i, 0, 0))
    return pl.pallas_call(
        _fox_sample_kernel,
        grid=(b,),
        in_specs=[blk(tq, B_WIDTH), blk(tk, B_WIDTH), blk(tk, B_WIDTH), blk(tq, LANES), blk(SUBLANES, tk),
                  pl.BlockSpec((1, DH_B), lambda i: (0, 0))],
        out_specs=blk(tq, B_WIDTH),
        out_shape=jax.ShapeDtypeStruct((b, tq, B_WIDTH), F32),
        compiler_params=_cparams(("parallel",)),
        name="fox_sample",
    )(q, k_all, v_all, ccol_q, crow, onorm)


def _outproj_kernel(oa_ref, ob_ref, h_ref, wo_ref, g_ref, wq_ref, h1_ref, xn_ref, qp_ref):
    dot = functools.partial(jnp.dot, preferred_element_type=F32)
    h1 = h_ref[...] + dot(oa_ref[...].astype(BF16), wo_ref[0:A_WIDTH, :]) \
        + dot(ob_ref[...].astype(BF16), wo_ref[A_WIDTH:A_WIDTH + B_WIDTH, :])
    h1_ref[...] = h1
    xn = _rms(h1, g_ref[...])
    xn_ref[...] = xn
    qp_ref[...] = dot(xn.astype(BF16), wq_ref[...])


def _outproj(oa, ob, h, w_out, g, wq, tm):
    t = h.shape[0]
    nq = wq.shape[1]
    full = lambda shape: pl.BlockSpec(shape, lambda i: (0, 0))
    row = lambda n: pl.BlockSpec((tm, n), lambda i: (i, 0))
    return pl.pallas_call(
        _outproj_kernel,
        grid=(t // tm,),
        in_specs=[row(A_WIDTH), row(B_WIDTH), row(D_MODEL), full((D_MODEL, D_MODEL)), full((1, D_MODEL)),
                  full((D_MODEL, nq))],
        out_specs=[row(D_MODEL), row(D_MODEL), row(nq)],
        out_shape=[jax.ShapeDtypeStruct((t, D_MODEL), F32), jax.ShapeDtypeStruct((t, D_MODEL), F32),
                   jax.ShapeDtypeStruct((t, nq), F32)],
        compiler_params=_cparams(("parallel",)),
        name="outproj",
    )(oa, ob, h, w_out, g, wq)


def _topk_rows(s, k, payload=None):
    n = s.shape[0]
    rid = lax.broadcasted_iota(jnp.int32, s.shape, 0).astype(F32)
    vals, picks = [], []
    for _ in range(k):
        m = jnp.max(s, axis=0, keepdims=True)
        idx = jnp.min(jnp.where(s == m, rid, float(n)), axis=0, keepdims=True)
        hit = rid == idx
        vals.append(m)
        picks.append(idx if payload is None else jnp.sum(jnp.where(hit, payload, 0.0), axis=0, keepdims=True))
        s = jnp.where(hit, -jnp.inf, s)
    return jnp.concatenate(vals, axis=0), jnp.concatenate(picks, axis=0)


PEER_PAIRS = [(i, j) for i in range(PEER_TOPK) for j in range(PEER_TOPK) if (i + 1) * (j + 1) <= PEER_TOPK]
PEER_PAIR_ROWS = -(-len(PEER_PAIRS) // SUBLANES) * SUBLANES


def _pair_selectors():
    sel = [[[1.0 if (r < len(PEER_PAIRS) and PEER_PAIRS[r][side] == i) else 0.0 for i in range(PEER_TOPK)]
            for r in range(PEER_PAIR_ROWS)] for side in range(2)]
    return jnp.asarray(sel, BF16)


def _select_rows(sel, x, exact_f32):
    dot = functools.partial(jnp.dot, preferred_element_type=F32)
    if not exact_f32:
        return dot(sel, x.astype(BF16))
    p = _split3_bf16(x)
    return dot(sel, p[0]) + dot(sel, p[1]) + dot(sel, p[2])


def _peer_route_kernel(qp_ref, keys_ref, sel_ref, e_ref, gate_ref):
    tb = qp_ref.shape[0]
    dq = LANES
    k = PEER_TOPK
    valid = lax.broadcasted_iota(jnp.int32, (PEER_PAIR_ROWS, tb), 0) < len(PEER_PAIRS)
    es, gs = [], []
    for h in range(PEER_HEADS):
        sv, si = [], []
        for c in range(2):
            q = qp_ref[:, (2 * h + c) * dq:(2 * h + c + 1) * dq]
            st = _mm(keys_ref[c], q, passes=3, dims=NT)
            v, i = _topk_rows(st, k)
            sv.append(v)
            si.append(i)
        cand = _select_rows(sel_ref[0], sv[0], True) + _select_rows(sel_ref[1], sv[1], True)
        cand = jnp.where(valid, cand, -jnp.inf)
        ecand = (_select_rows(sel_ref[0], si[0], False) * N_KEYS + _select_rows(sel_ref[1], si[1], False)) * ROW_WORDS
        cv, e = _topk_rows(cand, k, payload=ecand)
        p = jnp.exp(cv - cv[0:1, :])
        gs.append(p / jnp.sum(p, axis=0, keepdims=True))
        es.append(e)
    e_all = jnp.concatenate(es, axis=0)
    g_all = jnp.concatenate(gs, axis=0)
    for j in range(tb // LANES):
        e_ref[j * LANES:(j + 1) * LANES, :] = e_all[:, j * LANES:(j + 1) * LANES].T.astype(jnp.int32)
        gate_ref[j * LANES:(j + 1) * LANES, :] = g_all[:, j * LANES:(j + 1) * LANES].T


def _peer_route(qp, keys, tb):
    t = qp.shape[0]
    sel = _pair_selectors()
    return pl.pallas_call(
        _peer_route_kernel,
        grid=(t // tb,),
        in_specs=[pl.BlockSpec((tb, qp.shape[1]), lambda i: (i, 0)),
                  pl.BlockSpec(keys.shape, lambda i: (0, 0, 0)),
                  pl.BlockSpec(sel.shape, lambda i: (0, 0, 0))],
        out_specs=[pl.BlockSpec((tb, PEER_SLOTS), lambda i: (i, 0)), pl.BlockSpec((tb, PEER_SLOTS), lambda i: (i, 0))],
        out_shape=[jax.ShapeDtypeStruct((t, PEER_SLOTS), jnp.int32), jax.ShapeDtypeStruct((t, PEER_SLOTS), F32)],
        compiler_params=_cparams(("parallel",)),
        name="peer_route",
    )(qp, keys, sel)


ROW_WORDS = D_MODEL // 2 // LANES
ROW_CHUNKS = D_MODEL // LANES
PEER_NSLOT_U = 4
PEER_NSLOT_V = 2


def _pack_kernel(t_ref, o_ref):
    o_ref[...] = pltpu.bitcast(t_ref[...].astype(BF16), jnp.int32)


def _pack_table(tab, blk=1024):
    n = tab.shape[0]
    return pl.pallas_call(
        _pack_kernel,
        grid=(n // blk,),
        in_specs=[pl.BlockSpec((blk * ROW_CHUNKS, LANES), lambda i: (i, 0))],
        out_specs=pl.BlockSpec((blk * ROW_WORDS, LANES), lambda i: (i, 0)),
        out_shape=jax.ShapeDtypeStruct((n * ROW_WORDS, LANES), jnp.int32),
        compiler_params=_cparams(("parallel",)),
        name="pack_table",
    )(tab.astype(F32).reshape(n * ROW_CHUNKS, LANES))


def _gather_rows(tab_ref, e_ref, t, slot_ref):
    ids = e_ref.at[t]
    for j in range(PEER_SLOTS):
        slot_ref[j * ROW_WORDS:(j + 1) * ROW_WORDS, :] = tab_ref[pl.ds(pl.multiple_of(ids[j], ROW_WORDS), ROW_WORDS), :]


def _slot_rows(slot_ref):
    return pltpu.bitcast(slot_ref[...], BF16)


def _slotted_token_loop(tb, nslot, gather, compute):
    for s in range(nslot):
        gather(s, s)

    def body(i, carry):
        t0 = nslot * i
        for s in range(nslot):
            compute(t0 + s, s)
            gather(jnp.minimum(t0 + nslot + s, tb - 1), s)
        return carry

    lax.fori_loop(0, tb // nslot, body, 0)


def _chunk_mask():
    n = PEER_SLOTS * ROW_CHUNKS
    r = lax.broadcasted_iota(jnp.int32, (ROW_CHUNKS, n), 0)
    c = lax.broadcasted_iota(jnp.int32, (ROW_CHUNKS, n), 1)
    return (c % ROW_CHUNKS == r).astype(F32)


def _gelu(a):
    return 0.5 * a * (1.0 + lax.erf(a * (2.0 ** -0.5)))


def _peer_u_kernel(e_ref, xn_ref, gate_ref, tab_ref, w_ref, slots_ref, drow_ref):
    tb = xn_ref.shape[0]
    mask = _chunk_mask()
    slots = [slots_ref.at[s] for s in range(slots_ref.shape[0])]

    def gather(t, s):
        _gather_rows(tab_ref, e_ref, t, slots[s])

    def compute(t, s):
        xh, xl = _split_bf16(xn_ref[t])
        d = lax.dot_general(jnp.concatenate([xh, xl], axis=0), _slot_rows(slots[s]), NT,
                            preferred_element_type=F32)
        e = (d[0:ROW_CHUNKS] + d[ROW_CHUNKS:2 * ROW_CHUNKS]) * mask
        drow_ref[pl.ds(t, 1), :] = jnp.sum(e, axis=0, keepdims=True)

    _slotted_token_loop(tb, len(slots), gather, compute)
    n = PEER_SLOTS * ROW_CHUNKS
    r = lax.broadcasted_iota(jnp.int32, (n, PEER_SLOTS), 0)
    c = lax.broadcasted_iota(jnp.int32, (n, PEER_SLOTS), 1)
    comp = (r // ROW_CHUNKS == c).astype(BF16)
    dp = _split3_bf16(drow_ref[...])
    dot = functools.partial(jnp.dot, preferred_element_type=F32)
    a = dot(dp[0], comp) + (dot(dp[1], comp) + dot(dp[2], comp))
    w_ref[...] = gate_ref[...] * _gelu(a)


def _peer_v_kernel(e_ref, w_ref, h_ref, tab_ref, o_ref, slots_ref, wexp_ref):
    tb = h_ref.shape[0]
    slots = [slots_ref.at[s] for s in range(slots_ref.shape[0])]
    mask = _chunk_mask()
    n = PEER_SLOTS * ROW_CHUNKS
    r = lax.broadcasted_iota(jnp.int32, (PEER_SLOTS, n), 0)
    c = lax.broadcasted_iota(jnp.int32, (PEER_SLOTS, n), 1)
    expand = (c // ROW_CHUNKS == r).astype(BF16)
    wp = _split3_bf16(w_ref[...])
    dot = functools.partial(jnp.dot, preferred_element_type=F32)
    wexp_ref[...] = dot(wp[0], expand) + (dot(wp[1], expand) + dot(wp[2], expand))

    def gather(t, s):
        _gather_rows(tab_ref, e_ref, t, slots[s])

    def compute(t, s):
        wm = wexp_ref[pl.ds(t, 1), :] * mask
        wh, wl = _split_bf16(wm)
        d = jnp.dot(jnp.concatenate([wh, wl], axis=0), _slot_rows(slots[s]), preferred_element_type=F32)
        o_ref[t] = h_ref[t] + (d[0:ROW_CHUNKS] + d[ROW_CHUNKS:2 * ROW_CHUNKS])

    _slotted_token_loop(tb, len(slots), gather, compute)


def _table_spec(tab):
    return pl.BlockSpec(tab.shape, lambda i: (0, 0), pipeline_mode=pl.Buffered(1))


def _peer_u(e, xn3, gate, tab, tb):
    t = e.shape[0]
    n = PEER_SLOTS * ROW_CHUNKS
    return pl.pallas_call(
        _peer_u_kernel,
        grid=(t // tb,),
        in_specs=[pl.BlockSpec((tb, PEER_SLOTS), lambda i: (i, 0), memory_space=pltpu.SMEM),
                  pl.BlockSpec((tb, ROW_CHUNKS, LANES), lambda i: (i, 0, 0)),
                  pl.BlockSpec((tb, PEER_SLOTS), lambda i: (i, 0)),
                  _table_spec(tab)],
        out_specs=pl.BlockSpec((tb, PEER_SLOTS), lambda i: (i, 0)),
        out_shape=jax.ShapeDtypeStruct((t, PEER_SLOTS), F32),
        scratch_shapes=[pltpu.VMEM((PEER_NSLOT_U, PEER_SLOTS * ROW_WORDS, LANES), jnp.int32), pltpu.VMEM((tb, n), F32)],
        compiler_params=_cparams(("arbitrary",)),
        name="peer_u",
    )(e, xn3, gate, tab)


def _peer_v(e, w, h3, tab, tb):
    t = e.shape[0]
    n = PEER_SLOTS * ROW_CHUNKS
    return pl.pallas_call(
        _peer_v_kernel,
        grid=(t // tb,),
        in_specs=[pl.BlockSpec((tb, PEER_SLOTS), lambda i: (i, 0), memory_space=pltpu.SMEM),
                  pl.BlockSpec((tb, PEER_SLOTS), lambda i: (i, 0)),
                  pl.BlockSpec((tb, ROW_CHUNKS, LANES), lambda i: (i, 0, 0)),
                  _table_spec(tab)],
        out_specs=pl.BlockSpec((tb, ROW_CHUNKS, LANES), lambda i: (i, 0, 0)),
        out_shape=jax.ShapeDtypeStruct((t, ROW_CHUNKS, LANES), F32),
        scratch_shapes=[pltpu.VMEM((PEER_NSLOT_V, PEER_SLOTS * ROW_WORDS, LANES), jnp.int32), pltpu.VMEM((tb, n), F32)],
        compiler_params=_cparams(("arbitrary",)),
        name="peer_v",
    )(e, w, h3, tab)


def _ple_kernel(h_ref, p_ref, gple_ref, wg_ref, wp_ref, gfin_ref, y_ref):
    dot = functools.partial(jnp.dot, preferred_element_type=F32)
    h = h_ref[...]
    gate = _sigmoid(dot(_rms(h, gple_ref[...]).astype(BF16), wg_ref[...]))
    h = h + dot(p_ref[...].astype(BF16), wp_ref[...]) * gate
    y_ref[...] = _rms(h, gfin_ref[...])


def _ple(h, p, gple, wg, wp, gfin, tm):
    t = h.shape[0]
    full = lambda shape: pl.BlockSpec(shape, lambda i: (0, 0))
    row = lambda n: pl.BlockSpec((tm, n), lambda i: (i, 0))
    return pl.pallas_call(
        _ple_kernel,
        grid=(t // tm,),
        in_specs=[row(D_MODEL), row(PLE_DIM), full((1, D_MODEL)), full((D_MODEL, D_MODEL)),
                  full((PLE_DIM, D_MODEL)), full((1, D_MODEL))],
        out_specs=row(D_MODEL),
        out_shape=jax.ShapeDtypeStruct((t, D_MODEL), F32),
        compiler_params=_cparams(("parallel",)),
        name="ple_final",
    )(h, p, gple, wg, wp, gfin)


def _row_tile(n, want):
    t = want
    while n % t:
        t //= 2
    return t


def _prep_weights(w_in, conv_w, a_log, dt_bias, gdn_onorm, b_f, fox_onorm, w_out, norm_mix, norm_ffn,
                  peer_wq, peer_keys, peer_u, peer_v, norm_ple, w_ple_proj, w_ple_gate, norm_final):
    w_main = jnp.concatenate([w_in[:, :OFF1], w_in[:, OFF3:OFF4], w_in[:, OFF4:OFF5]], axis=1).astype(BF16)
    w_small = jnp.concatenate([w_in[:, OFF1:OFF3], w_in[:, OFF5:],
                               jnp.zeros((D_MODEL, LANES - 2 * H_A - H_B), w_in.dtype)], axis=1).astype(BF16)
    pad = lambda v, lo: jnp.pad(v.astype(F32), (lo, LANES - lo - v.shape[0])).reshape(1, LANES)
    bias = pad(dt_bias, H_A) + pad(b_f, 2 * H_A)
    alog = pad(a_log, H_A)
    r = lambda v: v.astype(F32).reshape(1, -1)
    return dict(
        w_main=w_main, w_small=w_small, bias=bias, alog=alog, conv_w=conv_w.astype(F32),
        gdn_onorm=r(gdn_onorm), fox_onorm=r(fox_onorm), w_out=w_out.astype(BF16),
        norm_mix=r(norm_mix), norm_ffn=r(norm_ffn), wq=peer_wq.astype(BF16), keys=peer_keys.astype(F32),
        utab=_pack_table(peer_u), vtab=_pack_table(peer_v), norm_ple=r(norm_ple),
        w_proj=w_ple_proj.astype(BF16), w_gate=w_ple_gate.astype(BF16), norm_final=r(norm_final))


def _layer(x, p, prefix, s0, past_k, past_v, past_lf, w):
    b, t, _ = x.shape
    n = b * t
    tm = _row_tile(n, 512)
    h = x.reshape(n, D_MODEL)
    conv_in, zg, qb, kb, vb, small = _inproj(h, w["norm_mix"], w["w_main"], w["w_small"], w["bias"], w["alog"], tm)
    conv3 = conv_in.reshape(b, t, CONV_DIM)
    small3 = small.reshape(b, t, LANES)
    prefix8 = jnp.zeros((b, SUBLANES, CONV_DIM), F32)
    if prefix is not None:
        prefix8 = prefix8.at[:, SUBLANES - (CONV_W - 1):, :].set(prefix.astype(F32))
    if s0 is None:
        s0 = jnp.zeros((b, H_A, DK_A, DV_A), F32)
    o_a, s_new = _gdn(conv3, small3, zg.reshape(b, t, A_WIDTH), prefix8, s0.astype(F32), w["conv_w"], w["gdn_onorm"],
                      nb=2 if b % 2 == 0 else 1)

    q3 = qb.reshape(b, t, B_WIDTH)
    k3 = kb.reshape(b, t, B_WIDTH)
    v3 = vb.reshape(b, t, B_WIDTH)
    if past_k is None:
        tq = _row_tile(t, 512)
        qx, kx, vx = _fox_prep(small3, q3, k3, v3, tq)
        o_b = _fox_prompt(qx, kx, vx, w["fox_onorm"], tq)
    else:
        pl_len = past_k.shape[1]
        lf_past = jnp.pad(past_lf.astype(F32), ((0, 0), (0, 0), (2 * H_A, LANES - 2 * H_A - H_B)))
        ccol, crow = _fox_cumsum(jnp.concatenate([lf_past, small3], axis=1))
        k_all = jnp.concatenate([past_k.reshape(b, pl_len, B_WIDTH).astype(F32), k3], axis=1)
        v_all = jnp.concatenate([past_v.reshape(b, pl_len, B_WIDTH).astype(F32), v3], axis=1)
        o_b = _fox_sample(q3, k_all, v_all, ccol[:, pl_len:, :], crow, w["fox_onorm"])

    h1, xn, qp = _outproj(o_a.reshape(n, A_WIDTH), o_b.reshape(n, B_WIDTH), h, w["w_out"], w["norm_ffn"], w["wq"], tm)
    tb = _row_tile(n, 128)
    e, gate = _peer_route(qp, w["keys"], tb)
    wgt = _peer_u(e, xn.reshape(n, ROW_CHUNKS, LANES), gate, w["utab"], tb)
    h2 = _peer_v(e, wgt, h1.reshape(n, ROW_CHUNKS, LANES), w["vtab"], tb)
    y = _ple(h2.reshape(n, D_MODEL), p.reshape(n, PLE_DIM), w["norm_ple"], w["w_gate"], w["w_proj"],
             w["norm_final"], tm)
    conv_state = conv3[:, t - (CONV_W - 1):, :]
    return (y.reshape(b, t, D_MODEL), conv_state, s_new, k3.reshape(b, t, H_B, DH_B), v3.reshape(b, t, H_B, DH_B),
            small3[:, :, 2 * H_A:2 * H_A + H_B])


def kernel(x_prompt, x_sample, p_prompt, p_sample, cache_conv, state_gdn, cache_fox_k, cache_fox_v, cache_fox_logf, w_in, conv_w, a_log, dt_bias, gdn_onorm, b_f, fox_onorm, w_out, norm_mix, norm_ffn, peer_wq, peer_keys, peer_u, peer_v, norm_ple, w_ple_proj, w_ple_gate, norm_final):
    assert w_in.shape[0] == 1, "single-layer step"
    w = _prep_weights(w_in[0], conv_w[0], a_log[0], dt_bias[0], gdn_onorm[0], b_f[0], fox_onorm[0], w_out[0],
                      norm_mix[0], norm_ffn[0], peer_wq[0], peer_keys[0], peer_u[0], peer_v[0], norm_ple[0],
                      w_ple_proj[0], w_ple_gate[0], norm_final)
    ys, c2, s2, k2, v2, l2 = _layer(x_sample, p_sample[0], cache_conv[0], state_gdn[0], cache_fox_k[0],
                                    cache_fox_v[0], cache_fox_logf[0], w)
    yp, c1, s1, k1, v1, l1 = _layer(x_prompt, p_prompt[0], None, None, None, None, None, w)
    st = lambda a: a[None]
    return (yp, ys, st(c1), st(s1), st(k1), st(v1), st(l1), st(c2), st(s2), st(k2), st(v2), st(l2))
```

```python
import functools
import math

import jax
import jax.numpy as jnp
from jax import lax
from jax.experimental import pallas as pl
from jax.experimental.pallas import tpu as pltpu

F32 = jnp.float32
BF16 = jnp.bfloat16

D_MODEL = 1024
H_A, DK_A, DV_A = 4, 128, 128
CONV_W = 4
CONV_DIM = H_A * (2 * DK_A + DV_A)
A_WIDTH = H_A * DV_A
H_B, DH_B = 8, 64
B_WIDTH = H_B * DH_B
OFF1 = CONV_DIM
OFF2 = OFF1 + H_A
OFF3 = OFF2 + H_A
OFF4 = OFF3 + A_WIDTH
OFF5 = OFF4 + 3 * B_WIDTH
N_KEYS = 128
PEER_HEADS = 8
PEER_TOPK = 16
PEER_SLOTS = PEER_HEADS * PEER_TOPK
PLE_DIM = 256
EPS = 1e-6
GDN_CHUNK = 64

LANES = 128
SUBLANES = 8
VMEM_LIMIT = 56 * 1024 * 1024

NEG_BIG = -1e30


def _cparams(sem, vmem=VMEM_LIMIT):
    return pltpu.CompilerParams(dimension_semantics=sem, vmem_limit_bytes=vmem)


def _split_bf16(x):
    hi = x.astype(BF16)
    lo = (x - hi.astype(F32)).astype(BF16)
    return hi, lo


def _mm(a, b, passes=1, dims=(((1,), (0,)), ((), ()))):
    dg = functools.partial(lax.dot_general, dimension_numbers=dims, preferred_element_type=F32)
    if passes == 1:
        return dg(a.astype(BF16), b.astype(BF16))
    ah, al = _split_bf16(a)
    bh, bl = _split_bf16(b)
    return dg(ah, bh) + (dg(al, bh) + dg(ah, bl))


NT = (((1,), (1,)), ((), ()))


def _rms(x, g):
    return x * lax.rsqrt(jnp.mean(x * x, axis=-1, keepdims=True) + EPS) * g


def _softplus(x):
    return jnp.maximum(x, 0.0) + jnp.log1p(jnp.exp(-jnp.abs(x)))


def _sigmoid(x):
    return 1.0 / (1.0 + jnp.exp(-x))


def _inproj_kernel(h_ref, g_ref, w_ref, ws_ref, bias_ref, alog_ref,
                   conv_ref, gate_ref, qb_ref, kb_ref, vb_ref, small_ref):
    xn = _rms(h_ref[...], g_ref[...]).astype(BF16)
    dot = functools.partial(jnp.dot, preferred_element_type=F32)
    conv_ref[...] = dot(xn, w_ref[:, 0:CONV_DIM])
    gate_ref[...] = dot(xn, w_ref[:, CONV_DIM:CONV_DIM + A_WIDTH])
    o = CONV_DIM + A_WIDTH
    qb_ref[...] = (dot(xn, w_ref[:, o:o + B_WIDTH]) * (DH_B ** -0.5)).astype(BF16)
    kb_ref[...] = dot(xn, w_ref[:, o + B_WIDTH:o + 2 * B_WIDTH])
    vb_ref[...] = dot(xn, w_ref[:, o + 2 * B_WIDTH:o + 3 * B_WIDTH])
    z = dot(xn, ws_ref[...]) + bias_ref[...]
    lane = lax.broadcasted_iota(jnp.int32, z.shape, 1)
    small_ref[...] = jnp.where(lane < H_A, _sigmoid(z),
                               jnp.where(lane < 2 * H_A, -jnp.exp(alog_ref[...]) * _softplus(z), -_softplus(-z)))


def _inproj(h, g, w_main, w_small, bias, alog, tm):
    t = h.shape[0]
    nm = w_main.shape[1]
    full = lambda shape: pl.BlockSpec(shape, lambda i: (0, 0))
    row = lambda n: pl.BlockSpec((tm, n), lambda i: (i, 0))
    return pl.pallas_call(
        _inproj_kernel,
        grid=(t // tm,),
        in_specs=[row(D_MODEL), full((1, D_MODEL)), full((D_MODEL, nm)), full((D_MODEL, LANES)),
                  full((1, LANES)), full((1, LANES))],
        out_specs=[row(CONV_DIM), row(A_WIDTH), row(B_WIDTH), row(B_WIDTH), row(B_WIDTH), row(LANES)],
        out_shape=[jax.ShapeDtypeStruct((t, CONV_DIM), F32), jax.ShapeDtypeStruct((t, A_WIDTH), F32),
                   jax.ShapeDtypeStruct((t, B_WIDTH), BF16), jax.ShapeDtypeStruct((t, B_WIDTH), F32),
                   jax.ShapeDtypeStruct((t, B_WIDTH), F32), jax.ShapeDtypeStruct((t, LANES), F32)],
        compiler_params=_cparams(("parallel",)),
        name="inproj",
    )(h, g, w_main, w_small, bias, alog)


def _split3_bf16(x):
    h1 = x.astype(BF16)
    r = x - h1.astype(F32)
    h2 = r.astype(BF16)
    h3 = (r - h2.astype(F32)).astype(BF16)
    return h1, h2, h3


def _cumsum_kernel(lf_ref, ccol_ref, crow_ref, *, chunk):
    tk = lf_ref.shape[1]
    r = lax.broadcasted_iota(jnp.int32, (chunk, chunk), 0)
    c = lax.broadcasted_iota(jnp.int32, (chunk, chunk), 1)
    tril = (r >= c).astype(BF16)
    er = lax.broadcasted_iota(jnp.int32, (SUBLANES, LANES), 0)
    ec = lax.broadcasted_iota(jnp.int32, (SUBLANES, LANES), 1)
    pick = (ec == er + SUBLANES).astype(BF16)
    dot = functools.partial(jnp.dot, preferred_element_type=F32)
    carry = jnp.zeros((1, LANES), F32)
    for i in range(tk // chunk):
        sl = pl.ds(i * chunk, chunk)
        parts = _split3_bf16(lf_ref[0, sl, :])
        cs = dot(tril, parts[0]) + (dot(tril, parts[1]) + dot(tril, parts[2])) + carry
        ccol_ref[0, sl, :] = cs
        carry = cs[chunk - 1:chunk, :]
        cparts = _split3_bf16(cs)
        dg = functools.partial(lax.dot_general, dimension_numbers=NT, preferred_element_type=F32)
        crow_ref[0, :, sl] = dg(pick, cparts[0]) + (dg(pick, cparts[1]) + dg(pick, cparts[2]))


def _fox_cumsum(lf):
    b, tk, _ = lf.shape
    chunk = tk if tk <= 2048 else 512
    return pl.pallas_call(
        functools.partial(_cumsum_kernel, chunk=chunk),
        grid=(b,),
        in_specs=[pl.BlockSpec((1, tk, LANES), lambda i: (i, 0, 0))],
        out_specs=[pl.BlockSpec((1, tk, LANES), lambda i: (i, 0, 0)),
                   pl.BlockSpec((1, SUBLANES, tk), lambda i: (i, 0, 0))],
        out_shape=[jax.ShapeDtypeStruct((b, tk, LANES), F32), jax.ShapeDtypeStruct((b, SUBLANES, tk), F32)],
        compiler_params=_cparams(("parallel",)),
        name="fox_cumsum",
    )(lf)


GDN_PASSES = 1
GDN_STATE_PASSES = 3
TN = (((0,), (0,)), ((), ()))


def _unit_lower_inverses(ms, row, col):
    size = ms[0].shape[0]
    mm = functools.partial(_mm, passes=GDN_PASSES)
    eye = (row == col).astype(F32)
    diag = (row // 8) == (col // 8)
    ns = [jnp.where(diag, -m, 0.0) for m in ms]
    n2s = [mm(n, n) for n in ns]
    n4s = [mm(n2, n2) for n2 in n2s]
    ts = [eye + n for n in ns]
    ts = [t + mm(t, n2) for t, n2 in zip(ts, n2s)]
    ts = [t + mm(t, n4) for t, n4 in zip(ts, n4s)]
    blk = 8
    while blk < size:
        sel = ((row // (2 * blk)) == (col // (2 * blk))) & ((row // blk) != (col // blk))
        tls = [mm(t, jnp.where(sel, m, 0.0)) for t, m in zip(ts, ms)]
        ts = [t - mm(tl, t) for t, tl in zip(ts, tls)]
        blk *= 2
    return ts


def _gdn_kernel(u_ref, small_ref, zg_ref, prefix_ref, s0_ref, cw_ref, onorm_ref,
                o_ref, s_ref, ubuf_ref):
    L = GDN_CHUNK
    nb = u_ref.shape[0]
    mm = functools.partial(_mm, passes=GDN_PASSES)
    dot = functools.partial(jnp.dot, preferred_element_type=F32)
    dg = functools.partial(lax.dot_general, dimension_numbers=NT, preferred_element_type=F32)

    @pl.when(pl.program_id(1) == 0)
    def _():
        ubuf_ref[:, 0:SUBLANES, :] = prefix_ref[...]
        s_ref[...] = s0_ref[...]

    row = lax.broadcasted_iota(jnp.int32, (L, L), 0)
    col = lax.broadcasted_iota(jnp.int32, (L, L), 1)
    causal = row >= col
    strict = row > col
    tril = causal.astype(BF16)
    er = lax.broadcasted_iota(jnp.int32, (SUBLANES, LANES), 0)
    ec = lax.broadcasted_iota(jnp.int32, (SUBLANES, LANES), 1)
    pick = (ec == er + H_A).astype(BF16)

    cs, smalls, g_alls, g_rowss = [], [], [], []
    for b in range(nb):
        ubuf_ref[b, SUBLANES:SUBLANES + L, :] = u_ref[b]
        y = ubuf_ref[b, SUBLANES - 3:SUBLANES - 3 + L, :] * cw_ref[0:1, :]
        for j in range(1, CONV_W):
            y = y + ubuf_ref[b, SUBLANES - 3 + j:SUBLANES - 3 + j + L, :] * cw_ref[j:j + 1, :]
        tail = ubuf_ref[b, L:L + SUBLANES, :]
        ubuf_ref[b, 0:SUBLANES, :] = tail
        cs.append(y * _sigmoid(y))
        small = small_ref[b]
        sp = _split3_bf16(small)
        g_all = dot(tril, sp[0]) + (dot(tril, sp[1]) + dot(tril, sp[2]))
        gp = _split3_bf16(g_all)
        smalls.append(small)
        g_alls.append(g_all)
        g_rowss.append(dg(pick, gp[0]) + (dg(pick, gp[1]) + dg(pick, gp[2])))

    chains = [(b, h) for b in range(nb) for h in range(H_A)]
    qs, ks, vs, betas, gcols, decays, kbs = [], [], [], [], [], [], []
    for b, h in chains:
        c = cs[b]
        qh = c[:, h * DK_A:(h + 1) * DK_A]
        kh = c[:, H_A * DK_A + h * DK_A:H_A * DK_A + (h + 1) * DK_A]
        qs.append(qh * lax.rsqrt(jnp.sum(qh * qh, axis=-1, keepdims=True) + EPS) * (DK_A ** -0.5))
        kh = kh * lax.rsqrt(jnp.sum(kh * kh, axis=-1, keepdims=True) + EPS)
        ks.append(kh)
        vs.append(c[:, 2 * H_A * DK_A + h * DV_A:2 * H_A * DK_A + (h + 1) * DV_A])
        beta = smalls[b][:, h:h + 1]
        g_col = g_alls[b][:, H_A + h:H_A + h + 1]
        diff = g_col - g_rowss[b][h:h + 1, :]
        betas.append(beta)
        gcols.append(g_col)
        decays.append(jnp.where(causal, jnp.exp(jnp.where(causal, diff, 0.0)), 0.0))
        kbs.append(kh * beta)
    ms = [jnp.where(strict, mm(kb, k, dims=NT) * d, 0.0) for kb, k, d in zip(kbs, ks, decays)]
    ts = _unit_lower_inverses(ms, row, col)
    egs = [jnp.exp(g) for g in gcols]
    s_olds = [s_ref[b, h] for b, h in chains]
    sol_vs = [mm(t, v * beta) for t, v, beta in zip(ts, vs, betas)]
    sol_ks = [mm(t, kb * eg) for t, kb, eg in zip(ts, kbs, egs)]
    mm_state = functools.partial(_mm, passes=GDN_STATE_PASSES)
    u_news = [sv - mm_state(sk, s) for sv, sk, s in zip(sol_vs, sol_ks, s_olds)]
    attns = [_mm(q, k, dims=NT) * d for q, k, d in zip(qs, ks, decays)]
    outs = [_mm(q * eg, s) + _mm(a, u) for q, eg, s, a, u in zip(qs, egs, s_olds, attns, u_news)]
    for (b, h), k, g, s, u, o in zip(chains, ks, gcols, s_olds, u_news, outs):
        g_last = g[L - 1:L, :]
        s_ref[b, h] = jnp.exp(g_last) * s + mm_state(k * jnp.exp(g_last - g), u, dims=TN)
        zgh = zg_ref[b, :, h * DV_A:(h + 1) * DV_A]
        o_ref[b, :, h * DV_A:(h + 1) * DV_A] = _rms(o, onorm_ref[...]) * (zgh * _sigmoid(zgh))


def _gdn(u, small, zg, prefix8, s0, conv_w, onorm, nb):
    b, t, _ = u.shape
    L = GDN_CHUNK
    tile = lambda n: pl.BlockSpec((nb, L, n), lambda i, j: (i, j, 0))
    return pl.pallas_call(
        _gdn_kernel,
        grid=(b // nb, t // L),
        in_specs=[tile(CONV_DIM), tile(LANES), tile(A_WIDTH),
                  pl.BlockSpec((nb, SUBLANES, CONV_DIM), lambda i, j: (i, 0, 0)),
                  pl.BlockSpec((nb, H_A, DK_A, DV_A), lambda i, j: (i, 0, 0, 0)),
                  pl.BlockSpec((CONV_W, CONV_DIM), lambda i, j: (0, 0)),
                  pl.BlockSpec((1, DV_A), lambda i, j: (0, 0))],
        out_specs=[tile(A_WIDTH), pl.BlockSpec((nb, H_A, DK_A, DV_A), lambda i, j: (i, 0, 0, 0))],
        out_shape=[jax.ShapeDtypeStruct((b, t, A_WIDTH), F32), jax.ShapeDtypeStruct((b, H_A, DK_A, DV_A), F32)],
        scratch_shapes=[pltpu.VMEM((nb, SUBLANES + L, CONV_DIM), F32)],
        compiler_params=_cparams(("parallel", "arbitrary")),
        name="gdn",
    )(u, small, zg, prefix8, s0, conv_w, onorm)


FOX_GROUP = LANES
FOX_XW = H_B * FOX_GROUP
FOX_C0 = DH_B
FOX_C1 = DH_B + 3
FOX_HEAD_GROUP = 4


def _fox_prep_kernel(lf_ref, q_ref, k_ref, v_ref, qx_ref, kx_ref, vx_ref, carry_ref, place_ref):
    L = lf_ref.shape[1]
    dot = functools.partial(jnp.dot, preferred_element_type=F32)

    @pl.when(pl.program_id(1) == 0)
    def _():
        carry_ref[...] = jnp.zeros(carry_ref.shape, F32)
        r = lax.broadcasted_iota(jnp.int32, (B_WIDTH, FOX_XW), 0)
        c = lax.broadcasted_iota(jnp.int32, (B_WIDTH, FOX_XW), 1)
        d = c % FOX_GROUP
        place_ref[...] = ((d < DH_B) & (r == (c // FOX_GROUP) * DH_B + d)).astype(BF16)

    r = lax.broadcasted_iota(jnp.int32, (L, L), 0)
    c = lax.broadcasted_iota(jnp.int32, (L, L), 1)
    tril = (r >= c).astype(BF16)
    parts = _split3_bf16(lf_ref[0])
    cs = dot(tril, parts[0]) + (dot(tril, parts[1]) + dot(tril, parts[2])) + carry_ref[...]
    carry_ref[...] = cs[L - 1:L, :]
    cparts = _split3_bf16(cs)

    rr = lax.broadcasted_iota(jnp.int32, (LANES, FOX_XW), 0)
    cc = lax.broadcasted_iota(jnp.int32, (LANES, FOX_XW), 1)
    head_lane = rr == 2 * H_A + cc // FOX_GROUP
    dd = cc % FOX_GROUP
    lane1 = lax.broadcasted_iota(jnp.int32, (1, FOX_XW), 1) % FOX_GROUP
    qx = dot(q_ref[0], place_ref[...])
    kx = dot(k_ref[0].astype(BF16), place_ref[...])
    vx = dot(v_ref[0].astype(BF16), place_ref[...])
    for i in range(3):
        qx = qx + dot(cparts[i], (head_lane & (dd == FOX_C0 + i)).astype(BF16))
        kx = kx + dot(-cparts[i], (head_lane & (dd == FOX_C1 + i)).astype(BF16))
    qx = qx + ((lane1 >= FOX_C1) & (lane1 < FOX_C1 + 3)).astype(F32)
    kx = kx + ((lane1 >= FOX_C0) & (lane1 < FOX_C0 + 3)).astype(F32)
    vx = vx + (lane1 == DH_B).astype(F32)
    qx_ref[0] = qx.astype(BF16)
    kx_ref[0] = kx.astype(BF16)
    vx_ref[0] = vx.astype(BF16)


def _fox_prep(lf, q, k, v, tl):
    b, t, _ = q.shape
    blk = lambda w: pl.BlockSpec((1, tl, w), lambda i, j: (i, j, 0))
    xs = jax.ShapeDtypeStruct((b, t, FOX_XW), BF16)
    return pl.pallas_call(
        _fox_prep_kernel,
        grid=(b, t // tl),
        in_specs=[blk(LANES), blk(B_WIDTH), blk(B_WIDTH), blk(B_WIDTH)],
        out_specs=[blk(FOX_XW), blk(FOX_XW), blk(FOX_XW)],
        out_shape=[xs, xs, xs],
        scratch_shapes=[pltpu.VMEM((1, LANES), F32), pltpu.VMEM((B_WIDTH, FOX_XW), BF16)],
        compiler_params=_cparams(("parallel", "arbitrary")),
        name="fox_prep",
    )(lf, q, k, v)


def _fox_flash_kernel(qx_ref, kx_ref, vx_ref, onorm_ref, o_ref, m_ref, acc_ref):
    qi = pl.program_id(1)
    kj = pl.program_id(2)
    tq = qx_ref.shape[1]
    tk = kx_ref.shape[1]

    @pl.when(kj == 0)
    def _():
        m_ref[...] = jnp.full(m_ref.shape, NEG_BIG, F32)
        acc_ref[...] = jnp.zeros(acc_ref.shape, F32)

    def step(masked):
        if masked:
            row = lax.broadcasted_iota(jnp.int32, (tq, tk), 0)
            col = lax.broadcasted_iota(jnp.int32, (tq, tk), 1)
            keep = row >= col
        for h0 in range(0, H_B, FOX_HEAD_GROUP):
            heads = range(h0, h0 + FOX_HEAD_GROUP)
            hss = [slice(h * FOX_GROUP, (h + 1) * FOX_GROUP) for h in heads]
            ss = [lax.dot_general(qx_ref[0, :, hs], kx_ref[0, :, hs], NT, preferred_element_type=F32) for hs in hss]
            if masked:
                ss = [jnp.where(keep, s, NEG_BIG) for s in ss]
            m_olds = [m_ref[h] for h in heads]
            m_news = [jnp.maximum(m_old, jnp.broadcast_to(jnp.max(s, axis=-1, keepdims=True), m_old.shape))
                      for m_old, s in zip(m_olds, ss)]
            ps = [jnp.concatenate([jnp.exp(s[:, j * LANES:(j + 1) * LANES] - m_new) for j in range(tk // LANES)],
                                  axis=1).astype(BF16) for s, m_new in zip(ss, m_news)]
            pvs = [jnp.dot(p, vx_ref[0, :, hs], preferred_element_type=F32) for p, hs in zip(ps, hss)]
            for h, m_old, m_new, pv in zip(heads, m_olds, m_news, pvs):
                acc_ref[h] = jnp.exp(m_old - m_new) * acc_ref[h] + pv
                m_ref[h] = m_new

    @pl.when(kj < qi)
    def _():
        step(False)

    @pl.when(kj == qi)
    def _():
        step(True)
        for h in range(H_B):
            acc = acc_ref[h]
            o = acc[:, 0:DH_B] / acc[:, DH_B:DH_B + 1]
            o_ref[0, :, h * DH_B:(h + 1) * DH_B] = _rms(o, onorm_ref[...])


def _fox_prompt(qx, kx, vx, onorm, tq):
    b, t, _ = qx.shape
    n = t // tq
    qspec = lambda w: pl.BlockSpec((1, tq, w), lambda i, a, c: (i, a, 0))
    kspec = lambda w: pl.BlockSpec((1, tq, w), lambda i, a, c: (i, jnp.minimum(a, c), 0))
    return pl.pallas_call(
        _fox_flash_kernel,
        grid=(b, n, n),
        in_specs=[qspec(FOX_XW), kspec(FOX_XW), kspec(FOX_XW), pl.BlockSpec((1, DH_B), lambda i, a, c: (0, 0))],
        out_specs=qspec(B_WIDTH),
        out_shape=jax.ShapeDtypeStruct((b, t, B_WIDTH), F32),
        scratch_shapes=[pltpu.VMEM((H_B, tq, FOX_GROUP), F32), pltpu.VMEM((H_B, tq, FOX_GROUP), F32)],
        compiler_params=_cparams(("parallel", "parallel", "arbitrary")),
        name="fox_prompt",
    )(qx, kx, vx, onorm)


def _fox_sample_kernel(q_ref, k_ref, v_ref, ccol_ref, crow_ref, onorm_ref, o_ref):
    tq = q_ref.shape[1]
    tk = k_ref.shape[1]
    row = lax.broadcasted_iota(jnp.int32, (tq, tk), 0) + (tk - tq)
    col = lax.broadcasted_iota(jnp.int32, (tq, tk), 1)
    keep = row >= col
    for h in range(H_B):
        hs = slice(h * DH_B, (h + 1) * DH_B)
        s = lax.dot_general(q_ref[0, :, hs], k_ref[0, :, hs].astype(BF16), NT, preferred_element_type=F32)
        s = s + (ccol_ref[0, :, SUBLANES + h:SUBLANES + h + 1] - crow_ref[0, h:h + 1, :])
        s = jnp.where(keep, s, NEG_BIG)
        p = jnp.exp(s - jnp.max(s, axis=-1, keepdims=True))
        l = jnp.sum(p, axis=-1, keepdims=True)
        o = jnp.dot(p.astype(BF16), v_ref[0, :, hs].astype(BF16), preferred_element_type=F32) / l
        o_ref[0, :, hs] = _rms(o, onorm_ref[...])


def _fox_sample(q, k_all, v_all, ccol_q, crow, onorm):
    b, tq, _ = q.shape
    tk = k_all.shape[1]
    blk = lambda n, w: pl.BlockSpec((1, n, w), lambda i: (i, 0, 0))
    return pl.pallas_call(
        _fox_sample_kernel,
        grid=(b,),
        in_specs=[blk(tq, B_WIDTH), blk(tk, B_WIDTH), blk(tk, B_WIDTH), blk(tq, LANES), blk(SUBLANES, tk),
                  pl.BlockSpec((1, DH_B), lambda i: (0, 0))],
        out_specs=blk(tq, B_WIDTH),
        out_shape=jax.ShapeDtypeStruct((b, tq, B_WIDTH), F32),
        compiler_params=_cparams(("parallel",)),
        name="fox_sample",
    )(q, k_all, v_all, ccol_q, crow, onorm)


def _outproj_kernel(oa_ref, ob_ref, h_ref, wo_ref, g_ref, wq_ref, h1_ref, xn_ref, qp_ref):
    dot = functools.partial(jnp.dot, preferred_element_type=F32)
    h1 = h_ref[...] + dot(oa_ref[...].astype(BF16), wo_ref[0:A_WIDTH, :]) \
        + dot(ob_ref[...].astype(BF16), wo_ref[A_WIDTH:A_WIDTH + B_WIDTH, :])
    h1_ref[...] = h1.reshape(h1_ref.shape)
    xn = _rms(h1, g_ref[...])
    xn_ref[...] = xn.reshape(xn_ref.shape)
    qp_ref[...] = dot(xn.astype(BF16), wq_ref[...])


def _outproj(oa, ob, h, w_out, g, wq, tm):
    t = h.shape[0]
    nq = wq.shape[1]
    full = lambda shape: pl.BlockSpec(shape, lambda i: (0, 0))
    row = lambda n: pl.BlockSpec((tm, n), lambda i: (i, 0))
    tok3 = pl.BlockSpec((tm, ROW_CHUNKS, LANES), lambda i: (i, 0, 0))
    return pl.pallas_call(
        _outproj_kernel,
        grid=(t // tm,),
        in_specs=[row(A_WIDTH), row(B_WIDTH), row(D_MODEL), full((D_MODEL, D_MODEL)), full((1, D_MODEL)),
                  full((D_MODEL, nq))],
        out_specs=[tok3, tok3, row(nq)],
        out_shape=[jax.ShapeDtypeStruct((t, ROW_CHUNKS, LANES), F32), jax.ShapeDtypeStruct((t, ROW_CHUNKS, LANES), F32),
                   jax.ShapeDtypeStruct((t, nq), F32)],
        compiler_params=_cparams(("parallel",)),
        name="outproj",
    )(oa, ob, h, w_out, g, wq)


def _topk_rows(s, k, payload=None):
    n = s.shape[0]
    rid = lax.broadcasted_iota(jnp.int32, s.shape, 0).astype(F32)
    vals, picks = [], []
    for _ in range(k):
        m = jnp.max(s, axis=0, keepdims=True)
        idx = jnp.min(jnp.where(s == m, rid, float(n)), axis=0, keepdims=True)
        hit = rid == idx
        vals.append(m)
        picks.append(idx if payload is None else jnp.sum(jnp.where(hit, payload, 0.0), axis=0, keepdims=True))
        s = jnp.where(hit, -jnp.inf, s)
    return jnp.concatenate(vals, axis=0), jnp.concatenate(picks, axis=0)


PEER_PAIRS = [(i, j) for i in range(PEER_TOPK) for j in range(PEER_TOPK) if (i + 1) * (j + 1) <= PEER_TOPK]
PEER_PAIR_ROWS = -(-len(PEER_PAIRS) // SUBLANES) * SUBLANES


def _pair_selectors():
    sel = [[[1.0 if (r < len(PEER_PAIRS) and PEER_PAIRS[r][side] == i) else 0.0 for i in range(PEER_TOPK)]
            for r in range(PEER_PAIR_ROWS)] for side in range(2)]
    return jnp.asarray(sel, BF16)


def _select_rows(sel, x, exact_f32):
    dot = functools.partial(jnp.dot, preferred_element_type=F32)
    if not exact_f32:
        return dot(sel, x.astype(BF16))
    p = _split3_bf16(x)
    return dot(sel, p[0]) + dot(sel, p[1]) + dot(sel, p[2])


def _peer_route_kernel(qp_ref, keys_ref, sel_ref, e_ref, gate_ref):
    tb = qp_ref.shape[0]
    dq = LANES
    k = PEER_TOPK
    valid = lax.broadcasted_iota(jnp.int32, (PEER_PAIR_ROWS, tb), 0) < len(PEER_PAIRS)
    es, gs = [], []
    for h in range(PEER_HEADS):
        sv, si = [], []
        for c in range(2):
            q = qp_ref[:, (2 * h + c) * dq:(2 * h + c + 1) * dq]
            st = _mm(keys_ref[c], q, passes=3, dims=NT)
            v, i = _topk_rows(st, k)
            sv.append(v)
            si.append(i)
        cand = _select_rows(sel_ref[0], sv[0], True) + _select_rows(sel_ref[1], sv[1], True)
        cand = jnp.where(valid, cand, -jnp.inf)
        ecand = (_select_rows(sel_ref[0], si[0], False) * N_KEYS + _select_rows(sel_ref[1], si[1], False)) * ROW_WORDS
        cv, e = _topk_rows(cand, k, payload=ecand)
        p = jnp.exp(cv - cv[0:1, :])
        gs.append(p / jnp.sum(p, axis=0, keepdims=True))
        es.append(e)
    e_all = jnp.concatenate(es, axis=0)
    g_all = jnp.concatenate(gs, axis=0)
    for j in range(tb // LANES):
        e_ref[j * LANES:(j + 1) * LANES, :] = e_all[:, j * LANES:(j + 1) * LANES].T.astype(jnp.int32)
        gate_ref[j * LANES:(j + 1) * LANES, :] = g_all[:, j * LANES:(j + 1) * LANES].T


def _peer_route(qp, keys, tb):
    t = qp.shape[0]
    sel = _pair_selectors()
    return pl.pallas_call(
        _peer_route_kernel,
        grid=(t // tb,),
        in_specs=[pl.BlockSpec((tb, qp.shape[1]), lambda i: (i, 0)),
                  pl.BlockSpec(keys.shape, lambda i: (0, 0, 0)),
                  pl.BlockSpec(sel.shape, lambda i: (0, 0, 0))],
        out_specs=[pl.BlockSpec((tb, PEER_SLOTS), lambda i: (i, 0)), pl.BlockSpec((tb, PEER_SLOTS), lambda i: (i, 0))],
        out_shape=[jax.ShapeDtypeStruct((t, PEER_SLOTS), jnp.int32), jax.ShapeDtypeStruct((t, PEER_SLOTS), F32)],
        compiler_params=_cparams(("parallel",)),
        name="peer_route",
    )(qp, keys, sel)


ROW_WORDS = D_MODEL // 2 // LANES
ROW_CHUNKS = D_MODEL // LANES
PEER_NSLOT_U = 4
PEER_NSLOT_V = 4
PEER_LOOP_REPS = 1


def _pack_kernel(t_ref, o_ref):
    o_ref[...] = pltpu.bitcast(t_ref[...].astype(BF16), jnp.int32)


def _pack_table(tab, blk=1024):
    n = tab.shape[0]
    return pl.pallas_call(
        _pack_kernel,
        grid=(n // blk,),
        in_specs=[pl.BlockSpec((blk * ROW_CHUNKS, LANES), lambda i: (i, 0))],
        out_specs=pl.BlockSpec((blk * ROW_WORDS, LANES), lambda i: (i, 0)),
        out_shape=jax.ShapeDtypeStruct((n * ROW_WORDS, LANES), jnp.int32),
        compiler_params=_cparams(("parallel",)),
        name="pack_table",
    )(tab.astype(F32).reshape(n * ROW_CHUNKS, LANES))


def _gather_rows(tab_ref, e_ref, ts, slot_refs):
    for j0 in range(0, PEER_SLOTS, SUBLANES):
        ids = [e_ref.at[t, pl.ds(j0, SUBLANES)] for t in ts]
        for jj in range(SUBLANES):
            j = j0 + jj
            for row_ids, slot_ref in zip(ids, slot_refs):
                slot_ref[j * ROW_WORDS:(j + 1) * ROW_WORDS, :] = \
                    tab_ref[pl.ds(pl.multiple_of(row_ids[jj], ROW_WORDS), ROW_WORDS), :]


def _slot_rows(slot_ref):
    return pltpu.bitcast(slot_ref[...], BF16)


def _slotted_token_loop(tb, nslot, gather, compute):
    pairs = [(s, s + 1) for s in range(0, nslot, 2)]
    for ss in pairs:
        gather(list(ss), ss)
    tokens_per_trip = nslot * PEER_LOOP_REPS

    def body(i, carry):
        for r in range(PEER_LOOP_REPS):
            t0 = tokens_per_trip * i + nslot * r
            for ss in pairs:
                for s in ss:
                    compute(t0 + s, s)
                gather([jnp.minimum(t0 + nslot + s, tb - 1) for s in ss], ss)
        return carry

    lax.fori_loop(0, tb // tokens_per_trip, body, 0)


def _chunk_mask():
    n = PEER_SLOTS * ROW_CHUNKS
    r = lax.broadcasted_iota(jnp.int32, (ROW_CHUNKS, n), 0)
    c = lax.broadcasted_iota(jnp.int32, (ROW_CHUNKS, n), 1)
    return (c % ROW_CHUNKS == r).astype(F32)


def _gelu(a):
    return 0.5 * a * (1.0 + lax.erf(a * (2.0 ** -0.5)))


def _peer_u_kernel(e_ref, xn_ref, gate_ref, tab_ref, w_ref, slots_ref, drow_ref):
    tb = xn_ref.shape[0]
    mask = _chunk_mask()
    slots = [slots_ref.at[s] for s in range(slots_ref.shape[0])]

    def gather(ts, ss):
        _gather_rows(tab_ref, e_ref, ts, [slots[s] for s in ss])

    def compute(t, s):
        xh, xl = _split_bf16(xn_ref[t])
        d = lax.dot_general(jnp.concatenate([xh, xl], axis=0), _slot_rows(slots[s]), NT,
                            preferred_element_type=F32)
        e = (d[0:ROW_CHUNKS] + d[ROW_CHUNKS:2 * ROW_CHUNKS]) * mask
        drow_ref[pl.ds(t, 1), :] = jnp.sum(e, axis=0, keepdims=True)

    _slotted_token_loop(tb, len(slots), gather, compute)
    n = PEER_SLOTS * ROW_CHUNKS
    r = lax.broadcasted_iota(jnp.int32, (n, PEER_SLOTS), 0)
    c = lax.broadcasted_iota(jnp.int32, (n, PEER_SLOTS), 1)
    comp = (r // ROW_CHUNKS == c).astype(BF16)
    dp = _split3_bf16(drow_ref[...])
    dot = functools.partial(jnp.dot, preferred_element_type=F32)
    a = dot(dp[0], comp) + (dot(dp[1], comp) + dot(dp[2], comp))
    w_ref[...] = gate_ref[...] * _gelu(a)


def _peer_v_kernel(e_ref, w_ref, h_ref, tab_ref, o_ref, slots_ref, wexp_ref):
    tb = h_ref.shape[0]
    slots = [slots_ref.at[s] for s in range(slots_ref.shape[0])]
    mask = _chunk_mask()
    n = PEER_SLOTS * ROW_CHUNKS
    r = lax.broadcasted_iota(jnp.int32, (PEER_SLOTS, n), 0)
    c = lax.broadcasted_iota(jnp.int32, (PEER_SLOTS, n), 1)
    expand = (c // ROW_CHUNKS == r).astype(BF16)
    wp = _split3_bf16(w_ref[...])
    dot = functools.partial(jnp.dot, preferred_element_type=F32)
    wexp_ref[...] = dot(wp[0], expand) + (dot(wp[1], expand) + dot(wp[2], expand))

    def gather(ts, ss):
        _gather_rows(tab_ref, e_ref, ts, [slots[s] for s in ss])

    def compute(t, s):
        wm = wexp_ref[pl.ds(t, 1), :] * mask
        wh, wl = _split_bf16(wm)
        d = jnp.dot(jnp.concatenate([wh, wl], axis=0), _slot_rows(slots[s]), preferred_element_type=F32)
        o_ref[t] = h_ref[t] + (d[0:ROW_CHUNKS] + d[ROW_CHUNKS:2 * ROW_CHUNKS])

    _slotted_token_loop(tb, len(slots), gather, compute)


def _table_spec(tab):
    return pl.BlockSpec(tab.shape, lambda i: (0, 0), pipeline_mode=pl.Buffered(1))


def _peer_u(e, xn3, gate, tab, tb):
    t = e.shape[0]
    n = PEER_SLOTS * ROW_CHUNKS
    return pl.pallas_call(
        _peer_u_kernel,
        grid=(t // tb,),
        in_specs=[pl.BlockSpec((tb, PEER_SLOTS), lambda i: (i, 0), memory_space=pltpu.SMEM),
                  pl.BlockSpec((tb, ROW_CHUNKS, LANES), lambda i: (i, 0, 0)),
                  pl.BlockSpec((tb, PEER_SLOTS), lambda i: (i, 0)),
                  _table_spec(tab)],
        out_specs=pl.BlockSpec((tb, PEER_SLOTS), lambda i: (i, 0)),
        out_shape=jax.ShapeDtypeStruct((t, PEER_SLOTS), F32),
        scratch_shapes=[pltpu.VMEM((PEER_NSLOT_U, PEER_SLOTS * ROW_WORDS, LANES), jnp.int32), pltpu.VMEM((tb, n), F32)],
        compiler_params=_cparams(("arbitrary",)),
        name="peer_u",
    )(e, xn3, gate, tab)


def _peer_v(e, w, h3, tab, tb):
    t = e.shape[0]
    n = PEER_SLOTS * ROW_CHUNKS
    return pl.pallas_call(
        _peer_v_kernel,
        grid=(t // tb,),
        in_specs=[pl.BlockSpec((tb, PEER_SLOTS), lambda i: (i, 0), memory_space=pltpu.SMEM),
                  pl.BlockSpec((tb, PEER_SLOTS), lambda i: (i, 0)),
                  pl.BlockSpec((tb, ROW_CHUNKS, LANES), lambda i: (i, 0, 0)),
                  _table_spec(tab)],
        out_specs=pl.BlockSpec((tb, ROW_CHUNKS, LANES), lambda i: (i, 0, 0)),
        out_shape=jax.ShapeDtypeStruct((t, ROW_CHUNKS, LANES), F32),
        scratch_shapes=[pltpu.VMEM((PEER_NSLOT_V, PEER_SLOTS * ROW_WORDS, LANES), jnp.int32), pltpu.VMEM((tb, n), F32)],
        compiler_params=_cparams(("arbitrary",)),
        name="peer_v",
    )(e, w, h3, tab)


def _ple_kernel(h_ref, p_ref, gple_ref, wg_ref, wp_ref, gfin_ref, y_ref):
    dot = functools.partial(jnp.dot, preferred_element_type=F32)
    h = h_ref[...].reshape(h_ref.shape[0], D_MODEL)
    gate = _sigmoid(dot(_rms(h, gple_ref[...]).astype(BF16), wg_ref[...]))
    h = h + dot(p_ref[...].astype(BF16), wp_ref[...]) * gate
    y_ref[...] = _rms(h, gfin_ref[...])


def _ple(h, p, gple, wg, wp, gfin, tm):
    t = h.shape[0]
    full = lambda shape: pl.BlockSpec(shape, lambda i: (0, 0))
    row = lambda n: pl.BlockSpec((tm, n), lambda i: (i, 0))
    return pl.pallas_call(
        _ple_kernel,
        grid=(t // tm,),
        in_specs=[pl.BlockSpec((tm, ROW_CHUNKS, LANES), lambda i: (i, 0, 0)), row(PLE_DIM), full((1, D_MODEL)),
                  full((D_MODEL, D_MODEL)), full((PLE_DIM, D_MODEL)), full((1, D_MODEL))],
        out_specs=row(D_MODEL),
        out_shape=jax.ShapeDtypeStruct((t, D_MODEL), F32),
        compiler_params=_cparams(("parallel",)),
        name="ple_final",
    )(h, p, gple, wg, wp, gfin)


def _row_tile(n, want):
    t = want
    while n % t:
        t //= 2
    return t


def _prep_weights(w_in, conv_w, a_log, dt_bias, gdn_onorm, b_f, fox_onorm, w_out, norm_mix, norm_ffn,
                  peer_wq, peer_keys, peer_u, peer_v, norm_ple, w_ple_proj, w_ple_gate, norm_final):
    w_main = jnp.concatenate([w_in[:, :OFF1], w_in[:, OFF3:OFF4], w_in[:, OFF4:OFF5]], axis=1).astype(BF16)
    w_small = jnp.concatenate([w_in[:, OFF1:OFF3], w_in[:, OFF5:],
                               jnp.zeros((D_MODEL, LANES - 2 * H_A - H_B), w_in.dtype)], axis=1).astype(BF16)
    pad = lambda v, lo: jnp.pad(v.astype(F32), (lo, LANES - lo - v.shape[0])).reshape(1, LANES)
    bias = pad(dt_bias, H_A) + pad(b_f, 2 * H_A)
    alog = pad(a_log, H_A)
    r = lambda v: v.astype(F32).reshape(1, -1)
    return dict(
        w_main=w_main, w_small=w_small, bias=bias, alog=alog, conv_w=conv_w.astype(F32),
        gdn_onorm=r(gdn_onorm), fox_onorm=r(fox_onorm), w_out=w_out.astype(BF16),
        norm_mix=r(norm_mix), norm_ffn=r(norm_ffn), wq=peer_wq.astype(BF16), keys=peer_keys.astype(F32),
        utab=_pack_table(peer_u), vtab=_pack_table(peer_v), norm_ple=r(norm_ple),
        w_proj=w_ple_proj.astype(BF16), w_gate=w_ple_gate.astype(BF16), norm_final=r(norm_final))


def _layer(x, p, prefix, s0, past_k, past_v, past_lf, w):
    b, t, _ = x.shape
    n = b * t
    tm = _row_tile(n, 512)
    h = x.reshape(n, D_MODEL)
    conv_in, zg, qb, kb, vb, small = _inproj(h, w["norm_mix"], w["w_main"], w["w_small"], w["bias"], w["alog"], tm)
    conv3 = conv_in.reshape(b, t, CONV_DIM)
    small3 = small.reshape(b, t, LANES)
    prefix8 = jnp.zeros((b, SUBLANES, CONV_DIM), F32)
    if prefix is not None:
        prefix8 = prefix8.at[:, SUBLANES - (CONV_W - 1):, :].set(prefix.astype(F32))
    if s0 is None:
        s0 = jnp.zeros((b, H_A, DK_A, DV_A), F32)
    o_a, s_new = _gdn(conv3, small3, zg.reshape(b, t, A_WIDTH), prefix8, s0.astype(F32), w["conv_w"], w["gdn_onorm"],
                      nb=2 if b % 2 == 0 else 1)

    q3 = qb.reshape(b, t, B_WIDTH)
    k3 = kb.reshape(b, t, B_WIDTH)
    v3 = vb.reshape(b, t, B_WIDTH)
    if past_k is None:
        tq = _row_tile(t, 512)
        qx, kx, vx = _fox_prep(small3, q3, k3, v3, tq)
        o_b = _fox_prompt(qx, kx, vx, w["fox_onorm"], tq)
    else:
        pl_len = past_k.shape[1]
        lf_past = jnp.pad(past_lf.astype(F32), ((0, 0), (0, 0), (2 * H_A, LANES - 2 * H_A - H_B)))
        ccol, crow = _fox_cumsum(jnp.concatenate([lf_past, small3], axis=1))
        k_all = jnp.concatenate([past_k.reshape(b, pl_len, B_WIDTH).astype(F32), k3], axis=1)
        v_all = jnp.concatenate([past_v.reshape(b, pl_len, B_WIDTH).astype(F32), v3], axis=1)
        o_b = _fox_sample(q3, k_all, v_all, ccol[:, pl_len:, :], crow, w["fox_onorm"])

    h1, xn, qp = _outproj(o_a.reshape(n, A_WIDTH), o_b.reshape(n, B_WIDTH), h, w["w_out"], w["norm_ffn"], w["wq"], tm)
    tb = _row_tile(n, 128)
    e, gate = _peer_route(qp, w["keys"], tb)
    wgt = _peer_u(e, xn, gate, w["utab"], tb)
    h2 = _peer_v(e, wgt, h1, w["vtab"], tb)
    y = _ple(h2, p.reshape(n, PLE_DIM), w["norm_ple"], w["w_gate"], w["w_proj"],
             w["norm_final"], tm)
    conv_state = conv3[:, t - (CONV_W - 1):, :]
    return (y.reshape(b, t, D_MODEL), conv_state, s_new, k3.reshape(b, t, H_B, DH_B), v3.reshape(b, t, H_B, DH_B),
            small3[:, :, 2 * H_A:2 * H_A + H_B])


def kernel(x_prompt, x_sample, p_prompt, p_sample, cache_conv, state_gdn, cache_fox_k, cache_fox_v, cache_fox_logf, w_in, conv_w, a_log, dt_bias, gdn_onorm, b_f, fox_onorm, w_out, norm_mix, norm_ffn, peer_wq, peer_keys, peer_u, peer_v, norm_ple, w_ple_proj, w_ple_gate, norm_final):
    assert w_in.shape[0] == 1, "single-layer step"
    w = _prep_weights(w_in[0], conv_w[0], a_log[0], dt_bias[0], gdn_onorm[0], b_f[0], fox_onorm[0], w_out[0],
                      norm_mix[0], norm_ffn[0], peer_wq[0], peer_keys[0], peer_u[0], peer_v[0], norm_ple[0],
                      w_ple_proj[0], w_ple_gate[0], norm_final)
    ys, c2, s2, k2, v2, l2 = _layer(x_sample, p_sample[0], cache_conv[0], state_gdn[0], cache_fox_k[0],
                                    cache_fox_v[0], cache_fox_logf[0], w)
    yp, c1, s1, k1, v1, l1 = _layer(x_prompt, p_prompt[0], None, None, None, None, None, w)
    st = lambda a: a[None]
    return (yp, ys, st(c1), st(s1), st(k1), st(v1), st(l1), st(c2), st(s2), st(k2), st(v2), st(l2))
```

```python
import functools
import math

import jax
import jax.numpy as jnp
from jax import lax
from jax.experimental import pallas as pl
from jax.experimental.pallas import tpu as pltpu

F32 = jnp.float32
BF16 = jnp.bfloat16

D_MODEL = 1024
H_A, DK_A, DV_A = 4, 128, 128
CONV_W = 4
CONV_DIM = H_A * (2 * DK_A + DV_A)
A_WIDTH = H_A * DV_A
H_B, DH_B = 8, 64
B_WIDTH = H_B * DH_B
OFF1 = CONV_DIM
OFF2 = OFF1 + H_A
OFF3 = OFF2 + H_A
OFF4 = OFF3 + A_WIDTH
OFF5 = OFF4 + 3 * B_WIDTH
N_KEYS = 128
PEER_HEADS = 8
PEER_TOPK = 16
PEER_SLOTS = PEER_HEADS * PEER_TOPK
PLE_DIM = 256
EPS = 1e-6
GDN_CHUNK = 64

LANES = 128
SUBLANES = 8
VMEM_LIMIT = 56 * 1024 * 1024

NEG_BIG = -1e30


def _cparams(sem, vmem=VMEM_LIMIT):
    return pltpu.CompilerParams(dimension_semantics=sem, vmem_limit_bytes=vmem)


def _split_bf16(x):
    hi = x.astype(BF16)
    lo = (x - hi.astype(F32)).astype(BF16)
    return hi, lo


def _mm(a, b, passes=1, dims=(((1,), (0,)), ((), ()))):
    dg = functools.partial(lax.dot_general, dimension_numbers=dims, preferred_element_type=F32)
    if passes == 1:
        return dg(a.astype(BF16), b.astype(BF16))
    ah, al = _split_bf16(a)
    bh, bl = _split_bf16(b)
    return dg(ah, bh) + (dg(al, bh) + dg(ah, bl))


NT = (((1,), (1,)), ((), ()))


def _rms(x, g):
    return x * lax.rsqrt(jnp.mean(x * x, axis=-1, keepdims=True) + EPS) * g


def _softplus(x):
    return jnp.maximum(x, 0.0) + jnp.log1p(jnp.exp(-jnp.abs(x)))


def _sigmoid(x):
    return 1.0 / (1.0 + jnp.exp(-x))


def _inproj_kernel(h_ref, g_ref, w_ref, ws_ref, bias_ref, alog_ref,
                   conv_ref, gate_ref, qb_ref, kb_ref, vb_ref, small_ref):
    xn = _rms(h_ref[...], g_ref[...]).astype(BF16)
    dot = functools.partial(jnp.dot, preferred_element_type=F32)
    conv_ref[...] = dot(xn, w_ref[:, 0:CONV_DIM])
    gate_ref[...] = dot(xn, w_ref[:, CONV_DIM:CONV_DIM + A_WIDTH])
    o = CONV_DIM + A_WIDTH
    qb_ref[...] = (dot(xn, w_ref[:, o:o + B_WIDTH]) * (DH_B ** -0.5)).astype(BF16)
    kb_ref[...] = dot(xn, w_ref[:, o + B_WIDTH:o + 2 * B_WIDTH])
    vb_ref[...] = dot(xn, w_ref[:, o + 2 * B_WIDTH:o + 3 * B_WIDTH])
    z = dot(xn, ws_ref[...]) + bias_ref[...]
    lane = lax.broadcasted_iota(jnp.int32, z.shape, 1)
    small_ref[...] = jnp.where(lane < H_A, _sigmoid(z),
                               jnp.where(lane < 2 * H_A, -jnp.exp(alog_ref[...]) * _softplus(z), -_softplus(-z)))


def _inproj(h, g, w_main, w_small, bias, alog, tm):
    t = h.shape[0]
    nm = w_main.shape[1]
    full = lambda shape: pl.BlockSpec(shape, lambda i: (0, 0))
    row = lambda n: pl.BlockSpec((tm, n), lambda i: (i, 0))
    return pl.pallas_call(
        _inproj_kernel,
        grid=(t // tm,),
        in_specs=[row(D_MODEL), full((1, D_MODEL)), full((D_MODEL, nm)), full((D_MODEL, LANES)),
                  full((1, LANES)), full((1, LANES))],
        out_specs=[row(CONV_DIM), row(A_WIDTH), row(B_WIDTH), row(B_WIDTH), row(B_WIDTH), row(LANES)],
        out_shape=[jax.ShapeDtypeStruct((t, CONV_DIM), F32), jax.ShapeDtypeStruct((t, A_WIDTH), F32),
                   jax.ShapeDtypeStruct((t, B_WIDTH), BF16), jax.ShapeDtypeStruct((t, B_WIDTH), F32),
                   jax.ShapeDtypeStruct((t, B_WIDTH), F32), jax.ShapeDtypeStruct((t, LANES), F32)],
        compiler_params=_cparams(("parallel",)),
        name="inproj",
    )(h, g, w_main, w_small, bias, alog)


def _split3_bf16(x):
    h1 = x.astype(BF16)
    r = x - h1.astype(F32)
    h2 = r.astype(BF16)
    h3 = (r - h2.astype(F32)).astype(BF16)
    return h1, h2, h3


def _cumsum_kernel(lf_ref, ccol_ref, crow_ref, *, chunk):
    tk = lf_ref.shape[1]
    r = lax.broadcasted_iota(jnp.int32, (chunk, chunk), 0)
    c = lax.broadcasted_iota(jnp.int32, (chunk, chunk), 1)
    tril = (r >= c).astype(BF16)
    er = lax.broadcasted_iota(jnp.int32, (SUBLANES, LANES), 0)
    ec = lax.broadcasted_iota(jnp.int32, (SUBLANES, LANES), 1)
    pick = (ec == er + SUBLANES).astype(BF16)
    dot = functools.partial(jnp.dot, preferred_element_type=F32)
    carry = jnp.zeros((1, LANES), F32)
    for i in range(tk // chunk):
        sl = pl.ds(i * chunk, chunk)
        parts = _split3_bf16(lf_ref[0, sl, :])
        cs = dot(tril, parts[0]) + (dot(tril, parts[1]) + dot(tril, parts[2])) + carry
        ccol_ref[0, sl, :] = cs
        carry = cs[chunk - 1:chunk, :]
        cparts = _split3_bf16(cs)
        dg = functools.partial(lax.dot_general, dimension_numbers=NT, preferred_element_type=F32)
        crow_ref[0, :, sl] = dg(pick, cparts[0]) + (dg(pick, cparts[1]) + dg(pick, cparts[2]))


def _fox_cumsum(lf):
    b, tk, _ = lf.shape
    chunk = tk if tk <= 2048 else 512
    return pl.pallas_call(
        functools.partial(_cumsum_kernel, chunk=chunk),
        grid=(b,),
        in_specs=[pl.BlockSpec((1, tk, LANES), lambda i: (i, 0, 0))],
        out_specs=[pl.BlockSpec((1, tk, LANES), lambda i: (i, 0, 0)),
                   pl.BlockSpec((1, SUBLANES, tk), lambda i: (i, 0, 0))],
        out_shape=[jax.ShapeDtypeStruct((b, tk, LANES), F32), jax.ShapeDtypeStruct((b, SUBLANES, tk), F32)],
        compiler_params=_cparams(("parallel",)),
        name="fox_cumsum",
    )(lf)


GDN_PASSES = 1
GDN_STATE_PASSES = 3
TN = (((0,), (0,)), ((), ()))


def _unit_lower_inverses(ms, row, col):
    size = ms[0].shape[0]
    mm = functools.partial(_mm, passes=GDN_PASSES)
    eye = (row == col).astype(F32)
    diag = (row // 8) == (col // 8)
    ns = [jnp.where(diag, -m, 0.0) for m in ms]
    n2s = [mm(n, n) for n in ns]
    n4s = [mm(n2, n2) for n2 in n2s]
    ts = [eye + n for n in ns]
    ts = [t + mm(t, n2) for t, n2 in zip(ts, n2s)]
    ts = [t + mm(t, n4) for t, n4 in zip(ts, n4s)]
    blk = 8
    while blk < size:
        sel = ((row // (2 * blk)) == (col // (2 * blk))) & ((row // blk) != (col // blk))
        tls = [mm(t, jnp.where(sel, m, 0.0)) for t, m in zip(ts, ms)]
        ts = [t - mm(tl, t) for t, tl in zip(ts, tls)]
        blk *= 2
    return ts


def _gdn_kernel(u_ref, small_ref, zg_ref, prefix_ref, s0_ref, cw_ref, onorm_ref,
                o_ref, s_ref, ubuf_ref):
    L = GDN_CHUNK
    nb = u_ref.shape[0]
    mm = functools.partial(_mm, passes=GDN_PASSES)
    dot = functools.partial(jnp.dot, preferred_element_type=F32)
    dg = functools.partial(lax.dot_general, dimension_numbers=NT, preferred_element_type=F32)

    @pl.when(pl.program_id(1) == 0)
    def _():
        ubuf_ref[:, 0:SUBLANES, :] = prefix_ref[...]
        s_ref[...] = s0_ref[...]

    row = lax.broadcasted_iota(jnp.int32, (L, L), 0)
    col = lax.broadcasted_iota(jnp.int32, (L, L), 1)
    causal = row >= col
    strict = row > col
    tril = causal.astype(BF16)
    er = lax.broadcasted_iota(jnp.int32, (SUBLANES, LANES), 0)
    ec = lax.broadcasted_iota(jnp.int32, (SUBLANES, LANES), 1)
    pick = (ec == er + H_A).astype(BF16)

    cs, smalls, g_alls, g_rowss = [], [], [], []
    for b in range(nb):
        ubuf_ref[b, SUBLANES:SUBLANES + L, :] = u_ref[b]
        y = ubuf_ref[b, SUBLANES - 3:SUBLANES - 3 + L, :] * cw_ref[0:1, :]
        for j in range(1, CONV_W):
            y = y + ubuf_ref[b, SUBLANES - 3 + j:SUBLANES - 3 + j + L, :] * cw_ref[j:j + 1, :]
        tail = ubuf_ref[b, L:L + SUBLANES, :]
        ubuf_ref[b, 0:SUBLANES, :] = tail
        cs.append(y * _sigmoid(y))
        small = small_ref[b]
        sp = _split3_bf16(small)
        g_all = dot(tril, sp[0]) + (dot(tril, sp[1]) + dot(tril, sp[2]))
        gp = _split3_bf16(g_all)
        smalls.append(small)
        g_alls.append(g_all)
        g_rowss.append(dg(pick, gp[0]) + (dg(pick, gp[1]) + dg(pick, gp[2])))

    chains = [(b, h) for b in range(nb) for h in range(H_A)]
    qs, ks, vs, betas, gcols, decays, kbs = [], [], [], [], [], [], []
    for b, h in chains:
        c = cs[b]
        qh = c[:, h * DK_A:(h + 1) * DK_A]
        kh = c[:, H_A * DK_A + h * DK_A:H_A * DK_A + (h + 1) * DK_A]
        qs.append(qh * lax.rsqrt(jnp.sum(qh * qh, axis=-1, keepdims=True) + EPS) * (DK_A ** -0.5))
        kh = kh * lax.rsqrt(jnp.sum(kh * kh, axis=-1, keepdims=True) + EPS)
        ks.append(kh)
        vs.append(c[:, 2 * H_A * DK_A + h * DV_A:2 * H_A * DK_A + (h + 1) * DV_A])
        beta = smalls[b][:, h:h + 1]
        g_col = g_alls[b][:, H_A + h:H_A + h + 1]
        diff = g_col - g_rowss[b][h:h + 1, :]
        betas.append(beta)
        gcols.append(g_col)
        decays.append(jnp.where(causal, jnp.exp(jnp.where(causal, diff, 0.0)), 0.0))
        kbs.append(kh * beta)
    ms = [jnp.where(strict, mm(kb, k, dims=NT) * d, 0.0) for kb, k, d in zip(kbs, ks, decays)]
    ts = _unit_lower_inverses(ms, row, col)
    egs = [jnp.exp(g) for g in gcols]
    s_olds = [s_ref[b, h] for b, h in chains]
    sol_vs = [mm(t, v * beta) for t, v, beta in zip(ts, vs, betas)]
    sol_ks = [mm(t, kb * eg) for t, kb, eg in zip(ts, kbs, egs)]
    mm_state = functools.partial(_mm, passes=GDN_STATE_PASSES)
    u_news = [sv - mm_state(sk, s) for sv, sk, s in zip(sol_vs, sol_ks, s_olds)]
    attns = [_mm(q, k, dims=NT) * d for q, k, d in zip(qs, ks, decays)]
    outs = [_mm(q * eg, s) + _mm(a, u) for q, eg, s, a, u in zip(qs, egs, s_olds, attns, u_news)]
    for (b, h), k, g, s, u, o in zip(chains, ks, gcols, s_olds, u_news, outs):
        g_last = g[L - 1:L, :]
        s_ref[b, h] = jnp.exp(g_last) * s + mm_state(k * jnp.exp(g_last - g), u, dims=TN)
        zgh = zg_ref[b, :, h * DV_A:(h + 1) * DV_A]
        o_ref[b, :, h * DV_A:(h + 1) * DV_A] = _rms(o, onorm_ref[...]) * (zgh * _sigmoid(zgh))


def _gdn(u, small, zg, prefix8, s0, conv_w, onorm, nb):
    b, t, _ = u.shape
    L = GDN_CHUNK
    tile = lambda n: pl.BlockSpec((nb, L, n), lambda i, j: (i, j, 0))
    return pl.pallas_call(
        _gdn_kernel,
        grid=(b // nb, t // L),
        in_specs=[tile(CONV_DIM), tile(LANES), tile(A_WIDTH),
                  pl.BlockSpec((nb, SUBLANES, CONV_DIM), lambda i, j: (i, 0, 0)),
                  pl.BlockSpec((nb, H_A, DK_A, DV_A), lambda i, j: (i, 0, 0, 0)),
                  pl.BlockSpec((CONV_W, CONV_DIM), lambda i, j: (0, 0)),
                  pl.BlockSpec((1, DV_A), lambda i, j: (0, 0))],
        out_specs=[tile(A_WIDTH), pl.BlockSpec((nb, H_A, DK_A, DV_A), lambda i, j: (i, 0, 0, 0))],
        out_shape=[jax.ShapeDtypeStruct((b, t, A_WIDTH), F32), jax.ShapeDtypeStruct((b, H_A, DK_A, DV_A), F32)],
        scratch_shapes=[pltpu.VMEM((nb, SUBLANES + L, CONV_DIM), F32)],
        compiler_params=_cparams(("parallel", "arbitrary")),
        name="gdn",
    )(u, small, zg, prefix8, s0, conv_w, onorm)


FOX_GROUP = LANES
FOX_XW = H_B * FOX_GROUP
FOX_C0 = DH_B
FOX_C1 = DH_B + 3
FOX_HEAD_GROUP = 4


def _fox_prep_kernel(lf_ref, q_ref, k_ref, v_ref, qx_ref, kx_ref, vx_ref, carry_ref, place_ref):
    L = lf_ref.shape[1]
    dot = functools.partial(jnp.dot, preferred_element_type=F32)

    @pl.when(pl.program_id(1) == 0)
    def _():
        carry_ref[...] = jnp.zeros(carry_ref.shape, F32)
        r = lax.broadcasted_iota(jnp.int32, (B_WIDTH, FOX_XW), 0)
        c = lax.broadcasted_iota(jnp.int32, (B_WIDTH, FOX_XW), 1)
        d = c % FOX_GROUP
        place_ref[...] = ((d < DH_B) & (r == (c // FOX_GROUP) * DH_B + d)).astype(BF16)

    r = lax.broadcasted_iota(jnp.int32, (L, L), 0)
    c = lax.broadcasted_iota(jnp.int32, (L, L), 1)
    tril = (r >= c).astype(BF16)
    parts = _split3_bf16(lf_ref[0])
    cs = dot(tril, parts[0]) + (dot(tril, parts[1]) + dot(tril, parts[2])) + carry_ref[...]
    carry_ref[...] = cs[L - 1:L, :]
    cparts = _split3_bf16(cs)

    rr = lax.broadcasted_iota(jnp.int32, (LANES, FOX_XW), 0)
    cc = lax.broadcasted_iota(jnp.int32, (LANES, FOX_XW), 1)
    head_lane = rr == 2 * H_A + cc // FOX_GROUP
    dd = cc % FOX_GROUP
    lane1 = lax.broadcasted_iota(jnp.int32, (1, FOX_XW), 1) % FOX_GROUP
    qx = dot(q_ref[0], place_ref[...])
    kx = dot(k_ref[0].astype(BF16), place_ref[...])
    vx = dot(v_ref[0].astype(BF16), place_ref[...])
    for i in range(3):
        qx = qx + dot(cparts[i], (head_lane & (dd == FOX_C0 + i)).astype(BF16))
        kx = kx + dot(-cparts[i], (head_lane & (dd == FOX_C1 + i)).astype(BF16))
    qx = qx + ((lane1 >= FOX_C1) & (lane1 < FOX_C1 + 3)).astype(F32)
    kx = kx + ((lane1 >= FOX_C0) & (lane1 < FOX_C0 + 3)).astype(F32)
    vx = vx + (lane1 == DH_B).astype(F32)
    qx_ref[0] = qx.astype(BF16)
    kx_ref[0] = kx.astype(BF16)
    vx_ref[0] = vx.astype(BF16)


def _fox_prep(lf, q, k, v, tl):
    b, t, _ = q.shape
    blk = lambda w: pl.BlockSpec((1, tl, w), lambda i, j: (i, j, 0))
    xs = jax.ShapeDtypeStruct((b, t, FOX_XW), BF16)
    return pl.pallas_call(
        _fox_prep_kernel,
        grid=(b, t // tl),
        in_specs=[blk(LANES), blk(B_WIDTH), blk(B_WIDTH), blk(B_WIDTH)],
        out_specs=[blk(FOX_XW), blk(FOX_XW), blk(FOX_XW)],
        out_shape=[xs, xs, xs],
        scratch_shapes=[pltpu.VMEM((1, LANES), F32), pltpu.VMEM((B_WIDTH, FOX_XW), BF16)],
        compiler_params=_cparams(("parallel", "arbitrary")),
        name="fox_prep",
    )(lf, q, k, v)


def _fox_flash_kernel(qx_ref, kx_ref, vx_ref, onorm_ref, o_ref, m_ref, acc_ref):
    qi = pl.program_id(1)
    kj = pl.program_id(2)
    tq = qx_ref.shape[1]
    tk = kx_ref.shape[1]

    @pl.when(kj == 0)
    def _():
        m_ref[...] = jnp.full(m_ref.shape, NEG_BIG, F32)
        acc_ref[...] = jnp.zeros(acc_ref.shape, F32)

    def step(masked):
        if masked:
            row = lax.broadcasted_iota(jnp.int32, (tq, tk), 0)
            col = lax.broadcasted_iota(jnp.int32, (tq, tk), 1)
            keep = row >= col
        for h0 in range(0, H_B, FOX_HEAD_GROUP):
            heads = range(h0, h0 + FOX_HEAD_GROUP)
            hss = [slice(h * FOX_GROUP, (h + 1) * FOX_GROUP) for h in heads]
            ss = [lax.dot_general(qx_ref[0, :, hs], kx_ref[0, :, hs], NT, preferred_element_type=F32) for hs in hss]
            if masked:
                ss = [jnp.where(keep, s, NEG_BIG) for s in ss]
            m_olds = [m_ref[h] for h in heads]
            m_news = [jnp.maximum(m_old, jnp.broadcast_to(jnp.max(s, axis=-1, keepdims=True), m_old.shape))
                      for m_old, s in zip(m_olds, ss)]
            ps = [jnp.concatenate([jnp.exp(s[:, j * LANES:(j + 1) * LANES] - m_new) for j in range(tk // LANES)],
                                  axis=1).astype(BF16) for s, m_new in zip(ss, m_news)]
            pvs = [jnp.dot(p, vx_ref[0, :, hs], preferred_element_type=F32) for p, hs in zip(ps, hss)]
            for h, m_old, m_new, pv in zip(heads, m_olds, m_news, pvs):
                acc_ref[h] = jnp.exp(m_old - m_new) * acc_ref[h] + pv
                m_ref[h] = m_new

    @pl.when(kj < qi)
    def _():
        step(False)

    @pl.when(kj == qi)
    def _():
        step(True)
        for h in range(H_B):
            acc = acc_ref[h]
            o = acc[:, 0:DH_B] / acc[:, DH_B:DH_B + 1]
            o_ref[0, :, h * DH_B:(h + 1) * DH_B] = _rms(o, onorm_ref[...])


def _fox_prompt(qx, kx, vx, onorm, tq):
    b, t, _ = qx.shape
    n = t // tq
    qspec = lambda w: pl.BlockSpec((1, tq, w), lambda i, a, c: (i, a, 0))
    kspec = lambda w: pl.BlockSpec((1, tq, w), lambda i, a, c: (i, jnp.minimum(a, c), 0))
    return pl.pallas_call(
        _fox_flash_kernel,
        grid=(b, n, n),
        in_specs=[qspec(FOX_XW), kspec(FOX_XW), kspec(FOX_XW), pl.BlockSpec((1, DH_B), lambda i, a, c: (0, 0))],
        out_specs=qspec(B_WIDTH),
        out_shape=jax.ShapeDtypeStruct((b, t, B_WIDTH), F32),
        scratch_shapes=[pltpu.VMEM((H_B, tq, FOX_GROUP), F32), pltpu.VMEM((H_B, tq, FOX_GROUP), F32)],
        compiler_params=_cparams(("parallel", "parallel", "arbitrary")),
        name="fox_prompt",
    )(qx, kx, vx, onorm)


def _fox_sample_kernel(q_ref, k_ref, v_ref, ccol_ref, crow_ref, onorm_ref, o_ref):
    tq = q_ref.shape[1]
    tk = k_ref.shape[1]
    row = lax.broadcasted_iota(jnp.int32, (tq, tk), 0) + (tk - tq)
    col = lax.broadcasted_iota(jnp.int32, (tq, tk), 1)
    keep = row >= col
    for h in range(H_B):
        hs = slice(h * DH_B, (h + 1) * DH_B)
        s = lax.dot_general(q_ref[0, :, hs], k_ref[0, :, hs].astype(BF16), NT, preferred_element_type=F32)
        s = s + (ccol_ref[0, :, SUBLANES + h:SUBLANES + h + 1] - crow_ref[0, h:h + 1, :])
        s = jnp.where(keep, s, NEG_BIG)
        p = jnp.exp(s - jnp.max(s, axis=-1, keepdims=True))
        l = jnp.sum(p, axis=-1, keepdims=True)
        o = jnp.dot(p.astype(BF16), v_ref[0, :, hs].astype(BF16), preferred_element_type=F32) / l
        o_ref[0, :, hs] = _rms(o, onorm_ref[...])


def _fox_sample(q, k_all, v_all, ccol_q, crow, onorm):
    b, tq, _ = q.shape
    tk = k_all.shape[1]
    blk = lambda n, w: pl.BlockSpec((1, n, w), lambda i: (i, 0, 0))
    return pl.pallas_call(
        _fox_sample_kernel,
        grid=(b,),
        in_specs=[blk(tq, B_WIDTH), blk(tk, B_WIDTH), blk(tk, B_WIDTH), blk(tq, LANES), blk(SUBLANES, tk),
                  pl.BlockSpec((1, DH_B), lambda i: (0, 0))],
        out_specs=blk(tq, B_WIDTH),
        out_shape=jax.ShapeDtypeStruct((b, tq, B_WIDTH), F32),
        compiler_params=_cparams(("parallel",)),
        name="fox_sample",
    )(q, k_all, v_all, ccol_q, crow, onorm)


def _outproj_kernel(oa_ref, ob_ref, h_ref, wo_ref, g_ref, wq_ref, h1_ref, xn_ref, qp_ref):
    dot = functools.partial(jnp.dot, preferred_element_type=F32)
    h1 = h_ref[...] + dot(oa_ref[...].astype(BF16), wo_ref[0:A_WIDTH, :]) \
        + dot(ob_ref[...].astype(BF16), wo_ref[A_WIDTH:A_WIDTH + B_WIDTH, :])
    h1_ref[...] = h1.reshape(h1_ref.shape)
    xn = _rms(h1, g_ref[...])
    xn_ref[...] = xn.reshape(xn_ref.shape)
    qp_ref[...] = dot(xn.astype(BF16), wq_ref[...])


def _outproj(oa, ob, h, w_out, g, wq, tm):
    t = h.shape[0]
    nq = wq.shape[1]
    full = lambda shape: pl.BlockSpec(shape, lambda i: (0, 0))
    row = lambda n: pl.BlockSpec((tm, n), lambda i: (i, 0))
    tok3 = pl.BlockSpec((tm, ROW_CHUNKS, LANES), lambda i: (i, 0, 0))
    return pl.pallas_call(
        _outproj_kernel,
        grid=(t // tm,),
        in_specs=[row(A_WIDTH), row(B_WIDTH), row(D_MODEL), full((D_MODEL, D_MODEL)), full((1, D_MODEL)),
                  full((D_MODEL, nq))],
        out_specs=[tok3, tok3, row(nq)],
        out_shape=[jax.ShapeDtypeStruct((t, ROW_CHUNKS, LANES), F32), jax.ShapeDtypeStruct((t, ROW_CHUNKS, LANES), F32),
                   jax.ShapeDtypeStruct((t, nq), F32)],
        compiler_params=_cparams(("parallel",)),
        name="outproj",
    )(oa, ob, h, w_out, g, wq)


def _topk_rows(ss, k, payloads=None):
    n = ss[0].shape[0]
    rid = lax.broadcasted_iota(jnp.int32, ss[0].shape, 0).astype(F32)
    vals = [[] for _ in ss]
    picks = [[] for _ in ss]
    for _ in range(k):
        ms = [jnp.max(s, axis=0, keepdims=True) for s in ss]
        idxs = [jnp.min(jnp.where(s == m, rid, float(n)), axis=0, keepdims=True) for s, m in zip(ss, ms)]
        hits = [rid == idx for idx in idxs]
        for i, (m, idx, hit) in enumerate(zip(ms, idxs, hits)):
            vals[i].append(m)
            picks[i].append(idx if payloads is None else
                            jnp.sum(jnp.where(hit, payloads[i], 0.0), axis=0, keepdims=True))
        ss = [jnp.where(hit, -jnp.inf, s) for s, hit in zip(ss, hits)]
    return [jnp.concatenate(v, axis=0) for v in vals], [jnp.concatenate(p, axis=0) for p in picks]


PEER_PAIRS = [(i, j) for i in range(PEER_TOPK) for j in range(PEER_TOPK) if (i + 1) * (j + 1) <= PEER_TOPK]
PEER_PAIR_ROWS = -(-len(PEER_PAIRS) // SUBLANES) * SUBLANES
PEER_ROUTE_GROUP = 8


def _pair_selectors():
    sel = [[[1.0 if (r < len(PEER_PAIRS) and PEER_PAIRS[r][side] == i) else 0.0 for i in range(PEER_TOPK)]
            for r in range(PEER_PAIR_ROWS)] for side in range(2)]
    return jnp.asarray(sel, BF16)


def _select_rows(sel, x, exact_f32):
    dot = functools.partial(jnp.dot, preferred_element_type=F32)
    if not exact_f32:
        return dot(sel, x.astype(BF16))
    p = _split3_bf16(x)
    return dot(sel, p[0]) + dot(sel, p[1]) + dot(sel, p[2])


def _peer_route_kernel(qp_ref, keys_ref, sel_ref, e_ref, gate_ref):
    tb = qp_ref.shape[0]
    dq = LANES
    k = PEER_TOPK
    valid = lax.broadcasted_iota(jnp.int32, (PEER_PAIR_ROWS, tb), 0) < len(PEER_PAIRS)
    es, gs = [], []
    for h0 in range(0, PEER_HEADS, PEER_ROUTE_GROUP):
        heads = range(h0, h0 + PEER_ROUTE_GROUP)
        sts = [_mm(keys_ref[c], qp_ref[:, (2 * h + c) * dq:(2 * h + c + 1) * dq], passes=3, dims=NT)
               for h in heads for c in range(2)]
        svs, sis = _topk_rows(sts, k)
        cands, ecands = [], []
        for g in range(len(heads)):
            sv0, sv1, si0, si1 = svs[2 * g], svs[2 * g + 1], sis[2 * g], sis[2 * g + 1]
            cand = _select_rows(sel_ref[0], sv0, True) + _select_rows(sel_ref[1], sv1, True)
            cands.append(jnp.where(valid, cand, -jnp.inf))
            ecands.append((_select_rows(sel_ref[0], si0, False) * N_KEYS + _select_rows(sel_ref[1], si1, False))
                          * ROW_WORDS)
        cvs, epicks = _topk_rows(cands, k, payloads=ecands)
        for cv, e in zip(cvs, epicks):
            p = jnp.exp(cv - cv[0:1, :])
            gs.append(p / jnp.sum(p, axis=0, keepdims=True))
            es.append(e)
    e_all = jnp.concatenate(es, axis=0)
    g_all = jnp.concatenate(gs, axis=0)
    for j in range(tb // LANES):
        e_ref[j * LANES:(j + 1) * LANES, :] = e_all[:, j * LANES:(j + 1) * LANES].T.astype(jnp.int32)
        gate_ref[j * LANES:(j + 1) * LANES, :] = g_all[:, j * LANES:(j + 1) * LANES].T


def _peer_route(qp, keys, tb):
    t = qp.shape[0]
    sel = _pair_selectors()
    return pl.pallas_call(
        _peer_route_kernel,
        grid=(t // tb,),
        in_specs=[pl.BlockSpec((tb, qp.shape[1]), lambda i: (i, 0)),
                  pl.BlockSpec(keys.shape, lambda i: (0, 0, 0)),
                  pl.BlockSpec(sel.shape, lambda i: (0, 0, 0))],
        out_specs=[pl.BlockSpec((tb, PEER_SLOTS), lambda i: (i, 0)), pl.BlockSpec((tb, PEER_SLOTS), lambda i: (i, 0))],
        out_shape=[jax.ShapeDtypeStruct((t, PEER_SLOTS), jnp.int32), jax.ShapeDtypeStruct((t, PEER_SLOTS), F32)],
        compiler_params=_cparams(("parallel",)),
        name="peer_route",
    )(qp, keys, sel)


ROW_WORDS = D_MODEL // 2 // LANES
ROW_CHUNKS = D_MODEL // LANES
PEER_NSLOT_U, PEER_GROUP_U, PEER_VIEW_U = 4, 4, 8
PEER_NSLOT_V, PEER_GROUP_V, PEER_VIEW_V = 2, 1, 128


def _pack_kernel(t_ref, o_ref):
    o_ref[...] = pltpu.bitcast(t_ref[...].astype(BF16), jnp.int32)


def _pack_table(tab, blk=1024):
    n = tab.shape[0]
    return pl.pallas_call(
        _pack_kernel,
        grid=(n // blk,),
        in_specs=[pl.BlockSpec((blk * ROW_CHUNKS, LANES), lambda i: (i, 0))],
        out_specs=pl.BlockSpec((blk * ROW_WORDS, LANES), lambda i: (i, 0)),
        out_shape=jax.ShapeDtypeStruct((n * ROW_WORDS, LANES), jnp.int32),
        compiler_params=_cparams(("parallel",)),
        name="pack_table",
    )(tab.astype(F32).reshape(n * ROW_CHUNKS, LANES))


def _gather_rows(tab_ref, e_ref, ts, slot_refs, view):
    for j0 in range(0, PEER_SLOTS, view):
        ids = [e_ref.at[t, pl.ds(j0, view)] for t in ts]
        for jj in range(view):
            j = j0 + jj
            for row_ids, slot_ref in zip(ids, slot_refs):
                slot_ref[j * ROW_WORDS:(j + 1) * ROW_WORDS, :] = \
                    tab_ref[pl.ds(pl.multiple_of(row_ids[jj], ROW_WORDS), ROW_WORDS), :]


def _slot_rows(slot_ref):
    return pltpu.bitcast(slot_ref[...], BF16)


def _slotted_token_loop(tb, nslot, group, gather, compute):
    groups = [tuple(range(s, s + group)) for s in range(0, nslot, group)]
    for ss in groups:
        gather(list(ss), ss)

    def body(i, carry):
        t0 = nslot * i
        for ss in groups:
            for s in ss:
                compute(t0 + s, s)
            gather([jnp.minimum(t0 + nslot + s, tb - 1) for s in ss], ss)
        return carry

    lax.fori_loop(0, tb // nslot, body, 0)


def _chunk_mask():
    n = PEER_SLOTS * ROW_CHUNKS
    r = lax.broadcasted_iota(jnp.int32, (ROW_CHUNKS, n), 0)
    c = lax.broadcasted_iota(jnp.int32, (ROW_CHUNKS, n), 1)
    return (c % ROW_CHUNKS == r).astype(F32)


def _gelu(a):
    return 0.5 * a * (1.0 + lax.erf(a * (2.0 ** -0.5)))


def _peer_u_kernel(e_ref, xn_ref, gate_ref, tab_ref, w_ref, slots_ref, drow_ref):
    tb = xn_ref.shape[0]
    mask = _chunk_mask()
    slots = [slots_ref.at[s] for s in range(slots_ref.shape[0])]

    def gather(ts, ss):
        _gather_rows(tab_ref, e_ref, ts, [slots[s] for s in ss], PEER_VIEW_U)

    def compute(t, s):
        xh, xl = _split_bf16(xn_ref[t])
        d = lax.dot_general(jnp.concatenate([xh, xl], axis=0), _slot_rows(slots[s]), NT,
                            preferred_element_type=F32)
        e = (d[0:ROW_CHUNKS] + d[ROW_CHUNKS:2 * ROW_CHUNKS]) * mask
        drow_ref[pl.ds(t, 1), :] = jnp.sum(e, axis=0, keepdims=True)

    _slotted_token_loop(tb, len(slots), PEER_GROUP_U, gather, compute)
    n = PEER_SLOTS * ROW_CHUNKS
    r = lax.broadcasted_iota(jnp.int32, (n, PEER_SLOTS), 0)
    c = lax.broadcasted_iota(jnp.int32, (n, PEER_SLOTS), 1)
    comp = (r // ROW_CHUNKS == c).astype(BF16)
    dp = _split3_bf16(drow_ref[...])
    dot = functools.partial(jnp.dot, preferred_element_type=F32)
    a = dot(dp[0], comp) + (dot(dp[1], comp) + dot(dp[2], comp))
    w_ref[...] = gate_ref[...] * _gelu(a)


def _peer_v_kernel(e_ref, w_ref, h_ref, tab_ref, o_ref, slots_ref, wexp_ref):
    tb = h_ref.shape[0]
    slots = [slots_ref.at[s] for s in range(slots_ref.shape[0])]
    mask = _chunk_mask()
    n = PEER_SLOTS * ROW_CHUNKS
    r = lax.broadcasted_iota(jnp.int32, (PEER_SLOTS, n), 0)
    c = lax.broadcasted_iota(jnp.int32, (PEER_SLOTS, n), 1)
    expand = (c // ROW_CHUNKS == r).astype(BF16)
    wp = _split3_bf16(w_ref[...])
    dot = functools.partial(jnp.dot, preferred_element_type=F32)
    wexp_ref[...] = dot(wp[0], expand) + (dot(wp[1], expand) + dot(wp[2], expand))

    def gather(ts, ss):
        _gather_rows(tab_ref, e_ref, ts, [slots[s] for s in ss], PEER_VIEW_V)

    def compute(t, s):
        wm =wexp_ref[pl.ds(t, 1), :] * mask
        wh, wl = _split_bf16(wm)
        d = jnp.dot(jnp.concatenate([wh, wl], axis=0), _slot_rows(slots[s]), preferred_element_type=F32)
        o_ref[t] = h_ref[t] + (d[0:ROW_CHUNKS] + d[ROW_CHUNKS:2 * ROW_CHUNKS])

    _slotted_token_loop(tb, len(slots), PEER_GROUP_V, gather, compute)


def _table_spec(tab):
    return pl.BlockSpec(tab.shape, lambda i: (0, 0), pipeline_mode=pl.Buffered(1))


def _peer_u(e, xn3, gate, tab, tb):
    t = e.shape[0]
    n = PEER_SLOTS * ROW_CHUNKS
    return pl.pallas_call(
        _peer_u_kernel,
        grid=(t // tb,),
        in_specs=[pl.BlockSpec((tb, PEER_SLOTS), lambda i: (i, 0), memory_space=pltpu.SMEM),
                  pl.BlockSpec((tb, ROW_CHUNKS, LANES), lambda i: (i, 0, 0)),
                  pl.BlockSpec((tb, PEER_SLOTS), lambda i: (i, 0)),
                  _table_spec(tab)],
        out_specs=pl.BlockSpec((tb, PEER_SLOTS), lambda i: (i, 0)),
        out_shape=jax.ShapeDtypeStruct((t, PEER_SLOTS), F32),
        scratch_shapes=[pltpu.VMEM((PEER_NSLOT_U, PEER_SLOTS * ROW_WORDS, LANES), jnp.int32), pltpu.VMEM((tb, n), F32)],
        compiler_params=_cparams(("arbitrary",)),
        name="peer_u",
    )(e, xn3, gate, tab)


def _peer_v(e, w, h3, tab, tb):
    t = e.shape[0]
    n = PEER_SLOTS * ROW_CHUNKS
    return pl.pallas_call(
        _peer_v_kernel,
        grid=(t // tb,),
        in_specs=[pl.BlockSpec((tb, PEER_SLOTS), lambda i: (i, 0), memory_space=pltpu.SMEM),
                  pl.BlockSpec((tb, PEER_SLOTS), lambda i: (i, 0)),
                  pl.BlockSpec((tb, ROW_CHUNKS, LANES), lambda i: (i, 0, 0)),
                  _table_spec(tab)],
        out_specs=pl.BlockSpec((tb, ROW_CHUNKS, LANES), lambda i: (i, 0, 0)),
        out_shape=jax.ShapeDtypeStruct((t, ROW_CHUNKS, LANES), F32),
        scratch_shapes=[pltpu.VMEM((PEER_NSLOT_V, PEER_SLOTS * ROW_WORDS, LANES), jnp.int32), pltpu.VMEM((tb, n), F32)],
        compiler_params=_cparams(("arbitrary",)),
        name="peer_v",
    )(e, w, h3, tab)


def _ple_kernel(h_ref, p_ref, gple_ref, wg_ref, wp_ref, gfin_ref, y_ref):
    dot = functools.partial(jnp.dot, preferred_element_type=F32)
    h = h_ref[...].reshape(h_ref.shape[0], D_MODEL)
    gate = _sigmoid(dot(_rms(h, gple_ref[...]).astype(BF16), wg_ref[...]))
    h = h + dot(p_ref[...].astype(BF16), wp_ref[...]) * gate
    y_ref[...] = _rms(h, gfin_ref[...])


def _ple(h, p, gple, wg, wp, gfin, tm):
    t = h.shape[0]
    full = lambda shape: pl.BlockSpec(shape, lambda i: (0, 0))
    row = lambda n: pl.BlockSpec((tm, n), lambda i: (i, 0))
    return pl.pallas_call(
        _ple_kernel,
        grid=(t // tm,),
        in_specs=[pl.BlockSpec((tm, ROW_CHUNKS, LANES), lambda i: (i, 0, 0)), row(PLE_DIM), full((1, D_MODEL)),
                  full((D_MODEL, D_MODEL)), full((PLE_DIM, D_MODEL)), full((1, D_MODEL))],
        out_specs=row(D_MODEL),
        out_shape=jax.ShapeDtypeStruct((t, D_MODEL), F32),
        compiler_params=_cparams(("parallel",)),
        name="ple_final",
    )(h, p, gple, wg, wp, gfin)


def _row_tile(n, want):
    t = want
    while n % t:
        t //= 2
    return t


def _prep_weights(w_in, conv_w, a_log, dt_bias, gdn_onorm, b_f, fox_onorm, w_out, norm_mix, norm_ffn,
                  peer_wq, peer_keys, peer_u, peer_v, norm_ple, w_ple_proj, w_ple_gate, norm_final):
    w_main = jnp.concatenate([w_in[:, :OFF1], w_in[:, OFF3:OFF4], w_in[:, OFF4:OFF5]], axis=1).astype(BF16)
    w_small = jnp.concatenate([w_in[:, OFF1:OFF3], w_in[:, OFF5:],
                               jnp.zeros((D_MODEL, LANES - 2 * H_A - H_B), w_in.dtype)], axis=1).astype(BF16)
    pad = lambda v, lo: jnp.pad(v.astype(F32), (lo, LANES - lo - v.shape[0])).reshape(1, LANES)
    bias = pad(dt_bias, H_A) + pad(b_f, 2 * H_A)
    alog = pad(a_log, H_A)
    r = lambda v: v.astype(F32).reshape(1, -1)
    return dict(
        w_main=w_main, w_small=w_small, bias=bias, alog=alog, conv_w=conv_w.astype(F32),
        gdn_onorm=r(gdn_onorm), fox_onorm=r(fox_onorm), w_out=w_out.astype(BF16),
        norm_mix=r(norm_mix), norm_ffn=r(norm_ffn), wq=peer_wq.astype(BF16), keys=peer_keys.astype(F32),
        utab=_pack_table(peer_u), vtab=_pack_table(peer_v), norm_ple=r(norm_ple),
        w_proj=w_ple_proj.astype(BF16), w_gate=w_ple_gate.astype(BF16), norm_final=r(norm_final))


def _layer(x, p, prefix, s0, past_k, past_v, past_lf, w):
    b, t, _ = x.shape
    n = b * t
    tm = _row_tile(n, 512)
    h = x.reshape(n, D_MODEL)
    conv_in, zg, qb, kb, vb, small = _inproj(h, w["norm_mix"], w["w_main"], w["w_small"], w["bias"], w["alog"], tm)
    conv3 = conv_in.reshape(b, t, CONV_DIM)
    small3 = small.reshape(b, t, LANES)
    prefix8 = jnp.zeros((b, SUBLANES, CONV_DIM), F32)
    if prefix is not None:
        prefix8 = prefix8.at[:, SUBLANES - (CONV_W - 1):, :].set(prefix.astype(F32))
    if s0 is None:
        s0 = jnp.zeros((b, H_A, DK_A, DV_A), F32)
    o_a, s_new = _gdn(conv3, small3, zg.reshape(b, t, A_WIDTH), prefix8, s0.astype(F32), w["conv_w"], w["gdn_onorm"],
                      nb=2 if b % 2 == 0 else 1)

    q3 = qb.reshape(b, t, B_WIDTH)
    k3 = kb.reshape(b, t, B_WIDTH)
    v3 = vb.reshape(b, t, B_WIDTH)
    if past_k is None:
        tq = _row_tile(t, 512)
        qx, kx, vx = _fox_prep(small3, q3, k3, v3, tq)
        o_b = _fox_prompt(qx, kx, vx, w["fox_onorm"], tq)
    else:
        pl_len = past_k.shape[1]
        lf_past = jnp.pad(past_lf.astype(F32), ((0, 0), (0, 0), (2 * H_A, LANES - 2 * H_A - H_B)))
        ccol, crow = _fox_cumsum(jnp.concatenate([lf_past, small3], axis=1))
        k_all = jnp.concatenate([past_k.reshape(b, pl_len, B_WIDTH).astype(F32), k3], axis=1)
        v_all = jnp.concatenate([past_v.reshape(b, pl_len, B_WIDTH).astype(F32), v3], axis=1)
        o_b = _fox_sample(q3, k_all, v_all, ccol[:, pl_len:, :], crow, w["fox_onorm"])

    h1, xn, qp = _outproj(o_a.reshape(n, A_WIDTH), o_b.reshape(n, B_WIDTH), h, w["w_out"], w["norm_ffn"], w["wq"], tm)
    e, gate = _peer_route(qp, w["keys"], _row_tile(n, 128))
    tb = _row_tile(n, 256)
    wgt = _peer_u(e, xn, gate, w["utab"], tb)
    h2 = _peer_v(e, wgt, h1, w["vtab"], tb)
    y = _ple(h2, p.reshape(n, PLE_DIM), w["norm_ple"], w["w_gate"], w["w_proj"],
             w["norm_final"], tm)
    conv_state = conv3[:, t - (CONV_W - 1):, :]
    return (y.reshape(b, t, D_MODEL), conv_state, s_new, k3.reshape(b, t, H_B, DH_B), v3.reshape(b, t, H_B, DH_B),
            small3[:, :, 2 * H_A:2 * H_A + H_B])


def kernel(x_prompt, x_sample, p_prompt, p_sample, cache_conv, state_gdn, cache_fox_k, cache_fox_v, cache_fox_logf, w_in, conv_w, a_log, dt_bias, gdn_onorm, b_f, fox_onorm, w_out, norm_mix, norm_ffn, peer_wq, peer_keys, peer_u, peer_v, norm_ple, w_ple_proj, w_ple_gate, norm_final):
    assert w_in.shape[0] == 1, "single-layer step"
    w = _prep_weights(w_in[0], conv_w[0], a_log[0], dt_bias[0], gdn_onorm[0], b_f[0], fox_onorm[0], w_out[0],
                      norm_mix[0], norm_ffn[0], peer_wq[0], peer_keys[0], peer_u[0], peer_v[0], norm_ple[0],
                      w_ple_proj[0], w_ple_gate[0], norm_final)
    ys, c2, s2, k2, v2, l2 = _layer(x_sample, p_sample[0], cache_conv[0], state_gdn[0], cache_fox_k[0],
                                    cache_fox_v[0], cache_fox_logf[0], w)
    yp, c1, s1, k1, v1, l1 = _layer(x_prompt, p_prompt[0], None, None, None, None, None, w)
    st = lambda a: a[None]
    return (yp, ys, st(c1), st(s1), st(k1), st(v1), st(l1), st(c2), st(s2), st(k2), st(v2), st(l2))
```

```python
import functools
import math

import jax
import jax.numpy as jnp
from jax import lax
from jax.experimental import pallas as pl
from jax.experimental.pallas import tpu as pltpu

F32 = jnp.float32
BF16 = jnp.bfloat16

D_MODEL = 1024
H_A, DK_A, DV_A = 4, 128, 128
CONV_W = 4
CONV_DIM = H_A * (2 * DK_A + DV_A)
A_WIDTH = H_A * DV_A
H_B, DH_B = 8, 64
B_WIDTH = H_B * DH_B
OFF1 = CONV_DIM
OFF2 = OFF1 + H_A
OFF3 = OFF2 + H_A
OFF4 = OFF3 + A_WIDTH
OFF5 = OFF4 + 3 * B_WIDTH
N_KEYS = 128
PEER_HEADS = 8
PEER_TOPK = 16
PEER_SLOTS = PEER_HEADS * PEER_TOPK
PLE_DIM = 256
EPS = 1e-6
GDN_CHUNK = 64

LANES = 128
SUBLANES = 8
VMEM_LIMIT = 56 * 1024 * 1024

NEG_BIG = -1e30


def _cparams(sem, vmem=VMEM_LIMIT):
    return pltpu.CompilerParams(dimension_semantics=sem, vmem_limit_bytes=vmem)


def _split_bf16(x):
    hi = x.astype(BF16)
    lo = (x - hi.astype(F32)).astype(BF16)
    return hi, lo


def _mm(a, b, passes=1, dims=(((1,), (0,)), ((), ()))):
    dg = functools.partial(lax.dot_general, dimension_numbers=dims, preferred_element_type=F32)
    if passes == 1:
        return dg(a.astype(BF16), b.astype(BF16))
    ah, al = _split_bf16(a)
    bh, bl = _split_bf16(b)
    return dg(ah, bh) + (dg(al, bh) + dg(ah, bl))


NT = (((1,), (1,)), ((), ()))


def _rms(x, g):
    return x * lax.rsqrt(jnp.mean(x * x, axis=-1, keepdims=True) + EPS) * g


def _softplus(x):
    return jnp.maximum(x, 0.0) + jnp.log1p(jnp.exp(-jnp.abs(x)))


def _sigmoid(x):
    return 1.0 / (1.0 + jnp.exp(-x))


def _inproj_kernel(h_ref, g_ref, w_ref, ws_ref, bias_ref, alog_ref,
                   conv_ref, gate_ref, qb_ref, kb_ref, vb_ref, small_ref):
    xn = _rms(h_ref[...], g_ref[...]).astype(BF16)
    dot = functools.partial(jnp.dot, preferred_element_type=F32)
    conv_ref[...] = dot(xn, w_ref[:, 0:CONV_DIM])
    gate_ref[...] = dot(xn, w_ref[:, CONV_DIM:CONV_DIM + A_WIDTH])
    o = CONV_DIM + A_WIDTH
    qb_ref[...] = (dot(xn, w_ref[:, o:o + B_WIDTH]) * (DH_B ** -0.5)).astype(BF16)
    kb_ref[...] = dot(xn, w_ref[:, o + B_WIDTH:o + 2 * B_WIDTH])
    vb_ref[...] = dot(xn, w_ref[:, o + 2 * B_WIDTH:o + 3 * B_WIDTH])
    z = dot(xn, ws_ref[...]) + bias_ref[...]
    lane = lax.broadcasted_iota(jnp.int32, z.shape, 1)
    small_ref[...] = jnp.where(lane < H_A, _sigmoid(z),
                               jnp.where(lane < 2 * H_A, -jnp.exp(alog_ref[...]) * _softplus(z), -_softplus(-z)))


def _inproj(h, g, w_main, w_small, bias, alog, tm):
    t = h.shape[0]
    nm = w_main.shape[1]
    full = lambda shape: pl.BlockSpec(shape, lambda i: (0, 0))
    row = lambda n: pl.BlockSpec((tm, n), lambda i: (i, 0))
    return pl.pallas_call(
        _inproj_kernel,
        grid=(t // tm,),
        in_specs=[row(D_MODEL), full((1, D_MODEL)), full((D_MODEL, nm)), full((D_MODEL, LANES)),
                  full((1, LANES)), full((1, LANES))],
        out_specs=[row(CONV_DIM), row(A_WIDTH), row(B_WIDTH), row(B_WIDTH), row(B_WIDTH), row(LANES)],
        out_shape=[jax.ShapeDtypeStruct((t, CONV_DIM), F32), jax.ShapeDtypeStruct((t, A_WIDTH), F32),
                   jax.ShapeDtypeStruct((t, B_WIDTH), BF16), jax.ShapeDtypeStruct((t, B_WIDTH), F32),
                   jax.ShapeDtypeStruct((t, B_WIDTH), F32), jax.ShapeDtypeStruct((t, LANES), F32)],
        compiler_params=_cparams(("parallel",)),
        name="inproj",
    )(h, g, w_main, w_small, bias, alog)


def _split3_bf16(x):
    h1 = x.astype(BF16)
    r = x - h1.astype(F32)
    h2 = r.astype(BF16)
    h3 = (r - h2.astype(F32)).astype(BF16)
    return h1, h2, h3


def _cumsum_kernel(lf_ref, ccol_ref, crow_ref, *, chunk):
    tk = lf_ref.shape[1]
    r = lax.broadcasted_iota(jnp.int32, (chunk, chunk), 0)
    c = lax.broadcasted_iota(jnp.int32, (chunk, chunk), 1)
    tril = (r >= c).astype(BF16)
    er = lax.broadcasted_iota(jnp.int32, (SUBLANES, LANES), 0)
    ec = lax.broadcasted_iota(jnp.int32, (SUBLANES, LANES), 1)
    pick = (ec == er + SUBLANES).astype(BF16)
    dot = functools.partial(jnp.dot, preferred_element_type=F32)
    carry = jnp.zeros((1, LANES), F32)
    for i in range(tk // chunk):
        sl = pl.ds(i * chunk, chunk)
        parts = _split3_bf16(lf_ref[0, sl, :])
        cs = dot(tril, parts[0]) + (dot(tril, parts[1]) + dot(tril, parts[2])) + carry
        ccol_ref[0, sl, :] = cs
        carry = cs[chunk - 1:chunk, :]
        cparts = _split3_bf16(cs)
        dg = functools.partial(lax.dot_general, dimension_numbers=NT, preferred_element_type=F32)
        crow_ref[0, :, sl] = dg(pick, cparts[0]) + (dg(pick, cparts[1]) + dg(pick, cparts[2]))


def _fox_cumsum(lf):
    b, tk, _ = lf.shape
    chunk = tk if tk <= 2048 else 512
    return pl.pallas_call(
        functools.partial(_cumsum_kernel, chunk=chunk),
        grid=(b,),
        in_specs=[pl.BlockSpec((1, tk, LANES), lambda i: (i, 0, 0))],
        out_specs=[pl.BlockSpec((1, tk, LANES), lambda i: (i, 0, 0)),
                   pl.BlockSpec((1, SUBLANES, tk), lambda i: (i, 0, 0))],
        out_shape=[jax.ShapeDtypeStruct((b, tk, LANES), F32), jax.ShapeDtypeStruct((b, SUBLANES, tk), F32)],
        compiler_params=_cparams(("parallel",)),
        name="fox_cumsum",
    )(lf)


GDN_PASSES = 1
GDN_STATE_PASSES = 3
TN = (((0,), (0,)), ((), ()))


def _unit_lower_inverses(ms, row, col):
    size = ms[0].shape[0]
    mm = functools.partial(_mm, passes=GDN_PASSES)
    eye = (row == col).astype(F32)
    diag = (row // 8) == (col // 8)
    ns = [jnp.where(diag, -m, 0.0) for m in ms]
    n2s = [mm(n, n) for n in ns]
    n4s = [mm(n2, n2) for n2 in n2s]
    ts = [eye + n for n in ns]
    ts = [t + mm(t, n2) for t, n2 in zip(ts, n2s)]
    ts = [t + mm(t, n4) for t, n4 in zip(ts, n4s)]
    blk = 8
    while blk < size:
        sel = ((row // (2 * blk)) == (col // (2 * blk))) & ((row // blk) != (col // blk))
        tls = [mm(t, jnp.where(sel, m, 0.0)) for t, m in zip(ts, ms)]
        ts = [t - mm(tl, t) for t, tl in zip(ts, tls)]
        blk *= 2
    return ts


def _gdn_kernel(u_ref, small_ref, zg_ref, prefix_ref, s0_ref, cw_ref, onorm_ref,
                o_ref, s_ref, ubuf_ref):
    L = GDN_CHUNK
    nb = u_ref.shape[0]
    mm = functools.partial(_mm, passes=GDN_PASSES)
    dot = functools.partial(jnp.dot, preferred_element_type=F32)
    dg = functools.partial(lax.dot_general, dimension_numbers=NT, preferred_element_type=F32)

    @pl.when(pl.program_id(1) == 0)
    def _():
        ubuf_ref[:, 0:SUBLANES, :] = prefix_ref[...]
        s_ref[...] = s0_ref[...]

    row = lax.broadcasted_iota(jnp.int32, (L, L), 0)
    col = lax.broadcasted_iota(jnp.int32, (L, L), 1)
    causal = row >= col
    strict = row > col
    tril = causal.astype(BF16)
    er = lax.broadcasted_iota(jnp.int32, (SUBLANES, LANES), 0)
    ec = lax.broadcasted_iota(jnp.int32, (SUBLANES, LANES), 1)
    pick = (ec == er + H_A).astype(BF16)

    cs, smalls, g_alls, g_rowss = [], [], [], []
    for b in range(nb):
        ubuf_ref[b, SUBLANES:SUBLANES + L, :] = u_ref[b]
        y = ubuf_ref[b, SUBLANES - 3:SUBLANES - 3 + L, :] * cw_ref[0:1, :]
        for j in range(1, CONV_W):
            y = y + ubuf_ref[b, SUBLANES - 3 + j:SUBLANES - 3 + j + L, :] * cw_ref[j:j + 1, :]
        tail = ubuf_ref[b, L:L + SUBLANES, :]
        ubuf_ref[b, 0:SUBLANES, :] = tail
        cs.append(y * _sigmoid(y))
        small = small_ref[b]
        sp = _split3_bf16(small)
        g_all = dot(tril, sp[0]) + (dot(tril, sp[1]) + dot(tril, sp[2]))
        gp = _split3_bf16(g_all)
        smalls.append(small)
        g_alls.append(g_all)
        g_rowss.append(dg(pick, gp[0]) + (dg(pick, gp[1]) + dg(pick, gp[2])))

    chains = [(b, h) for b in range(nb) for h in range(H_A)]
    qs, ks, vs, betas, gcols, decays, kbs = [], [], [], [], [], [], []
    for b, h in chains:
        c = cs[b]
        qh = c[:, h * DK_A:(h + 1) * DK_A]
        kh = c[:, H_A * DK_A + h * DK_A:H_A * DK_A + (h + 1) * DK_A]
        qs.append(qh * lax.rsqrt(jnp.sum(qh * qh, axis=-1, keepdims=True) + EPS) * (DK_A ** -0.5))
        kh = kh * lax.rsqrt(jnp.sum(kh * kh, axis=-1, keepdims=True) + EPS)
        ks.append(kh)
        vs.append(c[:, 2 * H_A * DK_A + h * DV_A:2 * H_A * DK_A + (h + 1) * DV_A])
        beta = smalls[b][:, h:h + 1]
        g_col = g_alls[b][:, H_A + h:H_A + h + 1]
        diff = g_col - g_rowss[b][h:h + 1, :]
        betas.append(beta)
        gcols.append(g_col)
        decays.append(jnp.where(causal, jnp.exp(jnp.where(causal, diff, 0.0)), 0.0))
        kbs.append(kh * beta)
    ms = [jnp.where(strict, mm(kb, k, dims=NT) * d, 0.0) for kb, k, d in zip(kbs, ks, decays)]
    ts = _unit_lower_inverses(ms, row, col)
    egs = [jnp.exp(g) for g in gcols]
    s_olds = [s_ref[b, h] for b, h in chains]
    sol_vs = [mm(t, v * beta) for t, v, beta in zip(ts, vs, betas)]
    sol_ks = [mm(t, kb * eg) for t, kb, eg in zip(ts, kbs, egs)]
    mm_state = functools.partial(_mm, passes=GDN_STATE_PASSES)
    u_news = [sv - mm_state(sk, s) for sv, sk, s in zip(sol_vs, sol_ks, s_olds)]
    attns = [_mm(q, k, dims=NT) * d for q, k, d in zip(qs, ks, decays)]
    outs = [_mm(q * eg, s) + _mm(a, u) for q, eg, s, a, u in zip(qs, egs, s_olds, attns, u_news)]
    for (b, h), k, g, s, u, o in zip(chains, ks, gcols, s_olds, u_news, outs):
        g_last = g[L - 1:L, :]
        s_ref[b, h] = jnp.exp(g_last) * s + mm_state(k * jnp.exp(g_last - g), u, dims=TN)
        zgh = zg_ref[b, :, h * DV_A:(h + 1) * DV_A]
        o_ref[b, :, h * DV_A:(h + 1) * DV_A] = _rms(o, onorm_ref[...]) * (zgh * _sigmoid(zgh))


def _gdn(u, small, zg, prefix8, s0, conv_w, onorm, nb):
    b, t, _ = u.shape
    L = GDN_CHUNK
    tile = lambda n: pl.BlockSpec((nb, L, n), lambda i, j: (i, j, 0))
    return pl.pallas_call(
        _gdn_kernel,
        grid=(b // nb, t // L),
        in_specs=[tile(CONV_DIM), tile(LANES), tile(A_WIDTH),
                  pl.BlockSpec((nb, SUBLANES, CONV_DIM), lambda i, j: (i, 0, 0)),
                  pl.BlockSpec((nb, H_A, DK_A, DV_A), lambda i, j: (i, 0, 0, 0)),
                  pl.BlockSpec((CONV_W, CONV_DIM), lambda i, j: (0, 0)),
                  pl.BlockSpec((1, DV_A), lambda i, j: (0, 0))],
        out_specs=[tile(A_WIDTH), pl.BlockSpec((nb, H_A, DK_A, DV_A), lambda i, j: (i, 0, 0, 0))],
        out_shape=[jax.ShapeDtypeStruct((b, t, A_WIDTH), F32), jax.ShapeDtypeStruct((b, H_A, DK_A, DV_A), F32)],
        scratch_shapes=[pltpu.VMEM((nb, SUBLANES + L, CONV_DIM), F32)],
        compiler_params=_cparams(("parallel", "arbitrary")),
        name="gdn",
    )(u, small, zg, prefix8, s0, conv_w, onorm)


FOX_GROUP = LANES
FOX_XW = H_B * FOX_GROUP
FOX_C0 = DH_B
FOX_C1 = DH_B + 3
FOX_HEAD_GROUP = 4
FOX_DIAG_ROWS = LANES


def _fox_prep_kernel(lf_ref, q_ref, k_ref, v_ref, qx_ref, kx_ref, vx_ref, carry_ref, place_ref):
    L = lf_ref.shape[1]
    dot = functools.partial(jnp.dot, preferred_element_type=F32)

    @pl.when(pl.program_id(1) == 0)
    def _():
        carry_ref[...] = jnp.zeros(carry_ref.shape, F32)
        r = lax.broadcasted_iota(jnp.int32, (B_WIDTH, FOX_XW), 0)
        c = lax.broadcasted_iota(jnp.int32, (B_WIDTH, FOX_XW), 1)
        d = c % FOX_GROUP
        place_ref[...] = ((d < DH_B) & (r == (c // FOX_GROUP) * DH_B + d)).astype(BF16)

    r = lax.broadcasted_iota(jnp.int32, (L, L), 0)
    c = lax.broadcasted_iota(jnp.int32, (L, L), 1)
    tril = (r >= c).astype(BF16)
    parts = _split3_bf16(lf_ref[0])
    cs = dot(tril, parts[0]) + (dot(tril, parts[1]) + dot(tril, parts[2])) + carry_ref[...]
    carry_ref[...] = cs[L - 1:L, :]
    cparts = _split3_bf16(cs)

    rr = lax.broadcasted_iota(jnp.int32, (LANES, FOX_XW), 0)
    cc = lax.broadcasted_iota(jnp.int32, (LANES, FOX_XW), 1)
    head_lane = rr == 2 * H_A + cc // FOX_GROUP
    dd = cc % FOX_GROUP
    lane1 = lax.broadcasted_iota(jnp.int32, (1, FOX_XW), 1) % FOX_GROUP
    qx = dot(q_ref[0], place_ref[...])
    kx = dot(k_ref[0].astype(BF16), place_ref[...])
    vx = dot(v_ref[0].astype(BF16), place_ref[...])
    for i in range(3):
        qx = qx + dot(cparts[i], (head_lane & (dd == FOX_C0 + i)).astype(BF16))
        kx = kx + dot(-cparts[i], (head_lane & (dd == FOX_C1 + i)).astype(BF16))
    qx = qx + ((lane1 >= FOX_C1) & (lane1 < FOX_C1 + 3)).astype(F32)
    kx = kx + ((lane1 >= FOX_C0) & (lane1 < FOX_C0 + 3)).astype(F32)
    vx = vx + (lane1 == DH_B).astype(F32)
    qx_ref[0] = qx.astype(BF16)
    kx_ref[0] = kx.astype(BF16)
    vx_ref[0] = vx.astype(BF16)


def _fox_prep(lf, q, k, v, tl):
    b, t, _ = q.shape
    blk = lambda w: pl.BlockSpec((1, tl, w), lambda i, j: (i, j, 0))
    xs = jax.ShapeDtypeStruct((b, t, FOX_XW), BF16)
    return pl.pallas_call(
        _fox_prep_kernel,
        grid=(b, t // tl),
        in_specs=[blk(LANES), blk(B_WIDTH), blk(B_WIDTH), blk(B_WIDTH)],
        out_specs=[blk(FOX_XW), blk(FOX_XW), blk(FOX_XW)],
        out_shape=[xs, xs, xs],
        scratch_shapes=[pltpu.VMEM((1, LANES), F32), pltpu.VMEM((B_WIDTH, FOX_XW), BF16)],
        compiler_params=_cparams(("parallel", "arbitrary")),
        name="fox_prep",
    )(lf, q, k, v)


def _fox_flash_kernel(qx_ref, kx_ref, vx_ref, onorm_ref, o_ref, m_ref, acc_ref):
    qi = pl.program_id(1)
    kj = pl.program_id(2)
    tq = qx_ref.shape[1]
    tk = kx_ref.shape[1]

    @pl.when(kj == 0)
    def _():
        m_ref[...] = jnp.full(m_ref.shape, NEG_BIG, F32)
        acc_ref[...] = jnp.zeros(acc_ref.shape, F32)

    def attend(r0, nr, nc, diagonal):
        rows = slice(r0, r0 + nr)
        if diagonal:
            keep = (lax.broadcasted_iota(jnp.int32, (nr, nr), 0) >= lax.broadcasted_iota(jnp.int32, (nr, nr), 1))
        for h0 in range(0, H_B, FOX_HEAD_GROUP):
            heads = range(h0, h0 + FOX_HEAD_GROUP)
            hss = [slice(h * FOX_GROUP, (h + 1) * FOX_GROUP) for h in heads]
            ss = [lax.dot_general(qx_ref[0, rows, hs], kx_ref[0, 0:nc, hs], NT, preferred_element_type=F32)
                  for hs in hss]
            tiles = [[s[:, j * LANES:(j + 1) * LANES] for j in range(nc // LANES)] for s in ss]
            if diagonal:
                for t in tiles:
                    t[-1] = jnp.where(keep, t[-1], NEG_BIG)
            m_olds = [m_ref[h, rows] for h in heads]
            m_news = []
            for m_old, t in zip(m_olds, tiles):
                tmax = functools.reduce(jnp.maximum, t)
                m_news.append(jnp.maximum(m_old, jnp.broadcast_to(jnp.max(tmax, axis=-1, keepdims=True),
                                                                  m_old.shape)))
            ps = [jnp.concatenate([jnp.exp(x - m_new) for x in t], axis=1).astype(BF16)
                  for t, m_new in zip(tiles, m_news)]
            pvs = [jnp.dot(p, vx_ref[0, 0:nc, hs], preferred_element_type=F32) for p, hs in zip(ps, hss)]
            for h, m_old, m_new, pv in zip(heads, m_olds, m_news, pvs):
                acc_ref[h, rows] = jnp.exp(m_old - m_new) * acc_ref[h, rows] + pv
                m_ref[h, rows] = m_new

    @pl.when(kj < qi)
    def _():
        attend(0, tq, tk, False)

    @pl.when(kj == qi)
    def _():
        for i in range(tq // FOX_DIAG_ROWS):
            attend(i * FOX_DIAG_ROWS, FOX_DIAG_ROWS, (i + 1) * FOX_DIAG_ROWS, True)
        value_lane = lax.broadcasted_iota(jnp.int32, (1, FOX_GROUP), 1) < DH_B
        gain = jnp.concatenate([onorm_ref[...], onorm_ref[...]], axis=1)
        for h in range(0, H_B, 2):
            pair = []
            for hh in (h, h + 1):
                acc = acc_ref[hh]
                o = jnp.where(value_lane, acc, 0.0) / acc[:, DH_B:DH_B + 1]
                ms = jnp.sum(o * o, axis=-1, keepdims=True) * (1.0 / DH_B)
                pair.append(o * lax.rsqrt(ms + EPS) * gain)
            o_ref[0, :, h * DH_B:(h + 2) * DH_B] = pair[0] + pltpu.roll(pair[1], DH_B, axis=1)


def _fox_prompt(qx, kx, vx, onorm, tq):
    b, t, _ = qx.shape
    n = t // tq
    qspec = lambda w: pl.BlockSpec((1, tq, w), lambda i, a, c: (i, a, 0))
    kspec = lambda w: pl.BlockSpec((1, tq, w), lambda i, a, c: (i, jnp.minimum(a, c), 0))
    return pl.pallas_call(
        _fox_flash_kernel,
        grid=(b, n, n),
        in_specs=[qspec(FOX_XW), kspec(FOX_XW), kspec(FOX_XW), pl.BlockSpec((1, DH_B), lambda i, a, c: (0, 0))],
        out_specs=qspec(B_WIDTH),
        out_shape=jax.ShapeDtypeStruct((b, t, B_WIDTH), F32),
        scratch_shapes=[pltpu.VMEM((H_B, tq, FOX_GROUP), F32), pltpu.VMEM((H_B, tq, FOX_GROUP), F32)],
        compiler_params=_cparams(("parallel", "parallel", "arbitrary")),
        name="fox_prompt",
    )(qx, kx, vx, onorm)


def _fox_sample_kernel(q_ref, k_ref, v_ref, ccol_ref, crow_ref, onorm_ref, o_ref):
    tq = q_ref.shape[1]
    tk = k_ref.shape[1]
    row = lax.broadcasted_iota(jnp.int32, (tq, tk), 0) + (tk - tq)
    col = lax.broadcasted_iota(jnp.int32, (tq, tk), 1)
    keep = row >= col
    for h in range(H_B):
        hs = slice(h * DH_B, (h + 1) * DH_B)
        s = lax.dot_general(q_ref[0, :, hs], k_ref[0, :, hs].astype(BF16), NT, preferred_element_type=F32)
        s = s + (ccol_ref[0, :, SUBLANES + h:SUBLANES + h + 1] - crow_ref[0, h:h + 1, :])
        s = jnp.where(keep, s, NEG_BIG)
        p = jnp.exp(s - jnp.max(s, axis=-1, keepdims=True))
        l = jnp.sum(p, axis=-1, keepdims=True)
        o = jnp.dot(p.astype(BF16), v_ref[0, :, hs].astype(BF16), preferred_element_type=F32) / l
        o_ref[0, :, hs] = _rms(o, onorm_ref[...])


def _fox_sample(q, k_all, v_all, ccol_q, crow, onorm):
    b, tq, _ = q.shape
    tk = k_all.shape[1]
    blk = lambda n, w: pl.BlockSpec((1, n, w), lambda i: (i, 0, 0))
    return pl.pallas_call(
        _fox_sample_kernel,
        grid=(b,),
        in_specs=[blk(tq, B_WIDTH), blk(tk, B_WIDTH), blk(tk, B_WIDTH), blk(tq, LANES), blk(SUBLANES, tk),
                  pl.BlockSpec((1, DH_B), lambda i: (0, 0))],
        out_specs=blk(tq, B_WIDTH),
        out_shape=jax.ShapeDtypeStruct((b, tq, B_WIDTH), F32),
        compiler_params=_cparams(("parallel",)),
        name="fox_sample",
    )(q, k_all, v_all, ccol_q, crow, onorm)


def _outproj_kernel(oa_ref, ob_ref, h_ref, wo_ref, g_ref, wq_ref, h1_ref, xn_ref, qp_ref):
    dot = functools.partial(jnp.dot, preferred_element_type=F32)
    h1 = h_ref[...] + dot(oa_ref[...].astype(BF16), wo_ref[0:A_WIDTH, :]) \
        + dot(ob_ref[...].astype(BF16), wo_ref[A_WIDTH:A_WIDTH + B_WIDTH, :])
    h1_ref[...] = h1.reshape(h1_ref.shape)
    xn = _rms(h1, g_ref[...])
    xn_ref[...] = xn.reshape(xn_ref.shape)
    qp_ref[...] = dot(xn.astype(BF16), wq_ref[...])


def _outproj(oa, ob, h, w_out, g, wq, tm):
    t = h.shape[0]
    nq = wq.shape[1]
    full = lambda shape: pl.BlockSpec(shape, lambda i: (0, 0))
    row = lambda n: pl.BlockSpec((tm, n), lambda i: (i, 0))
    tok3 = pl.BlockSpec((tm, ROW_CHUNKS, LANES), lambda i: (i, 0, 0))
    return pl.pallas_call(
        _outproj_kernel,
        grid=(t // tm,),
        in_specs=[row(A_WIDTH), row(B_WIDTH), row(D_MODEL), full((D_MODEL, D_MODEL)), full((1, D_MODEL)),
                  full((D_MODEL, nq))],
        out_specs=[tok3, tok3, row(nq)],
        out_shape=[jax.ShapeDtypeStruct((t, ROW_CHUNKS, LANES), F32), jax.ShapeDtypeStruct((t, ROW_CHUNKS, LANES), F32),
                   jax.ShapeDtypeStruct((t, nq), F32)],
        compiler_params=_cparams(("parallel",)),
        name="outproj",
    )(oa, ob, h, w_out, g, wq)


def _topk_rows(ss, k, payloads=None):
    n = ss[0].shape[0]
    rid = lax.broadcasted_iota(jnp.int32, ss[0].shape, 0).astype(F32)
    vals = [[] for _ in ss]
    picks = [[] for _ in ss]
    for _ in range(k):
        ms = [jnp.max(s, axis=0, keepdims=True) for s in ss]
        idxs = [jnp.min(jnp.where(s == m, rid, float(n)), axis=0, keepdims=True) for s, m in zip(ss, ms)]
        hits = [rid == idx for idx in idxs]
        for i, (m, idx, hit) in enumerate(zip(ms, idxs, hits)):
            vals[i].append(m)
            picks[i].append(idx if payloads is None else
                            jnp.sum(jnp.where(hit, payloads[i], 0.0), axis=0, keepdims=True))
        ss = [jnp.where(hit, -jnp.inf, s) for s, hit in zip(ss, hits)]
    return [jnp.concatenate(v, axis=0) for v in vals], [jnp.concatenate(p, axis=0) for p in picks]


PEER_PAIRS = [(i, j) for i in range(PEER_TOPK) for j in range(PEER_TOPK) if (i + 1) * (j + 1) <= PEER_TOPK]
PEER_PAIR_ROWS = -(-len(PEER_PAIRS) // SUBLANES) * SUBLANES
PEER_ROUTE_GROUP = 8


def _pair_selectors():
    sel = [[[1.0 if (r < len(PEER_PAIRS) and PEER_PAIRS[r][side] == i) else 0.0 for i in range(PEER_TOPK)]
            for r in range(PEER_PAIR_ROWS)] for side in range(2)]
    return jnp.asarray(sel, BF16)


def _select_rows(sel, x, exact_f32):
    dot = functools.partial(jnp.dot, preferred_element_type=F32)
    if not exact_f32:
        return dot(sel, x.astype(BF16))
    p = _split3_bf16(x)
    return dot(sel, p[0]) + dot(sel, p[1]) + dot(sel, p[2])


def _peer_route_kernel(qp_ref, keys_ref, sel_ref, e_ref, gate_ref):
    tb = qp_ref.shape[0]
    dq = LANES
    k = PEER_TOPK
    valid = lax.broadcasted_iota(jnp.int32, (PEER_PAIR_ROWS, tb), 0) < len(PEER_PAIRS)
    es, gs = [], []
    for h0 in range(0, PEER_HEADS, PEER_ROUTE_GROUP):
        heads = range(h0, h0 + PEER_ROUTE_GROUP)
        sts = [_mm(keys_ref[c], qp_ref[:, (2 * h + c) * dq:(2 * h + c + 1) * dq], passes=3, dims=NT)
               for h in heads for c in range(2)]
        svs, sis = _topk_rows(sts, k)
        cands, ecands = [], []
        for g in range(len(heads)):
            sv0, sv1, si0, si1 = svs[2 * g], svs[2 * g + 1], sis[2 * g], sis[2 * g + 1]
            cand = _select_rows(sel_ref[0], sv0, True) + _select_rows(sel_ref[1], sv1, True)
            cands.append(jnp.where(valid, cand, -jnp.inf))
            ecands.append((_select_rows(sel_ref[0], si0, False) * N_KEYS + _select_rows(sel_ref[1], si1, False))
                          * ROW_WORDS)
        cvs, epicks = _topk_rows(cands, k, payloads=ecands)
        for cv, e in zip(cvs, epicks):
            p = jnp.exp(cv - cv[0:1, :])
            gs.append(p / jnp.sum(p, axis=0, keepdims=True))
            es.append(e)
    e_all = jnp.concatenate(es, axis=0)
    g_all = jnp.concatenate(gs, axis=0)
    for j in range(tb // LANES):
        e_ref[j * LANES:(j + 1) * LANES, :] = e_all[:, j * LANES:(j + 1) * LANES].T.astype(jnp.int32)
        gate_ref[j * LANES:(j + 1) * LANES, :] = g_all[:, j * LANES:(j + 1) * LANES].T


def _peer_route(qp, keys, tb):
    t = qp.shape[0]
    sel = _pair_selectors()
    return pl.pallas_call(
        _peer_route_kernel,
        grid=(t // tb,),
        in_specs=[pl.BlockSpec((tb, qp.shape[1]), lambda i: (i, 0)),
                  pl.BlockSpec(keys.shape, lambda i: (0, 0, 0)),
                  pl.BlockSpec(sel.shape, lambda i: (0, 0, 0))],
        out_specs=[pl.BlockSpec((tb, PEER_SLOTS), lambda i: (i, 0)), pl.BlockSpec((tb, PEER_SLOTS), lambda i: (i, 0))],
        out_shape=[jax.ShapeDtypeStruct((t, PEER_SLOTS), jnp.int32), jax.ShapeDtypeStruct((t, PEER_SLOTS), F32)],
        compiler_params=_cparams(("parallel",)),
        name="peer_route",
    )(qp, keys, sel)


ROW_WORDS = D_MODEL // 2 // LANES
ROW_CHUNKS = D_MODEL // LANES
PEER_NSLOT_U, PEER_GROUP_U, PEER_VIEW_U = 4, 4, 8
PEER_NSLOT_V, PEER_GROUP_V, PEER_VIEW_V = 2, 1, 128


def _pack_kernel(t_ref, o_ref):
    o_ref[...] = pltpu.bitcast(t_ref[...].astype(BF16), jnp.int32)


def _pack_table(tab, blk=1024):
    n = tab.shape[0]
    return pl.pallas_call(
        _pack_kernel,
        grid=(n // blk,),
        in_specs=[pl.BlockSpec((blk * ROW_CHUNKS, LANES), lambda i: (i, 0))],
        out_specs=pl.BlockSpec((blk * ROW_WORDS, LANES), lambda i: (i, 0)),
        out_shape=jax.ShapeDtypeStruct((n * ROW_WORDS, LANES), jnp.int32),
        compiler_params=_cparams(("parallel",)),
        name="pack_table",
    )(tab.astype(F32).reshape(n * ROW_CHUNKS, LANES))


def _gather_rows(tab_ref, e_ref, ts, slot_refs, view):
    for j0 in range(0, PEER_SLOTS, view):
        ids = [e_ref.at[t, pl.ds(j0, view)] for t in ts]
        for jj in range(view):
            j = j0 + jj
            for row_ids, slot_ref in zip(ids, slot_refs):
                slot_ref[j * ROW_WORDS:(j + 1) * ROW_WORDS, :] = \
                    tab_ref[pl.ds(pl.multiple_of(row_ids[jj], ROW_WORDS), ROW_WORDS), :]


def _slot_rows(slot_ref):
    return pltpu.bitcast(slot_ref[...], BF16)


def _slotted_token_loop(tb, nslot, group, gather, compute):
    groups = [tuple(range(s, s + group)) for s in range(0, nslot, group)]
    for ss in groups:
        gather(list(ss), ss)

    def body(i, carry):
        t0 = nslot * i
        for ss in groups:
            for s in ss:
                compute(t0 + s, s)
            gather([jnp.minimum(t0 + nslot + s, tb - 1) for s in ss], ss)
        return carry

    lax.fori_loop(0, tb // nslot, body, 0)


def _chunk_mask():
    n = PEER_SLOTS * ROW_CHUNKS
    r = lax.broadcasted_iota(jnp.int32, (ROW_CHUNKS, n), 0)
    c = lax.broadcasted_iota(jnp.int32, (ROW_CHUNKS, n), 1)
    return (c % ROW_CHUNKS == r).astype(F32)


def _gelu(a):
    return 0.5 * a * (1.0 + lax.erf(a * (2.0 ** -0.5)))


def _peer_u_kernel(e_ref, xn_ref, gate_ref, tab_ref, w_ref, slots_ref, drow_ref):
    tb = xn_ref.shape[0]
    mask = _chunk_mask()
    slots = [slots_ref.at[s] for s in range(slots_ref.shape[0])]

    def gather(ts, ss):
        _gather_rows(tab_ref, e_ref, ts, [slots[s] for s in ss], PEER_VIEW_U)

    def compute(t, s):
        xh, xl = _split_bf16(xn_ref[t])
        d = lax.dot_general(jnp.concatenate([xh, xl], axis=0), _slot_rows(slots[s]), NT,
                            preferred_element_type=F32)
        e = (d[0:ROW_CHUNKS] + d[ROW_CHUNKS:2 * ROW_CHUNKS]) * mask
        drow_ref[pl.ds(t, 1), :] = jnp.sum(e, axis=0, keepdims=True)

    _slotted_token_loop(tb, len(slots), PEER_GROUP_U, gather, compute)
    n = PEER_SLOTS * ROW_CHUNKS
    r = lax.broadcasted_iota(jnp.int32, (n, PEER_SLOTS), 0)
    c = lax.broadcasted_iota(jnp.int32, (n, PEER_SLOTS), 1)
    comp = (r // ROW_CHUNKS == c).astype(BF16)
    dp = _split3_bf16(drow_ref[...])
    dot = functools.partial(jnp.dot, preferred_element_type=F32)
    a = dot(dp[0], comp) + (dot(dp[1], comp) + dot(dp[2], comp))
    w_ref[...] = gate_ref[...] * _gelu(a)


def _peer_v_kernel(e_ref, w_ref, h_ref, tab_ref, o_ref, slots_ref, wexp_ref):
    tb = h_ref.shape[0]
    slots = [slots_ref.at[s] for s in range(slots_ref.shape[0])]
    mask = _chunk_mask()
    n = PEER_SLOTS * ROW_CHUNKS
    r = lax.broadcasted_iota(jnp.int32, (PEER_SLOTS, n), 0)
    c = lax.broadcasted_iota(jnp.int32, (PEER_SLOTS, n), 1)
    expand = (c // ROW_CHUNKS == r).astype(BF16)
    wp = _split3_bf16(w_ref[...])
    dot = functools.partial(jnp.dot, preferred_element_type=F32)
    wexp_ref[...] = dot(wp[0], expand) + (dot(wp[1], expand) + dot(wp[2], expand))

    def gather(ts, ss):
        _gather_rows(tab_ref, e_ref, ts, [slots[s] for s in ss], PEER_VIEW_V)

    def compute(t, s):
        wm =wexp_ref[pl.ds(t, 1), :] * mask
        wh, wl = _split_bf16(wm)
        d = jnp.dot(jnp.concatenate([wh, wl], axis=0), _slot_rows(slots[s]), preferred_element_type=F32)
        o_ref[t] = h_ref[t] + (d[0:ROW_CHUNKS] + d[ROW_CHUNKS:2 * ROW_CHUNKS])

    _slotted_token_loop(tb, len(slots), PEER_GROUP_V, gather, compute)


def _table_spec(tab):
    return pl.BlockSpec(tab.shape, lambda i: (0, 0), pipeline_mode=pl.Buffered(1))


def _peer_u(e, xn3, gate, tab, tb):
    t = e.shape[0]
    n = PEER_SLOTS * ROW_CHUNKS
    return pl.pallas_call(
        _peer_u_kernel,
        grid=(t // tb,),
        in_specs=[pl.BlockSpec((tb, PEER_SLOTS), lambda i: (i, 0), memory_space=pltpu.SMEM),
                  pl.BlockSpec((tb, ROW_CHUNKS, LANES), lambda i: (i, 0, 0)),
                  pl.BlockSpec((tb, PEER_SLOTS), lambda i: (i, 0)),
                  _table_spec(tab)],
        out_specs=pl.BlockSpec((tb, PEER_SLOTS), lambda i: (i, 0)),
        out_shape=jax.ShapeDtypeStruct((t, PEER_SLOTS), F32),
        scratch_shapes=[pltpu.VMEM((PEER_NSLOT_U, PEER_SLOTS * ROW_WORDS, LANES), jnp.int32), pltpu.VMEM((tb, n), F32)],
        compiler_params=_cparams(("arbitrary",)),
        name="peer_u",
    )(e, xn3, gate, tab)


def _peer_v(e, w, h3, tab, tb):
    t = e.shape[0]
    n = PEER_SLOTS * ROW_CHUNKS
    return pl.pallas_call(
        _peer_v_kernel,
        grid=(t // tb,),
        in_specs=[pl.BlockSpec((tb, PEER_SLOTS), lambda i: (i, 0), memory_space=pltpu.SMEM),
                  pl.BlockSpec((tb, PEER_SLOTS), lambda i: (i, 0)),
                  pl.BlockSpec((tb, ROW_CHUNKS, LANES), lambda i: (i, 0, 0)),
                  _table_spec(tab)],
        out_specs=pl.BlockSpec((tb, ROW_CHUNKS, LANES), lambda i: (i, 0, 0)),
        out_shape=jax.ShapeDtypeStruct((t, ROW_CHUNKS, LANES), F32),
        scratch_shapes=[pltpu.VMEM((PEER_NSLOT_V, PEER_SLOTS * ROW_WORDS, LANES), jnp.int32), pltpu.VMEM((tb, n), F32)],
        compiler_params=_cparams(("arbitrary",)),
        name="peer_v",
    )(e, w, h3, tab)


def _ple_kernel(h_ref, p_ref, gple_ref, wg_ref, wp_ref, gfin_ref, y_ref):
    dot = functools.partial(jnp.dot, preferred_element_type=F32)
    h = h_ref[...].reshape(h_ref.shape[0], D_MODEL)
    gate = _sigmoid(dot(_rms(h, gple_ref[...]).astype(BF16), wg_ref[...]))
    h = h + dot(p_ref[...].astype(BF16), wp_ref[...]) * gate
    y_ref[...] = _rms(h, gfin_ref[...])


def _ple(h, p, gple, wg, wp, gfin, tm):
    t = h.shape[0]
    full = lambda shape: pl.BlockSpec(shape, lambda i: (0, 0))
    row = lambda n: pl.BlockSpec((tm, n), lambda i: (i, 0))
    return pl.pallas_call(
        _ple_kernel,
        grid=(t // tm,),
        in_specs=[pl.BlockSpec((tm, ROW_CHUNKS, LANES), lambda i: (i, 0, 0)), row(PLE_DIM), full((1, D_MODEL)),
                  full((D_MODEL, D_MODEL)), full((PLE_DIM, D_MODEL)), full((1, D_MODEL))],
        out_specs=row(D_MODEL),
        out_shape=jax.ShapeDtypeStruct((t, D_MODEL), F32),
        compiler_params=_cparams(("parallel",)),
        name="ple_final",
    )(h, p, gple, wg, wp, gfin)


def _row_tile(n, want):
    t = want
    while n % t:
        t //= 2
    return t


def _prep_weights(w_in, conv_w, a_log, dt_bias, gdn_onorm, b_f, fox_onorm, w_out, norm_mix, norm_ffn,
                  peer_wq, peer_keys, peer_u, peer_v, norm_ple, w_ple_proj, w_ple_gate, norm_final):
    w_main = jnp.concatenate([w_in[:, :OFF1], w_in[:, OFF3:OFF4], w_in[:, OFF4:OFF5]], axis=1).astype(BF16)
    w_small = jnp.concatenate([w_in[:, OFF1:OFF3], w_in[:, OFF5:],
                               jnp.zeros((D_MODEL, LANES - 2 * H_A - H_B), w_in.dtype)], axis=1).astype(BF16)
    pad = lambda v, lo: jnp.pad(v.astype(F32), (lo, LANES - lo - v.shape[0])).reshape(1, LANES)
    bias = pad(dt_bias, H_A) + pad(b_f, 2 * H_A)
    alog = pad(a_log, H_A)
    r = lambda v: v.astype(F32).reshape(1, -1)
    return dict(
        w_main=w_main, w_small=w_small, bias=bias, alog=alog, conv_w=conv_w.astype(F32),
        gdn_onorm=r(gdn_onorm), fox_onorm=r(fox_onorm), w_out=w_out.astype(BF16),
        norm_mix=r(norm_mix), norm_ffn=r(norm_ffn), wq=peer_wq.astype(BF16), keys=peer_keys.astype(F32),
        utab=_pack_table(peer_u), vtab=_pack_table(peer_v), norm_ple=r(norm_ple),
        w_proj=w_ple_proj.astype(BF16), w_gate=w_ple_gate.astype(BF16), norm_final=r(norm_final))


def _layer(x, p, prefix, s0, past_k, past_v, past_lf, w):
    b, t, _ = x.shape
    n = b * t
    tm = _row_tile(n, 512)
    h = x.reshape(n, D_MODEL)
    conv_in, zg, qb, kb, vb, small = _inproj(h, w["norm_mix"], w["w_main"], w["w_small"], w["bias"], w["alog"], tm)
    conv3 = conv_in.reshape(b, t, CONV_DIM)
    small3 = small.reshape(b, t, LANES)
    prefix8 = jnp.zeros((b, SUBLANES, CONV_DIM), F32)
    if prefix is not None:
        prefix8 = prefix8.at[:, SUBLANES - (CONV_W - 1):, :].set(prefix.astype(F32))
    if s0 is None:
        s0 = jnp.zeros((b, H_A, DK_A, DV_A), F32)
    o_a, s_new = _gdn(conv3, small3, zg.reshape(b, t, A_WIDTH), prefix8, s0.astype(F32), w["conv_w"], w["gdn_onorm"],
                      nb=4 if b % 4 == 0 else 1)

    q3 = qb.reshape(b, t, B_WIDTH)
    k3 = kb.reshape(b, t, B_WIDTH)
    v3 = vb.reshape(b, t, B_WIDTH)
    if past_k is None:
        tq = _row_tile(t, 512)
        qx, kx, vx = _fox_prep(small3, q3, k3, v3, tq)
        o_b = _fox_prompt(qx, kx, vx, w["fox_onorm"], tq)
    else:
        pl_len = past_k.shape[1]
        lf_past = jnp.pad(past_lf.astype(F32), ((0, 0), (0, 0), (2 * H_A, LANES - 2 * H_A - H_B)))
        ccol, crow = _fox_cumsum(jnp.concatenate([lf_past, small3], axis=1))
        k_all = jnp.concatenate([past_k.reshape(b, pl_len, B_WIDTH).astype(F32), k3], axis=1)
        v_all = jnp.concatenate([past_v.reshape(b, pl_len, B_WIDTH).astype(F32), v3], axis=1)
        o_b = _fox_sample(q3, k_all, v_all, ccol[:, pl_len:, :], crow, w["fox_onorm"])

    h1, xn, qp = _outproj(o_a.reshape(n, A_WIDTH), o_b.reshape(n, B_WIDTH), h, w["w_out"], w["norm_ffn"], w["wq"], tm)
    e, gate = _peer_route(qp, w["keys"], _row_tile(n, 128))
    tb = _row_tile(n, 256)
    wgt = _peer_u(e, xn, gate, w["utab"], tb)
    h2 = _peer_v(e, wgt, h1, w["vtab"], tb)
    y = _ple(h2, p.reshape(n, PLE_DIM), w["norm_ple"], w["w_gate"], w["w_proj"],
             w["norm_final"], tm)
    conv_state = conv3[:, t - (CONV_W - 1):, :]
    return (y.reshape(b, t, D_MODEL), conv_state, s_new, k3.reshape(b, t, H_B, DH_B), v3.reshape(b, t, H_B, DH_B),
            small3[:, :, 2 * H_A:2 * H_A + H_B])


def kernel(x_prompt, x_sample, p_prompt, p_sample, cache_conv, state_gdn, cache_fox_k, cache_fox_v, cache_fox_logf, w_in, conv_w, a_log, dt_bias, gdn_onorm, b_f, fox_onorm, w_out, norm_mix, norm_ffn, peer_wq, peer_keys, peer_u, peer_v, norm_ple, w_ple_proj, w_ple_gate, norm_final):
    assert w_in.shape[0] == 1, "single-layer step"
    w = _prep_weights(w_in[0], conv_w[0], a_log[0], dt_bias[0], gdn_onorm[0], b_f[0], fox_onorm[0], w_out[0],
                      norm_mix[0], norm_ffn[0], peer_wq[0], peer_keys[0], peer_u[0], peer_v[0], norm_ple[0],
                      w_ple_proj[0], w_ple_gate[0], norm_final)
    ys, c2, s2, k2, v2, l2 = _layer(x_sample, p_sample[0], cache_conv[0], state_gdn[0], cache_fox_k[0],
                                    cache_fox_v[0], cache_fox_logf[0], w)
    yp, c1, s1, k1, v1, l1 = _layer(x_prompt, p_prompt[0], None, None, None, None, None, w)
    st = lambda a: a[None]
    return (yp, ys, st(c1), st(s1), st(k1), st(v1), st(l1), st(c2), st(s2), st(k2), st(v2), st(l2))
```

```python
import functools
import math

import jax
import jax.numpy as jnp
from jax import lax
from jax.experimental import pallas as pl
from jax.experimental.pallas import tpu as pltpu

F32 = jnp.float32
BF16 = jnp.bfloat16

D_MODEL = 1024
H_A, DK_A, DV_A = 4, 128, 128
CONV_W = 4
CONV_DIM = H_A * (2 * DK_A + DV_A)
A_WIDTH = H_A * DV_A
H_B, DH_B = 8, 64
B_WIDTH = H_B * DH_B
OFF1 = CONV_DIM
OFF2 = OFF1 + H_A
OFF3 = OFF2 + H_A
OFF4 = OFF3 + A_WIDTH
OFF5 = OFF4 + 3 * B_WIDTH
N_KEYS = 128
PEER_HEADS = 8
PEER_TOPK = 16
PEER_SLOTS = PEER_HEADS * PEER_TOPK
PLE_DIM = 256
EPS = 1e-6
GDN_CHUNK = 64

LANES = 128
SUBLANES = 8
VMEM_LIMIT = 56 * 1024 * 1024

NEG_BIG = -1e30


def _cparams(sem, vmem=VMEM_LIMIT):
    return pltpu.CompilerParams(dimension_semantics=sem, vmem_limit_bytes=vmem)


def _split_bf16(x):
    hi = x.astype(BF16)
    lo = (x - hi.astype(F32)).astype(BF16)
    return hi, lo


def _mm(a, b, passes=1, dims=(((1,), (0,)), ((), ()))):
    dg = functools.partial(lax.dot_general, dimension_numbers=dims, preferred_element_type=F32)
    if passes == 1:
        return dg(a.astype(BF16), b.astype(BF16))
    ah, al = _split_bf16(a)
    bh, bl = _split_bf16(b)
    return dg(ah, bh) + (dg(al, bh) + dg(ah, bl))


NT = (((1,), (1,)), ((), ()))


def _rms(x, g):
    return x * lax.rsqrt(jnp.mean(x * x, axis=-1, keepdims=True) + EPS) * g


def _softplus(x):
    return jnp.maximum(x, 0.0) + jnp.log1p(jnp.exp(-jnp.abs(x)))


def _sigmoid(x):
    return 1.0 / (1.0 + jnp.exp(-x))


def _inproj_kernel(h_ref, g_ref, w_ref, ws_ref, bias_ref, alog_ref,
                   conv_ref, gate_ref, qb_ref, kb_ref, vb_ref, small_ref):
    xn = _rms(h_ref[...], g_ref[...]).astype(BF16)
    dot = functools.partial(jnp.dot, preferred_element_type=F32)
    conv_ref[...] = dot(xn, w_ref[:, 0:CONV_DIM])
    gate_ref[...] = dot(xn, w_ref[:, CONV_DIM:CONV_DIM + A_WIDTH])
    o = CONV_DIM + A_WIDTH
    qb_ref[...] = (dot(xn, w_ref[:, o:o + B_WIDTH]) * (DH_B ** -0.5)).astype(BF16)
    kb_ref[...] = dot(xn, w_ref[:, o + B_WIDTH:o + 2 * B_WIDTH])
    vb_ref[...] = dot(xn, w_ref[:, o + 2 * B_WIDTH:o + 3 * B_WIDTH])
    z = dot(xn, ws_ref[...]) + bias_ref[...]
    lane = lax.broadcasted_iota(jnp.int32, z.shape, 1)
    small_ref[...] = jnp.where(lane < H_A, _sigmoid(z),
                               jnp.where(lane < 2 * H_A, -jnp.exp(alog_ref[...]) * _softplus(z), -_softplus(-z)))


def _inproj(h, g, w_main, w_small, bias, alog, tm):
    t = h.shape[0]
    nm = w_main.shape[1]
    full = lambda shape: pl.BlockSpec(shape, lambda i: (0, 0))
    row = lambda n: pl.BlockSpec((tm, n), lambda i: (i, 0))
    return pl.pallas_call(
        _inproj_kernel,
        grid=(t // tm,),
        in_specs=[row(D_MODEL), full((1, D_MODEL)), full((D_MODEL, nm)), full((D_MODEL, LANES)),
                  full((1, LANES)), full((1, LANES))],
        out_specs=[row(CONV_DIM), row(A_WIDTH), row(B_WIDTH), row(B_WIDTH), row(B_WIDTH), row(LANES)],
        out_shape=[jax.ShapeDtypeStruct((t, CONV_DIM), F32), jax.ShapeDtypeStruct((t, A_WIDTH), F32),
                   jax.ShapeDtypeStruct((t, B_WIDTH), BF16), jax.ShapeDtypeStruct((t, B_WIDTH), F32),
                   jax.ShapeDtypeStruct((t, B_WIDTH), F32), jax.ShapeDtypeStruct((t, LANES), F32)],
        compiler_params=_cparams(("parallel",)),
        name="inproj",
    )(h, g, w_main, w_small, bias, alog)


def _split3_bf16(x):
    h1 = x.astype(BF16)
    r = x - h1.astype(F32)
    h2 = r.astype(BF16)
    h3 = (r - h2.astype(F32)).astype(BF16)
    return h1, h2, h3


def _cumsum_kernel(lf_ref, ccol_ref, crow_ref, *, chunk):
    tk = lf_ref.shape[1]
    r = lax.broadcasted_iota(jnp.int32, (chunk, chunk), 0)
    c = lax.broadcasted_iota(jnp.int32, (chunk, chunk), 1)
    tril = (r >= c).astype(BF16)
    er = lax.broadcasted_iota(jnp.int32, (SUBLANES, LANES), 0)
    ec = lax.broadcasted_iota(jnp.int32, (SUBLANES, LANES), 1)
    pick = (ec == er + SUBLANES).astype(BF16)
    dot = functools.partial(jnp.dot, preferred_element_type=F32)
    carry = jnp.zeros((1, LANES), F32)
    for i in range(tk // chunk):
        sl = pl.ds(i * chunk, chunk)
        parts = _split3_bf16(lf_ref[0, sl, :])
        cs = dot(tril, parts[0]) + (dot(tril, parts[1]) + dot(tril, parts[2])) + carry
        ccol_ref[0, sl, :] = cs
        carry = cs[chunk - 1:chunk, :]
        cparts = _split3_bf16(cs)
        dg = functools.partial(lax.dot_general, dimension_numbers=NT, preferred_element_type=F32)
        crow_ref[0, :, sl] = dg(pick, cparts[0]) + (dg(pick, cparts[1]) + dg(pick, cparts[2]))


def _fox_cumsum(lf):
    b, tk, _ = lf.shape
    chunk = tk if tk <= 2048 else 512
    return pl.pallas_call(
        functools.partial(_cumsum_kernel, chunk=chunk),
        grid=(b,),
        in_specs=[pl.BlockSpec((1, tk, LANES), lambda i: (i, 0, 0))],
        out_specs=[pl.BlockSpec((1, tk, LANES), lambda i: (i, 0, 0)),
                   pl.BlockSpec((1, SUBLANES, tk), lambda i: (i, 0, 0))],
        out_shape=[jax.ShapeDtypeStruct((b, tk, LANES), F32), jax.ShapeDtypeStruct((b, SUBLANES, tk), F32)],
        compiler_params=_cparams(("parallel",)),
        name="fox_cumsum",
    )(lf)


GDN_PASSES = 1
GDN_STATE_PASSES = 3
TN = (((0,), (0,)), ((), ()))


def _unit_lower_inverses(ms, row, col):
    size = ms[0].shape[0]
    mm = functools.partial(_mm, passes=GDN_PASSES)
    eye = (row == col).astype(F32)
    diag = (row // 8) == (col // 8)
    ns = [jnp.where(diag, -m, 0.0) for m in ms]
    n2s = [mm(n, n) for n in ns]
    n4s = [mm(n2, n2) for n2 in n2s]
    ts = [eye + n for n in ns]
    ts = [t + mm(t, n2) for t, n2 in zip(ts, n2s)]
    ts = [t + mm(t, n4) for t, n4 in zip(ts, n4s)]
    blk = 8
    while blk < size:
        sel = ((row // (2 * blk)) == (col // (2 * blk))) & ((row // blk) != (col // blk))
        tls = [mm(t, jnp.where(sel, m, 0.0)) for t, m in zip(ts, ms)]
        ts = [t - mm(tl, t) for t, tl in zip(ts, tls)]
        blk *= 2
    return ts


def _gdn_kernel(u_ref, small_ref, zg_ref, prefix_ref, s0_ref, cw_ref, onorm_ref,
                o_ref, s_ref, ubuf_ref):
    L = GDN_CHUNK
    nb = u_ref.shape[0]
    mm = functools.partial(_mm, passes=GDN_PASSES)
    dot = functools.partial(jnp.dot, preferred_element_type=F32)
    dg = functools.partial(lax.dot_general, dimension_numbers=NT, preferred_element_type=F32)

    @pl.when(pl.program_id(1) == 0)
    def _():
        ubuf_ref[:, 0:SUBLANES, :] = prefix_ref[...]
        s_ref[...] = s0_ref[...]

    row = lax.broadcasted_iota(jnp.int32, (L, L), 0)
    col = lax.broadcasted_iota(jnp.int32, (L, L), 1)
    causal = row >= col
    strict = row > col
    tril = causal.astype(BF16)
    er = lax.broadcasted_iota(jnp.int32, (SUBLANES, LANES), 0)
    ec = lax.broadcasted_iota(jnp.int32, (SUBLANES, LANES), 1)
    pick = (ec == er + H_A).astype(BF16)

    cs, smalls, g_alls, g_rowss = [], [], [], []
    for b in range(nb):
        ubuf_ref[b, SUBLANES:SUBLANES + L, :] = u_ref[b]
        y = ubuf_ref[b, SUBLANES - 3:SUBLANES - 3 + L, :] * cw_ref[0:1, :]
        for j in range(1, CONV_W):
            y = y + ubuf_ref[b, SUBLANES - 3 + j:SUBLANES - 3 + j + L, :] * cw_ref[j:j + 1, :]
        tail = ubuf_ref[b, L:L + SUBLANES, :]
        ubuf_ref[b, 0:SUBLANES, :] = tail
        cs.append(y * _sigmoid(y))
        small = small_ref[b]
        sp = _split3_bf16(small)
        g_all = dot(tril, sp[0]) + (dot(tril, sp[1]) + dot(tril, sp[2]))
        gp = _split3_bf16(g_all)
        smalls.append(small)
        g_alls.append(g_all)
        g_rowss.append(dg(pick, gp[0]) + (dg(pick, gp[1]) + dg(pick, gp[2])))

    chains = [(b, h) for b in range(nb) for h in range(H_A)]
    qs, ks, vs, betas, gcols, decays, kbs = [], [], [], [], [], [], []
    for b, h in chains:
        c = cs[b]
        qh = c[:, h * DK_A:(h + 1) * DK_A]
        kh = c[:, H_A * DK_A + h * DK_A:H_A * DK_A + (h + 1) * DK_A]
        qs.append(qh * lax.rsqrt(jnp.sum(qh * qh, axis=-1, keepdims=True) + EPS) * (DK_A ** -0.5))
        kh = kh * lax.rsqrt(jnp.sum(kh * kh, axis=-1, keepdims=True) + EPS)
        ks.append(kh)
        vs.append(c[:, 2 * H_A * DK_A + h * DV_A:2 * H_A * DK_A + (h + 1) * DV_A])
        beta = smalls[b][:, h:h + 1]
        g_col = g_alls[b][:, H_A + h:H_A + h + 1]
        diff = g_col - g_rowss[b][h:h + 1, :]
        betas.append(beta)
        gcols.append(g_col)
        decays.append(jnp.where(causal, jnp.exp(jnp.where(causal, diff, 0.0)), 0.0))
        kbs.append(kh * beta)
    ms = [jnp.where(strict, mm(kb, k, dims=NT) * d, 0.0) for kb, k, d in zip(kbs, ks, decays)]
    ts = _unit_lower_inverses(ms, row, col)
    egs = [jnp.exp(g) for g in gcols]
    s_olds = [s_ref[b, h] for b, h in chains]
    sol_vs = [mm(t, v * beta) for t, v, beta in zip(ts, vs, betas)]
    sol_ks = [mm(t, kb * eg) for t, kb, eg in zip(ts, kbs, egs)]
    mm_state = functools.partial(_mm, passes=GDN_STATE_PASSES)
    u_news = [sv - mm_state(sk, s) for sv, sk, s in zip(sol_vs, sol_ks, s_olds)]
    attns = [_mm(q, k, dims=NT) * d for q, k, d in zip(qs, ks, decays)]
    outs = [_mm(q * eg, s) + _mm(a, u) for q, eg, s, a, u in zip(qs, egs, s_olds, attns, u_news)]
    for (b, h), k, g, s, u, o in zip(chains, ks, gcols, s_olds, u_news, outs):
        g_last = g[L - 1:L, :]
        s_ref[b, h] = jnp.exp(g_last) * s + mm_state(k * jnp.exp(g_last - g), u, dims=TN)
        zgh = zg_ref[b, :, h * DV_A:(h + 1) * DV_A]
        o_ref[b, :, h * DV_A:(h + 1) * DV_A] = _rms(o, onorm_ref[...]) * (zgh * _sigmoid(zgh))


def _gdn(u, small, zg, prefix8, s0, conv_w, onorm, nb):
    b, t, _ = u.shape
    L = GDN_CHUNK
    tile = lambda n: pl.BlockSpec((nb, L, n), lambda i, j: (i, j, 0))
    return pl.pallas_call(
        _gdn_kernel,
        grid=(b // nb, t // L),
        in_specs=[tile(CONV_DIM), tile(LANES), tile(A_WIDTH),
                  pl.BlockSpec((nb, SUBLANES, CONV_DIM), lambda i, j: (i, 0, 0)),
                  pl.BlockSpec((nb, H_A, DK_A, DV_A), lambda i, j: (i, 0, 0, 0)),
                  pl.BlockSpec((CONV_W, CONV_DIM), lambda i, j: (0, 0)),
                  pl.BlockSpec((1, DV_A), lambda i, j: (0, 0))],
        out_specs=[tile(A_WIDTH), pl.BlockSpec((nb, H_A, DK_A, DV_A), lambda i, j: (i, 0, 0, 0))],
        out_shape=[jax.ShapeDtypeStruct((b, t, A_WIDTH), F32), jax.ShapeDtypeStruct((b, H_A, DK_A, DV_A), F32)],
        scratch_shapes=[pltpu.VMEM((nb, SUBLANES + L, CONV_DIM), F32)],
        compiler_params=_cparams(("parallel", "arbitrary")),
        name="gdn",
    )(u, small, zg, prefix8, s0, conv_w, onorm)


FOX_GROUP = LANES
FOX_XW = H_B * FOX_GROUP
FOX_C0 = DH_B
FOX_C1 = DH_B + 3
FOX_HEAD_GROUP = 4
FOX_DIAG_ROWS = LANES


def _fox_prep_kernel(lf_ref, q_ref, k_ref, v_ref, qx_ref, kx_ref, vx_ref, carry_ref, place_ref):
    L = lf_ref.shape[1]
    dot = functools.partial(jnp.dot, preferred_element_type=F32)

    @pl.when(pl.program_id(1) == 0)
    def _():
        carry_ref[...] = jnp.zeros(carry_ref.shape, F32)
        r = lax.broadcasted_iota(jnp.int32, (B_WIDTH, FOX_XW), 0)
        c = lax.broadcasted_iota(jnp.int32, (B_WIDTH, FOX_XW), 1)
        d = c % FOX_GROUP
        place_ref[...] = ((d < DH_B) & (r == (c // FOX_GROUP) * DH_B + d)).astype(BF16)

    r = lax.broadcasted_iota(jnp.int32, (L, L), 0)
    c = lax.broadcasted_iota(jnp.int32, (L, L), 1)
    tril = (r >= c).astype(BF16)
    parts = _split3_bf16(lf_ref[0])
    cs = dot(tril, parts[0]) + (dot(tril, parts[1]) + dot(tril, parts[2])) + carry_ref[...]
    carry_ref[...] = cs[L - 1:L, :]
    cparts = _split3_bf16(cs)

    rr = lax.broadcasted_iota(jnp.int32, (LANES, FOX_XW), 0)
    cc = lax.broadcasted_iota(jnp.int32, (LANES, FOX_XW), 1)
    head_lane = rr == 2 * H_A + cc // FOX_GROUP
    dd = cc % FOX_GROUP
    lane1 = lax.broadcasted_iota(jnp.int32, (1, FOX_XW), 1) % FOX_GROUP
    qx = dot(q_ref[0], place_ref[...])
    kx = dot(k_ref[0].astype(BF16), place_ref[...])
    vx = dot(v_ref[0].astype(BF16), place_ref[...])
    for i in range(3):
        qx = qx + dot(cparts[i], (head_lane & (dd == FOX_C0 + i)).astype(BF16))
        kx = kx + dot(-cparts[i], (head_lane & (dd == FOX_C1 + i)).astype(BF16))
    qx = qx + ((lane1 >= FOX_C1) & (lane1 < FOX_C1 + 3)).astype(F32)
    kx = kx + ((lane1 >= FOX_C0) & (lane1 < FOX_C0 + 3)).astype(F32)
    vx = vx + (lane1 == DH_B).astype(F32)
    qx_ref[0] = qx.astype(BF16)
    kx_ref[0] = kx.astype(BF16)
    vx_ref[0] = vx.astype(BF16)


def _fox_prep(lf, q, k, v, tl):
    b, t, _ = q.shape
    blk = lambda w: pl.BlockSpec((1, tl, w), lambda i, j: (i, j, 0))
    xs = jax.ShapeDtypeStruct((b, t, FOX_XW), BF16)
    return pl.pallas_call(
        _fox_prep_kernel,
        grid=(b, t // tl),
        in_specs=[blk(LANES), blk(B_WIDTH), blk(B_WIDTH), blk(B_WIDTH)],
        out_specs=[blk(FOX_XW), blk(FOX_XW), blk(FOX_XW)],
        out_shape=[xs, xs, xs],
        scratch_shapes=[pltpu.VMEM((1, LANES), F32), pltpu.VMEM((B_WIDTH, FOX_XW), BF16)],
        compiler_params=_cparams(("parallel", "arbitrary")),
        name="fox_prep",
    )(lf, q, k, v)


def _fox_flash_kernel(qx_ref, kx_ref, vx_ref, onorm_ref, o_ref, m_ref, acc_ref):
    qi = pl.program_id(1)
    kj = pl.program_id(2)
    tq = qx_ref.shape[1]
    tk = kx_ref.shape[1]

    @pl.when(kj == 0)
    def _():
        m_ref[...] = jnp.full(m_ref.shape, NEG_BIG, F32)
        acc_ref[...] = jnp.zeros(acc_ref.shape, F32)

    def attend(r0, nr, nc, diagonal):
        rows = slice(r0, r0 + nr)
        if diagonal:
            keep = (lax.broadcasted_iota(jnp.int32, (nr, nr), 0) >= lax.broadcasted_iota(jnp.int32, (nr, nr), 1))
        for h0 in range(0, H_B, FOX_HEAD_GROUP):
            heads = range(h0, h0 + FOX_HEAD_GROUP)
            hss = [slice(h * FOX_GROUP, (h + 1) * FOX_GROUP) for h in heads]
            ss = [lax.dot_general(qx_ref[0, rows, hs], kx_ref[0, 0:nc, hs], NT, preferred_element_type=F32)
                  for hs in hss]
            tiles = [[s[:, j * LANES:(j + 1) * LANES] for j in range(nc // LANES)] for s in ss]
            if diagonal:
                for t in tiles:
                    t[-1] = jnp.where(keep, t[-1], NEG_BIG)
            m_olds = [m_ref[h, rows] for h in heads]
            m_news = []
            for m_old, t in zip(m_olds, tiles):
                tmax = functools.reduce(jnp.maximum, t)
                m_news.append(jnp.maximum(m_old, jnp.broadcast_to(jnp.max(tmax, axis=-1, keepdims=True),
                                                                  m_old.shape)))
            ps = [jnp.concatenate([jnp.exp(x - m_new) for x in t], axis=1).astype(BF16)
                  for t, m_new in zip(tiles, m_news)]
            pvs = [jnp.dot(p, vx_ref[0, 0:nc, hs], preferred_element_type=F32) for p, hs in zip(ps, hss)]
            for h, m_old, m_new, pv in zip(heads, m_olds, m_news, pvs):
                acc_ref[h, rows] = jnp.exp(m_old - m_new) * acc_ref[h, rows] + pv
                m_ref[h, rows] = m_new

    @pl.when(kj < qi)
    def _():
        attend(0, tq, tk, False)

    @pl.when(kj == qi)
    def _():
        for i in range(tq // FOX_DIAG_ROWS):
            attend(i * FOX_DIAG_ROWS, FOX_DIAG_ROWS, (i + 1) * FOX_DIAG_ROWS, True)
        value_lane = lax.broadcasted_iota(jnp.int32, (1, FOX_GROUP), 1) < DH_B
        gain = jnp.concatenate([onorm_ref[...], onorm_ref[...]], axis=1)
        for h in range(0, H_B, 2):
            pair = []
            for hh in (h, h + 1):
                acc = acc_ref[hh]
                o = jnp.where(value_lane, acc, 0.0) / acc[:, DH_B:DH_B + 1]
                ms = jnp.sum(o * o, axis=-1, keepdims=True) * (1.0 / DH_B)
                pair.append(o * lax.rsqrt(ms + EPS) * gain)
            o_ref[0, :, h * DH_B:(h + 2) * DH_B] = pair[0] + pltpu.roll(pair[1], DH_B, axis=1)


def _fox_prompt(qx, kx, vx, onorm, tq):
    b, t, _ = qx.shape
    n = t // tq
    qspec = lambda w: pl.BlockSpec((1, tq, w), lambda i, a, c: (i, a, 0))
    kspec = lambda w: pl.BlockSpec((1, tq, w), lambda i, a, c: (i, jnp.minimum(a, c), 0))
    return pl.pallas_call(
        _fox_flash_kernel,
        grid=(b, n, n),
        in_specs=[qspec(FOX_XW), kspec(FOX_XW), kspec(FOX_XW), pl.BlockSpec((1, DH_B), lambda i, a, c: (0, 0))],
        out_specs=qspec(B_WIDTH),
        out_shape=jax.ShapeDtypeStruct((b, t, B_WIDTH), F32),
        scratch_shapes=[pltpu.VMEM((H_B, tq, FOX_GROUP), F32), pltpu.VMEM((H_B, tq, FOX_GROUP), F32)],
        compiler_params=_cparams(("parallel", "parallel", "arbitrary")),
        name="fox_prompt",
    )(qx, kx, vx, onorm)


def _fox_sample_kernel(q_ref, k_ref, v_ref, ccol_ref, crow_ref, onorm_ref, o_ref):
    tq = q_ref.shape[1]
    tk = k_ref.shape[1]
    row = lax.broadcasted_iota(jnp.int32, (tq, tk), 0) + (tk - tq)
    col = lax.broadcasted_iota(jnp.int32, (tq, tk), 1)
    keep = row >= col
    for h in range(H_B):
        hs = slice(h * DH_B, (h + 1) * DH_B)
        s = lax.dot_general(q_ref[0, :, hs], k_ref[0, :, hs].astype(BF16), NT, preferred_element_type=F32)
        s = s + (ccol_ref[0, :, SUBLANES + h:SUBLANES + h + 1] - crow_ref[0, h:h + 1, :])
        s = jnp.where(keep, s, NEG_BIG)
        p = jnp.exp(s - jnp.max(s, axis=-1, keepdims=True))
        l = jnp.sum(p, axis=-1, keepdims=True)
        o = jnp.dot(p.astype(BF16), v_ref[0, :, hs].astype(BF16), preferred_element_type=F32) / l
        o_ref[0, :, hs] = _rms(o, onorm_ref[...])


def _fox_sample(q, k_all, v_all, ccol_q, crow, onorm):
    b, tq, _ = q.shape
    tk = k_all.shape[1]
    blk = lambda n, w: pl.BlockSpec((1, n, w), lambda i: (i, 0, 0))
    return pl.pallas_call(
        _fox_sample_kernel,
        grid=(b,),
        in_specs=[blk(tq, B_WIDTH), blk(tk, B_WIDTH), blk(tk, B_WIDTH), blk(tq, LANES), blk(SUBLANES, tk),
                  pl.BlockSpec((1, DH_B), lambda i: (0, 0))],
        out_specs=blk(tq, B_WIDTH),
        out_shape=jax.ShapeDtypeStruct((b, tq, B_WIDTH), F32),
        compiler_params=_cparams(("parallel",)),
        name="fox_sample",
    )(q, k_all, v_all, ccol_q, crow, onorm)


def _outproj_kernel(oa_ref, ob_ref, h_ref, wo_ref, g_ref, wq_ref, h1_ref, xn_ref, qp_ref):
    dot = functools.partial(jnp.dot, preferred_element_type=F32)
    h1 = h_ref[...] + dot(oa_ref[...].astype(BF16), wo_ref[0:A_WIDTH, :]) \
        + dot(ob_ref[...].astype(BF16), wo_ref[A_WIDTH:A_WIDTH + B_WIDTH, :])
    h1_ref[...] = h1.reshape(h1_ref.shape)
    xn = _rms(h1, g_ref[...])
    xn_ref[...] = xn.reshape(xn_ref.shape)
    qp_ref[...] = dot(xn.astype(BF16), wq_ref[...])


def _outproj(oa, ob, h, w_out, g, wq, tm):
    t = h.shape[0]
    nq = wq.shape[1]
    full = lambda shape: pl.BlockSpec(shape, lambda i: (0, 0))
    row = lambda n: pl.BlockSpec((tm, n), lambda i: (i, 0))
    tok3 = pl.BlockSpec((tm, ROW_CHUNKS, LANES), lambda i: (i, 0, 0))
    return pl.pallas_call(
        _outproj_kernel,
        grid=(t // tm,),
        in_specs=[row(A_WIDTH), row(B_WIDTH), row(D_MODEL), full((D_MODEL, D_MODEL)), full((1, D_MODEL)),
                  full((D_MODEL, nq))],
        out_specs=[tok3, tok3, row(nq)],
        out_shape=[jax.ShapeDtypeStruct((t, ROW_CHUNKS, LANES), F32), jax.ShapeDtypeStruct((t, ROW_CHUNKS, LANES), F32),
                   jax.ShapeDtypeStruct((t, nq), F32)],
        compiler_params=_cparams(("parallel",)),
        name="outproj",
    )(oa, ob, h, w_out, g, wq)


def _topk_rows(ss, k, payloads=None):
    n = ss[0].shape[0]
    rid = lax.broadcasted_iota(jnp.int32, ss[0].shape, 0).astype(F32)
    rid_blocks = [rid[r:r + SUBLANES] for r in range(0, n, SUBLANES)]
    vals = [[] for _ in ss]
    picks = [[] for _ in ss]

    def block_argmax(s):
        level = [(s[r:r + SUBLANES], rb) for r, rb in zip(range(0, n, SUBLANES), rid_blocks)]
        while len(level) > 1:
            nxt = []
            for a, b in zip(level[0::2], level[1::2]):
                take = b[0] > a[0]
                nxt.append((jnp.maximum(a[0], b[0]), jnp.where(take, b[1], a[1])))
            if len(level) % 2:
                nxt.append(level[-1])
            level = nxt
        return level[0]

    for _ in range(k):
        tops = [block_argmax(s) for s in ss]
        ms = [jnp.max(v8, axis=0, keepdims=True) for v8, _ in tops]
        idxs = [jnp.min(jnp.where(v8 == m, r8, float(n)), axis=0, keepdims=True) for (v8, r8), m in zip(tops, ms)]
        hits = [rid == idx for idx in idxs]
        for i, (m, idx, hit) in enumerate(zip(ms, idxs, hits)):
            vals[i].append(m)
            picks[i].append(idx if payloads is None else
                            jnp.sum(jnp.where(hit, payloads[i], 0.0), axis=0, keepdims=True))
        ss = [jnp.where(hit, -jnp.inf, s) for s, hit in zip(ss, hits)]
    return [jnp.concatenate(v, axis=0) for v in vals], [jnp.concatenate(p, axis=0) for p in picks]


PEER_PAIRS = [(i, j) for i in range(PEER_TOPK) for j in range(PEER_TOPK) if (i + 1) * (j + 1) <= PEER_TOPK]
PEER_PAIR_ROWS = -(-len(PEER_PAIRS) // SUBLANES) * SUBLANES
PEER_ROUTE_GROUP = 8


def _pair_selectors():
    sel = [[[1.0 if (r < len(PEER_PAIRS) and PEER_PAIRS[r][side] == i) else 0.0 for i in range(PEER_TOPK)]
            for r in range(PEER_PAIR_ROWS)] for side in range(2)]
    return jnp.asarray(sel, BF16)


def _select_rows(sel, x, exact_f32):
    dot = functools.partial(jnp.dot, preferred_element_type=F32)
    if not exact_f32:
        return dot(sel, x.astype(BF16))
    p = _split3_bf16(x)
    return dot(sel, p[0]) + dot(sel, p[1]) + dot(sel, p[2])


def _peer_route_kernel(qp_ref, keys_ref, sel_ref, e_ref, gate_ref):
    tb = qp_ref.shape[0]
    dq = LANES
    k = PEER_TOPK
    valid = lax.broadcasted_iota(jnp.int32, (PEER_PAIR_ROWS, tb), 0) < len(PEER_PAIRS)
    es, gs = [], []
    for h0 in range(0, PEER_HEADS, PEER_ROUTE_GROUP):
        heads = range(h0, h0 + PEER_ROUTE_GROUP)
        sts = [_mm(keys_ref[c], qp_ref[:, (2 * h + c) * dq:(2 * h + c + 1) * dq], passes=3, dims=NT)
               for h in heads for c in range(2)]
        svs, sis = _topk_rows(sts, k)
        cands, ecands = [], []
        for g in range(len(heads)):
            sv0, sv1, si0, si1 = svs[2 * g], svs[2 * g + 1], sis[2 * g], sis[2 * g + 1]
            cand = _select_rows(sel_ref[0], sv0, True) + _select_rows(sel_ref[1], sv1, True)
            cands.append(jnp.where(valid, cand, -jnp.inf))
            ecands.append((_select_rows(sel_ref[0], si0, False) * N_KEYS + _select_rows(sel_ref[1], si1, False))
                          * ROW_WORDS)
        cvs, epicks = _topk_rows(cands, k, payloads=ecands)
        for cv, e in zip(cvs, epicks):
            p = jnp.exp(cv - cv[0:1, :])
            gs.append(p / jnp.sum(p, axis=0, keepdims=True))
            es.append(e)
    e_all = jnp.concatenate(es, axis=0)
    g_all = jnp.concatenate(gs, axis=0)
    for j in range(tb // LANES):
        e_ref[j * LANES:(j + 1) * LANES, :] = e_all[:, j * LANES:(j + 1) * LANES].T.astype(jnp.int32)
        gate_ref[j * LANES:(j + 1) * LANES, :] = g_all[:, j * LANES:(j + 1) * LANES].T


def _peer_route(qp, keys, tb):
    t = qp.shape[0]
    sel = _pair_selectors()
    return pl.pallas_call(
        _peer_route_kernel,
        grid=(t // tb,),
        in_specs=[pl.BlockSpec((tb, qp.shape[1]), lambda i: (i, 0)),
                  pl.BlockSpec(keys.shape, lambda i: (0, 0, 0)),
                  pl.BlockSpec(sel.shape, lambda i: (0, 0, 0))],
        out_specs=[pl.BlockSpec((tb, PEER_SLOTS), lambda i: (i, 0)), pl.BlockSpec((tb, PEER_SLOTS), lambda i: (i, 0))],
        out_shape=[jax.ShapeDtypeStruct((t, PEER_SLOTS), jnp.int32), jax.ShapeDtypeStruct((t, PEER_SLOTS), F32)],
        compiler_params=_cparams(("parallel",)),
        name="peer_route",
    )(qp, keys, sel)


ROW_WORDS = D_MODEL // 2 // LANES
ROW_CHUNKS = D_MODEL // LANES
PEER_NSLOT_U, PEER_GROUP_U, PEER_VIEW_U = 4, 4, 8
PEER_NSLOT_V, PEER_GROUP_V, PEER_VIEW_V = 2, 1, 128


def _pack_kernel(t_ref, o_ref):
    o_ref[...] = pltpu.bitcast(t_ref[...].astype(BF16), jnp.int32)


def _pack_table(tab, blk=1024):
    n = tab.shape[0]
    return pl.pallas_call(
        _pack_kernel,
        grid=(n // blk,),
        in_specs=[pl.BlockSpec((blk * ROW_CHUNKS, LANES), lambda i: (i, 0))],
        out_specs=pl.BlockSpec((blk * ROW_WORDS, LANES), lambda i: (i, 0)),
        out_shape=jax.ShapeDtypeStruct((n * ROW_WORDS, LANES), jnp.int32),
        compiler_params=_cparams(("parallel",)),
        name="pack_table",
    )(tab.astype(F32).reshape(n * ROW_CHUNKS, LANES))


def _gather_rows(tab_ref, e_ref, ts, slot_refs, view):
    for j0 in range(0, PEER_SLOTS, view):
        ids = [e_ref.at[t, pl.ds(j0, view)] for t in ts]
        for jj in range(view):
            j = j0 + jj
            for row_ids, slot_ref in zip(ids, slot_refs):
                slot_ref[j * ROW_WORDS:(j + 1) * ROW_WORDS, :] = \
                    tab_ref[pl.ds(pl.multiple_of(row_ids[jj], ROW_WORDS), ROW_WORDS), :]


def _slot_rows(slot_ref):
    return pltpu.bitcast(slot_ref[...], BF16)


def _slotted_token_loop(tb, nslot, group, gather, compute):
    groups = [tuple(range(s, s + group)) for s in range(0, nslot, group)]
    for ss in groups:
        gather(list(ss), ss)

    def body(i, carry):
        t0 = nslot * i
        for ss in groups:
            for s in ss:
                compute(t0 + s, s)
            gather([jnp.minimum(t0 + nslot + s, tb - 1) for s in ss], ss)
        return carry

    lax.fori_loop(0, tb // nslot, body, 0)


def _chunk_mask():
    n = PEER_SLOTS * ROW_CHUNKS
    r = lax.broadcasted_iota(jnp.int32, (ROW_CHUNKS, n), 0)
    c = lax.broadcasted_iota(jnp.int32, (ROW_CHUNKS, n), 1)
    return (c % ROW_CHUNKS == r).astype(F32)


def _gelu(a):
    return 0.5 * a * (1.0 + lax.erf(a * (2.0 ** -0.5)))


def _peer_u_kernel(e_ref, xn_ref, gate_ref, tab_ref, w_ref, slots_ref, drow_ref):
    tb = xn_ref.shape[0]
    mask = _chunk_mask()
    slots = [slots_ref.at[s] for s in range(slots_ref.shape[0])]

    def gather(ts, ss):
        _gather_rows(tab_ref, e_ref, ts, [slots[s] for s in ss], PEER_VIEW_U)

    def compute(t, s):
        xh, xl = _split_bf16(xn_ref[t])
        d = lax.dot_general(jnp.concatenate([xh, xl], axis=0), _slot_rows(slots[s]), NT,
                            preferred_element_type=F32)
        e = (d[0:ROW_CHUNKS] + d[ROW_CHUNKS:2 * ROW_CHUNKS]) * mask
        drow_ref[pl.ds(t, 1), :] = jnp.sum(e, axis=0, keepdims=True)

    _slotted_token_loop(tb, len(slots), PEER_GROUP_U, gather, compute)
    n = PEER_SLOTS * ROW_CHUNKS
    r = lax.broadcasted_iota(jnp.int32, (n, PEER_SLOTS), 0)
    c = lax.broadcasted_iota(jnp.int32, (n, PEER_SLOTS), 1)
    comp = (r // ROW_CHUNKS == c).astype(BF16)
    dp = _split3_bf16(drow_ref[...])
    dot = functools.partial(jnp.dot, preferred_element_type=F32)
    a = dot(dp[0], comp) + (dot(dp[1], comp) + dot(dp[2], comp))
    w_ref[...] = gate_ref[...] * _gelu(a)


def _peer_v_kernel(e_ref, w_ref, h_ref, tab_ref, o_ref, slots_ref, wexp_ref):
    tb = h_ref.shape[0]
    slots = [slots_ref.at[s] for s in range(slots_ref.shape[0])]
    mask = _chunk_mask()
    n = PEER_SLOTS * ROW_CHUNKS
    r = lax.broadcasted_iota(jnp.int32, (PEER_SLOTS, n), 0)
    c = lax.broadcasted_iota(jnp.int32, (PEER_SLOTS, n), 1)
    expand = (c // ROW_CHUNKS == r).astype(BF16)
    wp = _split3_bf16(w_ref[...])
    dot = functools.partial(jnp.dot, preferred_element_type=F32)
    wexp_ref[...] = dot(wp[0], expand) + (dot(wp[1], expand) + dot(wp[2], expand))

    def gather(ts, ss):
        _gather_rows(tab_ref, e_ref, ts, [slots[s] for s in ss], PEER_VIEW_V)

    def compute(t, s):
        wm =wexp_ref[pl.ds(t, 1), :] * mask
        wh, wl = _split_bf16(wm)
        d = jnp.dot(jnp.concatenate([wh, wl], axis=0), _slot_rows(slots[s]), preferred_element_type=F32)
        o_ref[t] = h_ref[t] + (d[0:ROW_CHUNKS] + d[ROW_CHUNKS:2 * ROW_CHUNKS])

    _slotted_token_loop(tb, len(slots), PEER_GROUP_V, gather, compute)


def _table_spec(tab):
    return pl.BlockSpec(tab.shape, lambda i: (0, 0), pipeline_mode=pl.Buffered(1))


def _peer_u(e, xn3, gate, tab, tb):
    t = e.shape[0]
    n = PEER_SLOTS * ROW_CHUNKS
    return pl.pallas_call(
        _peer_u_kernel,
        grid=(t // tb,),
        in_specs=[pl.BlockSpec((tb, PEER_SLOTS), lambda i: (i, 0), memory_space=pltpu.SMEM),
                  pl.BlockSpec((tb, ROW_CHUNKS, LANES), lambda i: (i, 0, 0)),
                  pl.BlockSpec((tb, PEER_SLOTS), lambda i: (i, 0)),
                  _table_spec(tab)],
        out_specs=pl.BlockSpec((tb, PEER_SLOTS), lambda i: (i, 0)),
        out_shape=jax.ShapeDtypeStruct((t, PEER_SLOTS), F32),
        scratch_shapes=[pltpu.VMEM((PEER_NSLOT_U, PEER_SLOTS * ROW_WORDS, LANES), jnp.int32), pltpu.VMEM((tb, n), F32)],
        compiler_params=_cparams(("arbitrary",)),
        name="peer_u",
    )(e, xn3, gate, tab)


def _peer_v(e, w, h3, tab, tb):
    t = e.shape[0]
    n = PEER_SLOTS * ROW_CHUNKS
    return pl.pallas_call(
        _peer_v_kernel,
        grid=(t // tb,),
        in_specs=[pl.BlockSpec((tb, PEER_SLOTS), lambda i: (i, 0), memory_space=pltpu.SMEM),
                  pl.BlockSpec((tb, PEER_SLOTS), lambda i: (i, 0)),
                  pl.BlockSpec((tb, ROW_CHUNKS, LANES), lambda i: (i, 0, 0)),
                  _table_spec(tab)],
        out_specs=pl.BlockSpec((tb, ROW_CHUNKS, LANES), lambda i: (i, 0, 0)),
        out_shape=jax.ShapeDtypeStruct((t, ROW_CHUNKS, LANES), F32),
        scratch_shapes=[pltpu.VMEM((PEER_NSLOT_V, PEER_SLOTS * ROW_WORDS, LANES), jnp.int32), pltpu.VMEM((tb, n), F32)],
        compiler_params=_cparams(("arbitrary",)),
        name="peer_v",
    )(e, w, h3, tab)


def _ple_kernel(h_ref, p_ref, gple_ref, wg_ref, wp_ref, gfin_ref, y_ref):
    dot = functools.partial(jnp.dot, preferred_element_type=F32)
    h = h_ref[...].reshape(h_ref.shape[0], D_MODEL)
    gate = _sigmoid(dot(_rms(h, gple_ref[...]).astype(BF16), wg_ref[...]))
    h = h + dot(p_ref[...].astype(BF16), wp_ref[...]) * gate
    y_ref[...] = _rms(h, gfin_ref[...])


def _ple(h, p, gple, wg, wp, gfin, tm):
    t = h.shape[0]
    full = lambda shape: pl.BlockSpec(shape, lambda i: (0, 0))
    row = lambda n: pl.BlockSpec((tm, n), lambda i: (i, 0))
    return pl.pallas_call(
        _ple_kernel,
        grid=(t // tm,),
        in_specs=[pl.BlockSpec((tm, ROW_CHUNKS, LANES), lambda i: (i, 0, 0)), row(PLE_DIM), full((1, D_MODEL)),
                  full((D_MODEL, D_MODEL)), full((PLE_DIM, D_MODEL)), full((1, D_MODEL))],
        out_specs=row(D_MODEL),
        out_shape=jax.ShapeDtypeStruct((t, D_MODEL), F32),
        compiler_params=_cparams(("parallel",)),
        name="ple_final",
    )(h, p, gple, wg, wp, gfin)


def _row_tile(n, want):
    t = want
    while n % t:
        t //= 2
    return t


def _prep_weights(w_in, conv_w, a_log, dt_bias, gdn_onorm, b_f, fox_onorm, w_out, norm_mix, norm_ffn,
                  peer_wq, peer_keys, peer_u, peer_v, norm_ple, w_ple_proj, w_ple_gate, norm_final):
    w_main = jnp.concatenate([w_in[:, :OFF1], w_in[:, OFF3:OFF4], w_in[:, OFF4:OFF5]], axis=1).astype(BF16)
    w_small = jnp.concatenate([w_in[:, OFF1:OFF3], w_in[:, OFF5:],
                               jnp.zeros((D_MODEL, LANES - 2 * H_A - H_B), w_in.dtype)], axis=1).astype(BF16)
    pad = lambda v, lo: jnp.pad(v.astype(F32), (lo, LANES - lo - v.shape[0])).reshape(1, LANES)
    bias = pad(dt_bias, H_A) + pad(b_f, 2 * H_A)
    alog = pad(a_log, H_A)
    r = lambda v: v.astype(F32).reshape(1, -1)
    return dict(
        w_main=w_main, w_small=w_small, bias=bias, alog=alog, conv_w=conv_w.astype(F32),
        gdn_onorm=r(gdn_onorm), fox_onorm=r(fox_onorm), w_out=w_out.astype(BF16),
        norm_mix=r(norm_mix), norm_ffn=r(norm_ffn), wq=peer_wq.astype(BF16), keys=peer_keys.astype(F32),
        utab=_pack_table(peer_u), vtab=_pack_table(peer_v), norm_ple=r(norm_ple),
        w_proj=w_ple_proj.astype(BF16), w_gate=w_ple_gate.astype(BF16), norm_final=r(norm_final))


def _layer(x, p, prefix, s0, past_k, past_v, past_lf, w):
    b, t, _ = x.shape
    n = b * t
    tm = _row_tile(n, 512)
    h = x.reshape(n, D_MODEL)
    conv_in, zg, qb, kb, vb, small = _inproj(h, w["norm_mix"], w["w_main"], w["w_small"], w["bias"], w["alog"], tm)
    conv3 = conv_in.reshape(b, t, CONV_DIM)
    small3 = small.reshape(b, t, LANES)
    prefix8 = jnp.zeros((b, SUBLANES, CONV_DIM), F32)
    if prefix is not None:
        prefix8 = prefix8.at[:, SUBLANES - (CONV_W - 1):, :].set(prefix.astype(F32))
    if s0 is None:
        s0 = jnp.zeros((b, H_A, DK_A, DV_A), F32)
    o_a, s_new = _gdn(conv3, small3, zg.reshape(b, t, A_WIDTH), prefix8, s0.astype(F32), w["conv_w"], w["gdn_onorm"],
                      nb=4 if b % 4 == 0 else 1)

    q3 = qb.reshape(b, t, B_WIDTH)
    k3 = kb.reshape(b, t, B_WIDTH)
    v3 = vb.reshape(b, t, B_WIDTH)
    if past_k is None:
        tq = _row_tile(t, 512)
        qx, kx, vx = _fox_prep(small3, q3, k3, v3, tq)
        o_b = _fox_prompt(qx, kx, vx, w["fox_onorm"], tq)
    else:
        pl_len = past_k.shape[1]
        lf_past = jnp.pad(past_lf.astype(F32), ((0, 0), (0, 0), (2 * H_A, LANES - 2 * H_A - H_B)))
        ccol, crow = _fox_cumsum(jnp.concatenate([lf_past, small3], axis=1))
        k_all = jnp.concatenate([past_k.reshape(b, pl_len, B_WIDTH).astype(F32), k3], axis=1)
        v_all = jnp.concatenate([past_v.reshape(b, pl_len, B_WIDTH).astype(F32), v3], axis=1)
        o_b = _fox_sample(q3, k_all, v_all, ccol[:, pl_len:, :], crow, w["fox_onorm"])

    h1, xn, qp = _outproj(o_a.reshape(n, A_WIDTH), o_b.reshape(n, B_WIDTH), h, w["w_out"], w["norm_ffn"], w["wq"], tm)
    e, gate = _peer_route(qp, w["keys"], _row_tile(n, 128))
    tb = _row_tile(n, 256)
    wgt = _peer_u(e, xn, gate, w["utab"], tb)
    h2 = _peer_v(e, wgt, h1, w["vtab"], tb)
    y = _ple(h2, p.reshape(n, PLE_DIM), w["norm_ple"], w["w_gate"], w["w_proj"],
             w["norm_final"], tm)
    conv_state = conv3[:, t - (CONV_W - 1):, :]
    return (y.reshape(b, t, D_MODEL), conv_state, s_new, k3.reshape(b, t, H_B, DH_B), v3.reshape(b, t, H_B, DH_B),
            small3[:, :, 2 * H_A:2 * H_A + H_B])


def kernel(x_prompt, x_sample, p_prompt, p_sample, cache_conv, state_gdn, cache_fox_k, cache_fox_v, cache_fox_logf, w_in, conv_w, a_log, dt_bias, gdn_onorm, b_f, fox_onorm, w_out, norm_mix, norm_ffn, peer_wq, peer_keys, peer_u, peer_v, norm_ple, w_ple_proj, w_ple_gate, norm_final):
    assert w_in.shape[0] == 1, "single-layer step"
    w = _prep_weights(w_in[0], conv_w[0], a_log[0], dt_bias[0], gdn_onorm[0], b_f[0], fox_onorm[0], w_out[0],
                      norm_mix[0], norm_ffn[0], peer_wq[0], peer_keys[0], peer_u[0], peer_v[0], norm_ple[0],
                      w_ple_proj[0], w_ple_gate[0], norm_final)
    ys, c2, s2, k2, v2, l2 = _layer(x_sample, p_sample[0], cache_conv[0], state_gdn[0], cache_fox_k[0],
                                    cache_fox_v[0], cache_fox_logf[0], w)
    yp, c1, s1, k1, v1, l1 = _layer(x_prompt, p_prompt[0], None, None, None, None, None, w)
    st = lambda a: a[None]
    return (yp, ys, st(c1), st(s1), st(k1), st(v1), st(l1), st(c2), st(s2), st(k2), st(v2), st(l2))
```

```python
import functools
import math

import jax
import jax.numpy as jnp
from jax import lax
from jax.experimental import pallas as pl
from jax.experimental.pallas import tpu as pltpu

F32 = jnp.float32
BF16 = jnp.bfloat16

D_MODEL = 1024
H_A, DK_A, DV_A = 4, 128, 128
CONV_W = 4
CONV_DIM = H_A * (2 * DK_A + DV_A)
A_WIDTH = H_A * DV_A
H_B, DH_B = 8, 64
B_WIDTH = H_B * DH_B
OFF1 = CONV_DIM
OFF2 = OFF1 + H_A
OFF3 = OFF2 + H_A
OFF4 = OFF3 + A_WIDTH
OFF5 = OFF4 + 3 * B_WIDTH
N_KEYS = 128
PEER_HEADS = 8
PEER_TOPK = 16
PEER_SLOTS = PEER_HEADS * PEER_TOPK
PLE_DIM = 256
EPS = 1e-6
GDN_CHUNK = 64

LANES = 128
SUBLANES = 8
VMEM_LIMIT = 56 * 1024 * 1024

NEG_BIG = -1e30


def _cparams(sem, vmem=VMEM_LIMIT):
    return pltpu.CompilerParams(dimension_semantics=sem, vmem_limit_bytes=vmem)


def _split_bf16(x):
    hi = x.astype(BF16)
    lo = (x - hi.astype(F32)).astype(BF16)
    return hi, lo


def _mm(a, b, passes=1, dims=(((1,), (0,)), ((), ()))):
    dg = functools.partial(lax.dot_general, dimension_numbers=dims, preferred_element_type=F32)
    if passes == 1:
        return dg(a.astype(BF16), b.astype(BF16))
    ah, al = _split_bf16(a)
    bh, bl = _split_bf16(b)
    return dg(ah, bh) + (dg(al, bh) + dg(ah, bl))


NT = (((1,), (1,)), ((), ()))


def _rms(x, g):
    return x * lax.rsqrt(jnp.mean(x * x, axis=-1, keepdims=True) + EPS) * g


def _softplus(x):
    return jnp.maximum(x, 0.0) + jnp.log1p(jnp.exp(-jnp.abs(x)))


def _sigmoid(x):
    return 1.0 / (1.0 + jnp.exp(-x))


def _inproj_kernel(h_ref, g_ref, w_ref, ws_ref, bias_ref, alog_ref,
                   conv_ref, gate_ref, qb_ref, kb_ref, vb_ref, small_ref):
    xn = _rms(h_ref[...], g_ref[...]).astype(BF16)
    dot = functools.partial(jnp.dot, preferred_element_type=F32)
    conv_ref[...] = dot(xn, w_ref[:, 0:CONV_DIM])
    gate_ref[...] = dot(xn, w_ref[:, CONV_DIM:CONV_DIM + A_WIDTH])
    o = CONV_DIM + A_WIDTH
    qb_ref[...] = (dot(xn, w_ref[:, o:o + B_WIDTH]) * (DH_B ** -0.5)).astype(BF16)
    kb_ref[...] = dot(xn, w_ref[:, o + B_WIDTH:o + 2 * B_WIDTH])
    vb_ref[...] = dot(xn, w_ref[:, o + 2 * B_WIDTH:o + 3 * B_WIDTH])
    z = dot(xn, ws_ref[...]) + bias_ref[...]
    lane = lax.broadcasted_iota(jnp.int32, z.shape, 1)
    small_ref[...] = jnp.where(lane < H_A, _sigmoid(z),
                               jnp.where(lane < 2 * H_A, -jnp.exp(alog_ref[...]) * _softplus(z), -_softplus(-z)))


def _inproj(h, g, w_main, w_small, bias, alog, tm):
    t = h.shape[0]
    nm = w_main.shape[1]
    full = lambda shape: pl.BlockSpec(shape, lambda i: (0, 0))
    row = lambda n: pl.BlockSpec((tm, n), lambda i: (i, 0))
    return pl.pallas_call(
        _inproj_kernel,
        grid=(t // tm,),
        in_specs=[row(D_MODEL), full((1, D_MODEL)), full((D_MODEL, nm)), full((D_MODEL, LANES)),
                  full((1, LANES)), full((1, LANES))],
        out_specs=[row(CONV_DIM), row(A_WIDTH), row(B_WIDTH), row(B_WIDTH), row(B_WIDTH), row(LANES)],
        out_shape=[jax.ShapeDtypeStruct((t, CONV_DIM), F32), jax.ShapeDtypeStruct((t, A_WIDTH), F32),
                   jax.ShapeDtypeStruct((t, B_WIDTH), BF16), jax.ShapeDtypeStruct((t, B_WIDTH), F32),
                   jax.ShapeDtypeStruct((t, B_WIDTH), F32), jax.ShapeDtypeStruct((t, LANES), F32)],
        compiler_params=_cparams(("parallel",)),
        name="inproj",
    )(h, g, w_main, w_small, bias, alog)


def _split3_bf16(x):
    h1 = x.astype(BF16)
    r = x - h1.astype(F32)
    h2 = r.astype(BF16)
    h3 = (r - h2.astype(F32)).astype(BF16)
    return h1, h2, h3


def _cumsum_kernel(lf_ref, ccol_ref, crow_ref, *, chunk):
    tk = lf_ref.shape[1]
    r = lax.broadcasted_iota(jnp.int32, (chunk, chunk), 0)
    c = lax.broadcasted_iota(jnp.int32, (chunk, chunk), 1)
    tril = (r >= c).astype(BF16)
    er = lax.broadcasted_iota(jnp.int32, (SUBLANES, LANES), 0)
    ec = lax.broadcasted_iota(jnp.int32, (SUBLANES, LANES), 1)
    pick = (ec == er + SUBLANES).astype(BF16)
    dot = functools.partial(jnp.dot, preferred_element_type=F32)
    carry = jnp.zeros((1, LANES), F32)
    for i in range(tk // chunk):
        sl = pl.ds(i * chunk, chunk)
        parts = _split3_bf16(lf_ref[0, sl, :])
        cs = dot(tril, parts[0]) + (dot(tril, parts[1]) + dot(tril, parts[2])) + carry
        ccol_ref[0, sl, :] = cs
        carry = cs[chunk - 1:chunk, :]
        cparts = _split3_bf16(cs)
        dg = functools.partial(lax.dot_general, dimension_numbers=NT, preferred_element_type=F32)
        crow_ref[0, :, sl] = dg(pick, cparts[0]) + (dg(pick, cparts[1]) + dg(pick, cparts[2]))


def _fox_cumsum(lf):
    b, tk, _ = lf.shape
    chunk = tk if tk <= 2048 else 512
    return pl.pallas_call(
        functools.partial(_cumsum_kernel, chunk=chunk),
        grid=(b,),
        in_specs=[pl.BlockSpec((1, tk, LANES), lambda i: (i, 0, 0))],
        out_specs=[pl.BlockSpec((1, tk, LANES), lambda i: (i, 0, 0)),
                   pl.BlockSpec((1, SUBLANES, tk), lambda i: (i, 0, 0))],
        out_shape=[jax.ShapeDtypeStruct((b, tk, LANES), F32), jax.ShapeDtypeStruct((b, SUBLANES, tk), F32)],
        compiler_params=_cparams(("parallel",)),
        name="fox_cumsum",
    )(lf)


GDN_PASSES = 1
GDN_STATE_PASSES = 3
TN = (((0,), (0,)), ((), ()))


def _unit_lower_inverses(ms, row, col):
    size = ms[0].shape[0]
    mm = functools.partial(_mm, passes=GDN_PASSES)
    eye = (row == col).astype(F32)
    diag = (row // 8) == (col // 8)
    ns = [jnp.where(diag, -m, 0.0) for m in ms]
    n2s = [mm(n, n) for n in ns]
    n4s = [mm(n2, n2) for n2 in n2s]
    ts = [eye + n for n in ns]
    ts = [t + mm(t, n2) for t, n2 in zip(ts, n2s)]
    ts = [t + mm(t, n4) for t, n4 in zip(ts, n4s)]
    blk = 8
    while blk < size:
        sel = ((row // (2 * blk)) == (col // (2 * blk))) & ((row // blk) != (col // blk))
        tls = [mm(t, jnp.where(sel, m, 0.0)) for t, m in zip(ts, ms)]
        ts = [t - mm(tl, t) for t, tl in zip(ts, tls)]
        blk *= 2
    return ts


def _gdn_kernel(u_ref, small_ref, zg_ref, prefix_ref, s0_ref, cw_ref, onorm_ref,
                o_ref, s_ref, ubuf_ref):
    L = GDN_CHUNK
    nb = u_ref.shape[0]
    mm = functools.partial(_mm, passes=GDN_PASSES)
    dot = functools.partial(jnp.dot, preferred_element_type=F32)
    dg = functools.partial(lax.dot_general, dimension_numbers=NT, preferred_element_type=F32)

    @pl.when(pl.program_id(1) == 0)
    def _():
        ubuf_ref[:, 0:SUBLANES, :] = prefix_ref[...]
        s_ref[...] = s0_ref[...]

    row = lax.broadcasted_iota(jnp.int32, (L, L), 0)
    col = lax.broadcasted_iota(jnp.int32, (L, L), 1)
    causal = row >= col
    strict = row > col
    tril = causal.astype(BF16)
    er = lax.broadcasted_iota(jnp.int32, (SUBLANES, LANES), 0)
    ec = lax.broadcasted_iota(jnp.int32, (SUBLANES, LANES), 1)
    pick = (ec == er + H_A).astype(BF16)

    cs, smalls, g_alls, g_rowss = [], [], [], []
    for b in range(nb):
        ubuf_ref[b, SUBLANES:SUBLANES + L, :] = u_ref[b]
        y = ubuf_ref[b, SUBLANES - 3:SUBLANES - 3 + L, :] * cw_ref[0:1, :]
        for j in range(1, CONV_W):
            y = y + ubuf_ref[b, SUBLANES - 3 + j:SUBLANES - 3 + j + L, :] * cw_ref[j:j + 1, :]
        tail = ubuf_ref[b, L:L + SUBLANES, :]
        ubuf_ref[b, 0:SUBLANES, :] = tail
        cs.append(y * _sigmoid(y))
        small = small_ref[b]
        sp = _split3_bf16(small)
        g_all = dot(tril, sp[0]) + (dot(tril, sp[1]) + dot(tril, sp[2]))
        gp = _split3_bf16(g_all)
        smalls.append(small)
        g_alls.append(g_all)
        g_rowss.append(dg(pick, gp[0]) + (dg(pick, gp[1]) + dg(pick, gp[2])))

    chains = [(b, h) for b in range(nb) for h in range(H_A)]
    qs, ks, vs, betas, gcols, decays, kbs = [], [], [], [], [], [], []
    for b, h in chains:
        c = cs[b]
        qh = c[:, h * DK_A:(h + 1) * DK_A]
        kh = c[:, H_A * DK_A + h * DK_A:H_A * DK_A + (h + 1) * DK_A]
        qs.append(qh * lax.rsqrt(jnp.sum(qh * qh, axis=-1, keepdims=True) + EPS) * (DK_A ** -0.5))
        kh = kh * lax.rsqrt(jnp.sum(kh * kh, axis=-1, keepdims=True) + EPS)
        ks.append(kh)
        vs.append(c[:, 2 * H_A * DK_A + h * DV_A:2 * H_A * DK_A + (h + 1) * DV_A])
        beta = smalls[b][:, h:h + 1]
        g_col = g_alls[b][:, H_A + h:H_A + h + 1]
        diff = g_col - g_rowss[b][h:h + 1, :]
        betas.append(beta)
        gcols.append(g_col)
        decays.append(jnp.where(causal, jnp.exp(jnp.where(causal, diff, 0.0)), 0.0))
        kbs.append(kh * beta)
    ms = [jnp.where(strict, mm(kb, k, dims=NT) * d, 0.0) for kb, k, d in zip(kbs, ks, decays)]
    ts = _unit_lower_inverses(ms, row, col)
    egs = [jnp.exp(g) for g in gcols]
    s_olds = [s_ref[b, h] for b, h in chains]
    sol_vs = [mm(t, v * beta) for t, v, beta in zip(ts, vs, betas)]
    sol_ks = [mm(t, kb * eg) for t, kb, eg in zip(ts, kbs, egs)]
    mm_state = functools.partial(_mm, passes=GDN_STATE_PASSES)
    u_news = [sv - mm_state(sk, s) for sv, sk, s in zip(sol_vs, sol_ks, s_olds)]
    attns = [_mm(q, k, dims=NT) * d for q, k, d in zip(qs, ks, decays)]
    outs = [_mm(q * eg, s) + _mm(a, u) for q, eg, s, a, u in zip(qs, egs, s_olds, attns, u_news)]
    for (b, h), k, g, s, u, o in zip(chains, ks, gcols, s_olds, u_news, outs):
        g_last = g[L - 1:L, :]
        s_ref[b, h] = jnp.exp(g_last) * s + mm_state(k * jnp.exp(g_last - g), u, dims=TN)
        zgh = zg_ref[b, :, h * DV_A:(h + 1) * DV_A]
        o_ref[b, :, h * DV_A:(h + 1) * DV_A] = _rms(o, onorm_ref[...]) * (zgh * _sigmoid(zgh))


def _gdn(u, small, zg, prefix8, s0, conv_w, onorm, nb):
    b, t, _ = u.shape
    L = GDN_CHUNK
    tile = lambda n: pl.BlockSpec((nb, L, n), lambda i, j: (i, j, 0))
    return pl.pallas_call(
        _gdn_kernel,
        grid=(b // nb, t // L),
        in_specs=[tile(CONV_DIM), tile(LANES), tile(A_WIDTH),
                  pl.BlockSpec((nb, SUBLANES, CONV_DIM), lambda i, j: (i, 0, 0)),
                  pl.BlockSpec((nb, H_A, DK_A, DV_A), lambda i, j: (i, 0, 0, 0)),
                  pl.BlockSpec((CONV_W, CONV_DIM), lambda i, j: (0, 0)),
                  pl.BlockSpec((1, DV_A), lambda i, j: (0, 0))],
        out_specs=[tile(A_WIDTH), pl.BlockSpec((nb, H_A, DK_A, DV_A), lambda i, j: (i, 0, 0, 0))],
        out_shape=[jax.ShapeDtypeStruct((b, t, A_WIDTH), F32), jax.ShapeDtypeStruct((b, H_A, DK_A, DV_A), F32)],
        scratch_shapes=[pltpu.VMEM((nb, SUBLANES + L, CONV_DIM), F32)],
        compiler_params=_cparams(("parallel", "arbitrary")),
        name="gdn",
    )(u, small, zg, prefix8, s0, conv_w, onorm)


FOX_GROUP = LANES
FOX_XW = H_B * FOX_GROUP
FOX_C0 = DH_B
FOX_C1 = DH_B + 3
FOX_HEAD_GROUP = 4
FOX_DIAG_ROWS = LANES


def _fox_prep_kernel(lf_ref, q_ref, k_ref, v_ref, qx_ref, kx_ref, vx_ref, carry_ref, place_ref):
    L = lf_ref.shape[1]
    dot = functools.partial(jnp.dot, preferred_element_type=F32)

    @pl.when(pl.program_id(1) == 0)
    def _():
        carry_ref[...] = jnp.zeros(carry_ref.shape, F32)
        r = lax.broadcasted_iota(jnp.int32, (B_WIDTH, FOX_XW), 0)
        c = lax.broadcasted_iota(jnp.int32, (B_WIDTH, FOX_XW), 1)
        d = c % FOX_GROUP
        place_ref[...] = ((d < DH_B) & (r == (c // FOX_GROUP) * DH_B + d)).astype(BF16)

    r = lax.broadcasted_iota(jnp.int32, (L, L), 0)
    c = lax.broadcasted_iota(jnp.int32, (L, L), 1)
    tril = (r >= c).astype(BF16)
    parts = _split3_bf16(lf_ref[0])
    cs = dot(tril, parts[0]) + (dot(tril, parts[1]) + dot(tril, parts[2])) + carry_ref[...]
    carry_ref[...] = cs[L - 1:L, :]
    cparts = _split3_bf16(cs)

    rr = lax.broadcasted_iota(jnp.int32, (LANES, FOX_XW), 0)
    cc = lax.broadcasted_iota(jnp.int32, (LANES, FOX_XW), 1)
    head_lane = rr == 2 * H_A + cc // FOX_GROUP
    dd = cc % FOX_GROUP
    lane1 = lax.broadcasted_iota(jnp.int32, (1, FOX_XW), 1) % FOX_GROUP
    qx = dot(q_ref[0], place_ref[...])
    kx = dot(k_ref[0].astype(BF16), place_ref[...])
    vx = dot(v_ref[0].astype(BF16), place_ref[...])
    for i in range(3):
        qx = qx + dot(cparts[i], (head_lane & (dd == FOX_C0 + i)).astype(BF16))
        kx = kx + dot(-cparts[i], (head_lane & (dd == FOX_C1 + i)).astype(BF16))
    qx = qx + ((lane1 >= FOX_C1) & (lane1 < FOX_C1 + 3)).astype(F32)
    kx = kx + ((lane1 >= FOX_C0) & (lane1 < FOX_C0 + 3)).astype(F32)
    vx = vx + (lane1 == DH_B).astype(F32)
    qx_ref[0] = qx.astype(BF16)
    kx_ref[0] = kx.astype(BF16)
    vx_ref[0] = vx.astype(BF16)


def _fox_prep(lf, q, k, v, tl):
    b, t, _ = q.shape
    blk = lambda w: pl.BlockSpec((1, tl, w), lambda i, j: (i, j, 0))
    xs = jax.ShapeDtypeStruct((b, t, FOX_XW), BF16)
    return pl.pallas_call(
        _fox_prep_kernel,
        grid=(b, t // tl),
        in_specs=[blk(LANES), blk(B_WIDTH), blk(B_WIDTH), blk(B_WIDTH)],
        out_specs=[blk(FOX_XW), blk(FOX_XW), blk(FOX_XW)],
        out_shape=[xs, xs, xs],
        scratch_shapes=[pltpu.VMEM((1, LANES), F32), pltpu.VMEM((B_WIDTH, FOX_XW), BF16)],
        compiler_params=_cparams(("parallel", "arbitrary")),
        name="fox_prep",
    )(lf, q, k, v)


def _fox_flash_kernel(qi_ref, kj_ref, qx_ref, kx_ref, vx_ref, onorm_ref, o_ref, m_ref, acc_ref):
    step = pl.program_id(1)
    qi = qi_ref[step]
    kj = kj_ref[step]
    tq = qx_ref.shape[1]
    tk = kx_ref.shape[1]

    @pl.when(kj == 0)
    def _():
        m_ref[...] = jnp.full(m_ref.shape, NEG_BIG, F32)
        acc_ref[...] = jnp.zeros(acc_ref.shape, F32)

    def attend(r0, nr, nc, diagonal):
        rows = slice(r0, r0 + nr)
        if diagonal:
            keep = (lax.broadcasted_iota(jnp.int32, (nr, nr), 0) >= lax.broadcasted_iota(jnp.int32, (nr, nr), 1))
        for h0 in range(0, H_B, FOX_HEAD_GROUP):
            heads = range(h0, h0 + FOX_HEAD_GROUP)
            hss = [slice(h * FOX_GROUP, (h + 1) * FOX_GROUP) for h in heads]
            ss = [lax.dot_general(qx_ref[0, rows, hs], kx_ref[0, 0:nc, hs], NT, preferred_element_type=F32)
                  for hs in hss]
            tiles = [[s[:, j * LANES:(j + 1) * LANES] for j in range(nc // LANES)] for s in ss]
            if diagonal:
                for t in tiles:
                    t[-1] = jnp.where(keep, t[-1], NEG_BIG)
            m_olds = [m_ref[h, rows] for h in heads]
            m_news = []
            for m_old, t in zip(m_olds, tiles):
                tmax = functools.reduce(jnp.maximum, t)
                m_news.append(jnp.maximum(m_old, jnp.broadcast_to(jnp.max(tmax, axis=-1, keepdims=True),
                                                                  m_old.shape)))
            ps = [jnp.concatenate([jnp.exp(x - m_new) for x in t], axis=1).astype(BF16)
                  for t, m_new in zip(tiles, m_news)]
            pvs = [jnp.dot(p, vx_ref[0, 0:nc, hs], preferred_element_type=F32) for p, hs in zip(ps, hss)]
            for h, m_old, m_new, pv in zip(heads, m_olds, m_news, pvs):
                acc_ref[h, rows] = jnp.exp(m_old - m_new) * acc_ref[h, rows] + pv
                m_ref[h, rows] = m_new

    @pl.when(kj < qi)
    def _():
        attend(0, tq, tk, False)

    @pl.when(kj == qi)
    def _():
        for i in range(tq // FOX_DIAG_ROWS):
            attend(i * FOX_DIAG_ROWS, FOX_DIAG_ROWS, (i + 1) * FOX_DIAG_ROWS, True)
        value_lane = lax.broadcasted_iota(jnp.int32, (1, FOX_GROUP), 1) < DH_B
        gain = jnp.concatenate([onorm_ref[...], onorm_ref[...]], axis=1)
        for h in range(0, H_B, 2):
            pair = []
            for hh in (h, h + 1):
                acc = acc_ref[hh]
                o = jnp.where(value_lane, acc, 0.0) / acc[:, DH_B:DH_B + 1]
                ms = jnp.sum(o * o, axis=-1, keepdims=True) * (1.0 / DH_B)
                pair.append(o * lax.rsqrt(ms + EPS) * gain)
            o_ref[0, :, h * DH_B:(h + 2) * DH_B] = pair[0] + pltpu.roll(pair[1], DH_B, axis=1)


def _fox_prompt(qx, kx, vx, onorm, tq):
    b, t, _ = qx.shape
    n = t // tq
    pairs = [(a, c) for a in range(n) for c in range(a + 1)]
    qi_tab = jnp.asarray([p[0] for p in pairs], jnp.int32)
    kj_tab = jnp.asarray([p[1] for p in pairs], jnp.int32)
    qspec = lambda w: pl.BlockSpec((1, tq, w), lambda i, s, qt, kt: (i, qt[s], 0))
    kspec = lambda w: pl.BlockSpec((1, tq, w), lambda i, s, qt, kt: (i, kt[s], 0))
    return pl.pallas_call(
        _fox_flash_kernel,
        grid_spec=pltpu.PrefetchScalarGridSpec(
            num_scalar_prefetch=2,
            grid=(b, len(pairs)),
            in_specs=[qspec(FOX_XW), kspec(FOX_XW), kspec(FOX_XW),
                      pl.BlockSpec((1, DH_B), lambda i, s, qt, kt: (0, 0))],
            out_specs=qspec(B_WIDTH),
            scratch_shapes=[pltpu.VMEM((H_B, tq, FOX_GROUP), F32), pltpu.VMEM((H_B, tq, FOX_GROUP), F32)]),
        out_shape=jax.ShapeDtypeStruct((b, t, B_WIDTH), F32),
        compiler_params=_cparams(("parallel", "arbitrary")),
        name="fox_prompt",
    )(qi_tab, kj_tab, qx, kx, vx, onorm)


def _fox_sample_kernel(q_ref, k_ref, v_ref, ccol_ref, crow_ref, onorm_ref, o_ref):
    tq = q_ref.shape[1]
    tk = k_ref.shape[1]
    row = lax.broadcasted_iota(jnp.int32, (tq, tk), 0) + (tk - tq)
    col = lax.broadcasted_iota(jnp.int32, (tq, tk), 1)
    keep = row >= col
    for h in range(H_B):
        hs = slice(h * DH_B, (h + 1) * DH_B)
        s = lax.dot_general(q_ref[0, :, hs], k_ref[0, :, hs].astype(BF16), NT, preferred_element_type=F32)
        s = s + (ccol_ref[0, :, SUBLANES + h:SUBLANES + h + 1] - crow_ref[0, h:h + 1, :])
        s = jnp.where(keep, s, NEG_BIG)
        p = jnp.exp(s - jnp.max(s, axis=-1, keepdims=True))
        l = jnp.sum(p, axis=-1, keepdims=True)
        o = jnp.dot(p.astype(BF16), v_ref[0, :, hs].astype(BF16), preferred_element_type=F32) / l
        o_ref[0, :, hs] = _rms(o, onorm_ref[...])


def _fox_sample(q, k_all, v_all, ccol_q, crow, onorm):
    b, tq, _ = q.shape
    tk = k_all.shape[1]
    blk = lambda n, w: pl.BlockSpec((1, n, w), lambda i: (i, 0, 0))
    return pl.pallas_call(
        _fox_sample_kernel,
        grid=(b,),
        in_specs=[blk(tq, B_WIDTH), blk(tk, B_WIDTH), blk(tk, B_WIDTH), blk(tq, LANES), blk(SUBLANES, tk),
                  pl.BlockSpec((1, DH_B), lambda i: (0, 0))],
        out_specs=blk(tq, B_WIDTH),
        out_shape=jax.ShapeDtypeStruct((b, tq, B_WIDTH), F32),
        compiler_params=_cparams(("parallel",)),
        name="fox_sample",
    )(q, k_all, v_all, ccol_q, crow, onorm)


def _outproj_kernel(oa_ref, ob_ref, h_ref, wo_ref, g_ref, wq_ref, h1_ref, xn_ref, qp_ref):
    dot = functools.partial(jnp.dot, preferred_element_type=F32)
    h1 = h_ref[...] + dot(oa_ref[...].astype(BF16), wo_ref[0:A_WIDTH, :]) \
        + dot(ob_ref[...].astype(BF16), wo_ref[A_WIDTH:A_WIDTH + B_WIDTH, :])
    h1_ref[...] = h1.reshape(h1_ref.shape)
    xn = _rms(h1, g_ref[...])
    xn_ref[...] = xn.reshape(xn_ref.shape)
    qp_ref[...] = dot(xn.astype(BF16), wq_ref[...])


def _outproj(oa, ob, h, w_out, g, wq, tm):
    t = h.shape[0]
    nq = wq.shape[1]
    full = lambda shape: pl.BlockSpec(shape, lambda i: (0, 0))
    row = lambda n: pl.BlockSpec((tm, n), lambda i: (i, 0))
    tok3 = pl.BlockSpec((tm, ROW_CHUNKS, LANES), lambda i: (i, 0, 0))
    return pl.pallas_call(
        _outproj_kernel,
        grid=(t // tm,),
        in_specs=[row(A_WIDTH), row(B_WIDTH), row(D_MODEL), full((D_MODEL, D_MODEL)), full((1, D_MODEL)),
                  full((D_MODEL, nq))],
        out_specs=[tok3, tok3, row(nq)],
        out_shape=[jax.ShapeDtypeStruct((t, ROW_CHUNKS, LANES), F32), jax.ShapeDtypeStruct((t, ROW_CHUNKS, LANES), F32),
                   jax.ShapeDtypeStruct((t, nq), F32)],
        compiler_params=_cparams(("parallel",)),
        name="outproj",
    )(oa, ob, h, w_out, g, wq)


def _topk_rows(ss, k, payloads=None):
    n = ss[0].shape[0]
    rid = lax.broadcasted_iota(jnp.int32, ss[0].shape, 0).astype(F32)
    rid_blocks = [rid[r:r + SUBLANES] for r in range(0, n, SUBLANES)]
    vals = [[] for _ in ss]
    picks = [[] for _ in ss]

    def block_argmax(s):
        level = [(s[r:r + SUBLANES], rb) for r, rb in zip(range(0, n, SUBLANES), rid_blocks)]
        while len(level) > 1:
            nxt = []
            for a, b in zip(level[0::2], level[1::2]):
                take = b[0] > a[0]
                nxt.append((jnp.maximum(a[0], b[0]), jnp.where(take, b[1], a[1])))
            if len(level) % 2:
                nxt.append(level[-1])
            level = nxt
        return level[0]

    for _ in range(k):
        tops = [block_argmax(s) for s in ss]
        ms = [jnp.max(v8, axis=0, keepdims=True) for v8, _ in tops]
        idxs = [jnp.min(jnp.where(v8 == m, r8, float(n)), axis=0, keepdims=True) for (v8, r8), m in zip(tops, ms)]
        hits = [rid == idx for idx in idxs]
        for i, (m, idx, hit) in enumerate(zip(ms, idxs, hits)):
            vals[i].append(m)
            picks[i].append(idx if payloads is None else
                            jnp.sum(jnp.where(hit, payloads[i], 0.0), axis=0, keepdims=True))
        ss = [jnp.where(hit, -jnp.inf, s) for s, hit in zip(ss, hits)]
    return [jnp.concatenate(v, axis=0) for v in vals], [jnp.concatenate(p, axis=0) for p in picks]


PEER_PAIRS = [(i, j) for i in range(PEER_TOPK) for j in range(PEER_TOPK) if (i + 1) * (j + 1) <= PEER_TOPK]
PEER_PAIR_ROWS = -(-len(PEER_PAIRS) // SUBLANES) * SUBLANES
PEER_ROUTE_GROUP = 8


def _pair_selectors():
    sel = [[[1.0 if (r < len(PEER_PAIRS) and PEER_PAIRS[r][side] == i) else 0.0 for i in range(PEER_TOPK)]
            for r in range(PEER_PAIR_ROWS)] for side in range(2)]
    return jnp.asarray(sel, BF16)


def _select_rows(sel, x, exact_f32):
    dot = functools.partial(jnp.dot, preferred_element_type=F32)
    if not exact_f32:
        return dot(sel, x.astype(BF16))
    p = _split3_bf16(x)
    return dot(sel, p[0]) + dot(sel, p[1]) + dot(sel, p[2])


def _peer_route_kernel(qp_ref, keys_ref, sel_ref, e_ref, gate_ref):
    tb = qp_ref.shape[0]
    dq = LANES
    k = PEER_TOPK
    valid = lax.broadcasted_iota(jnp.int32, (PEER_PAIR_ROWS, tb), 0) < len(PEER_PAIRS)
    es, gs = [], []
    for h0 in range(0, PEER_HEADS, PEER_ROUTE_GROUP):
        heads = range(h0, h0 + PEER_ROUTE_GROUP)
        sts = [_mm(keys_ref[c], qp_ref[:, (2 * h + c) * dq:(2 * h + c + 1) * dq], passes=3, dims=NT)
               for h in heads for c in range(2)]
        svs, sis = _topk_rows(sts, k)
        cands, ecands = [], []
        for g in range(len(heads)):
            sv0, sv1, si0, si1 = svs[2 * g], svs[2 * g + 1], sis[2 * g], sis[2 * g + 1]
            cand = _select_rows(sel_ref[0], sv0, True) + _select_rows(sel_ref[1], sv1, True)
            cands.append(jnp.where(valid, cand, -jnp.inf))
            ecands.append((_select_rows(sel_ref[0], si0, False) * N_KEYS + _select_rows(sel_ref[1], si1, False))
                          * ROW_WORDS)
        cvs, epicks = _topk_rows(cands, k, payloads=ecands)
        for cv, e in zip(cvs, epicks):
            p = jnp.exp(cv - cv[0:1, :])
            gs.append(p / jnp.sum(p, axis=0, keepdims=True))
            es.append(e)
    e_all = jnp.concatenate(es, axis=0)
    g_all = jnp.concatenate(gs, axis=0)
    for j in range(tb // LANES):
        e_ref[j * LANES:(j + 1) * LANES, :] = e_all[:, j * LANES:(j + 1) * LANES].T.astype(jnp.int32)
        gate_ref[j * LANES:(j + 1) * LANES, :] = g_all[:, j * LANES:(j + 1) * LANES].T


def _peer_route(qp, keys, tb):
    t = qp.shape[0]
    sel = _pair_selectors()
    return pl.pallas_call(
        _peer_route_kernel,
        grid=(t // tb,),
        in_specs=[pl.BlockSpec((tb, qp.shape[1]), lambda i: (i, 0)),
                  pl.BlockSpec(keys.shape, lambda i: (0, 0, 0)),
                  pl.BlockSpec(sel.shape, lambda i: (0, 0, 0))],
        out_specs=[pl.BlockSpec((tb, PEER_SLOTS), lambda i: (i, 0)), pl.BlockSpec((tb, PEER_SLOTS), lambda i: (i, 0))],
        out_shape=[jax.ShapeDtypeStruct((t, PEER_SLOTS), jnp.int32), jax.ShapeDtypeStruct((t, PEER_SLOTS), F32)],
        compiler_params=_cparams(("parallel",)),
        name="peer_route",
    )(qp, keys, sel)


ROW_WORDS = D_MODEL // 2 // LANES
ROW_CHUNKS = D_MODEL // LANES
PEER_NSLOT_U, PEER_GROUP_U, PEER_VIEW_U = 4, 4, 8
PEER_NSLOT_V, PEER_GROUP_V, PEER_VIEW_V = 2, 1, 128


def _pack_kernel(t_ref, o_ref):
    o_ref[...] = pltpu.bitcast(t_ref[...].astype(BF16), jnp.int32)


def _pack_table(tab, blk=1024):
    n = tab.shape[0]
    return pl.pallas_call(
        _pack_kernel,
        grid=(n // blk,),
        in_specs=[pl.BlockSpec((blk * ROW_CHUNKS, LANES), lambda i: (i, 0))],
        out_specs=pl.BlockSpec((blk * ROW_WORDS, LANES), lambda i: (i, 0)),
        out_shape=jax.ShapeDtypeStruct((n * ROW_WORDS, LANES), jnp.int32),
        compiler_params=_cparams(("parallel",)),
        name="pack_table",
    )(tab.astype(F32).reshape(n * ROW_CHUNKS, LANES))


def _gather_rows(tab_ref, e_ref, ts, slot_refs, view):
    for j0 in range(0, PEER_SLOTS, view):
        ids = [e_ref.at[t, pl.ds(j0, view)] for t in ts]
        for jj in range(view):
            j = j0 + jj
            for row_ids, slot_ref in zip(ids, slot_refs):
                slot_ref[j * ROW_WORDS:(j + 1) * ROW_WORDS, :] = \
                    tab_ref[pl.ds(pl.multiple_of(row_ids[jj], ROW_WORDS), ROW_WORDS), :]


def _slot_rows(slot_ref):
    return pltpu.bitcast(slot_ref[...], BF16)


def _slotted_token_loop(tb, nslot, group, gather, compute):
    groups = [tuple(range(s, s + group)) for s in range(0, nslot, group)]
    for ss in groups:
        gather(list(ss), ss)

    def body(i, carry):
        t0 = nslot * i
        for ss in groups:
            for s in ss:
                compute(t0 + s, s)
            gather([jnp.minimum(t0 + nslot + s, tb - 1) for s in ss], ss)
        return carry

    lax.fori_loop(0, tb // nslot, body, 0)


def _chunk_mask():
    n = PEER_SLOTS * ROW_CHUNKS
    r = lax.broadcasted_iota(jnp.int32, (ROW_CHUNKS, n), 0)
    c = lax.broadcasted_iota(jnp.int32, (ROW_CHUNKS, n), 1)
    return (c % ROW_CHUNKS == r).astype(F32)


def _gelu(a):
    return 0.5 * a * (1.0 + lax.erf(a * (2.0 ** -0.5)))


def _peer_u_kernel(e_ref, xn_ref, gate_ref, tab_ref, w_ref, slots_ref, drow_ref):
    tb = xn_ref.shape[0]
    mask = _chunk_mask()
    slots = [slots_ref.at[s] for s in range(slots_ref.shape[0])]

    def gather(ts, ss):
        _gather_rows(tab_ref, e_ref, ts, [slots[s] for s in ss], PEER_VIEW_U)

    def compute(t, s):
        xh, xl = _split_bf16(xn_ref[t])
        d = lax.dot_general(jnp.concatenate([xh, xl], axis=0), _slot_rows(slots[s]), NT,
                            preferred_element_type=F32)
        e = (d[0:ROW_CHUNKS] + d[ROW_CHUNKS:2 * ROW_CHUNKS]) * mask
        drow_ref[pl.ds(t, 1), :] = jnp.sum(e, axis=0, keepdims=True)

    _slotted_token_loop(tb, len(slots), PEER_GROUP_U, gather, compute)
    n = PEER_SLOTS * ROW_CHUNKS
    r = lax.broadcasted_iota(jnp.int32, (n, PEER_SLOTS), 0)
    c = lax.broadcasted_iota(jnp.int32, (n, PEER_SLOTS), 1)
    comp = (r // ROW_CHUNKS == c).astype(BF16)
    dp = _split3_bf16(drow_ref[...])
    dot = functools.partial(jnp.dot, preferred_element_type=F32)
    a = dot(dp[0], comp) + (dot(dp[1], comp) + dot(dp[2], comp))
    w_ref[...] = gate_ref[...] * _gelu(a)


def _peer_v_kernel(e_ref, w_ref, h_ref, tab_ref, o_ref, slots_ref, wexp_ref):
    tb = h_ref.shape[0]
    slots = [slots_ref.at[s] for s in range(slots_ref.shape[0])]
    mask = _chunk_mask()
    n = PEER_SLOTS * ROW_CHUNKS
    r = lax.broadcasted_iota(jnp.int32, (PEER_SLOTS, n), 0)
    c = lax.broadcasted_iota(jnp.int32, (PEER_SLOTS, n), 1)
    expand = (c // ROW_CHUNKS == r).astype(BF16)
    wp = _split3_bf16(w_ref[...])
    dot = functools.partial(jnp.dot, preferred_element_type=F32)
    wexp_ref[...] = dot(wp[0], expand) + (dot(wp[1], expand) + dot(wp[2], expand))

    def gather(ts, ss):
        _gather_rows(tab_ref, e_ref, ts, [slots[s] for s in ss], PEER_VIEW_V)

    def compute(t, s):
        wm =wexp_ref[pl.ds(t, 1), :] * mask
        wh, wl = _split_bf16(wm)
        d = jnp.dot(jnp.concatenate([wh, wl], axis=0), _slot_rows(slots[s]), preferred_element_type=F32)
        o_ref[t] = h_ref[t] + (d[0:ROW_CHUNKS] + d[ROW_CHUNKS:2 * ROW_CHUNKS])

    _slotted_token_loop(tb, len(slots), PEER_GROUP_V, gather, compute)


def _table_spec(tab):
    return pl.BlockSpec(tab.shape, lambda i: (0, 0), pipeline_mode=pl.Buffered(1))


def _peer_u(e, xn3, gate, tab, tb):
    t = e.shape[0]
    n = PEER_SLOTS * ROW_CHUNKS
    return pl.pallas_call(
        _peer_u_kernel,
        grid=(t // tb,),
        in_specs=[pl.BlockSpec((tb, PEER_SLOTS), lambda i: (i, 0), memory_space=pltpu.SMEM),
                  pl.BlockSpec((tb, ROW_CHUNKS, LANES), lambda i: (i, 0, 0)),
                  pl.BlockSpec((tb, PEER_SLOTS), lambda i: (i, 0)),
                  _table_spec(tab)],
        out_specs=pl.BlockSpec((tb, PEER_SLOTS), lambda i: (i, 0)),
        out_shape=jax.ShapeDtypeStruct((t, PEER_SLOTS), F32),
        scratch_shapes=[pltpu.VMEM((PEER_NSLOT_U, PEER_SLOTS * ROW_WORDS, LANES), jnp.int32), pltpu.VMEM((tb, n), F32)],
        compiler_params=_cparams(("arbitrary",)),
        name="peer_u",
    )(e, xn3, gate, tab)


def _peer_v(e, w, h3, tab, tb):
    t = e.shape[0]
    n = PEER_SLOTS * ROW_CHUNKS
    return pl.pallas_call(
        _peer_v_kernel,
        grid=(t // tb,),
        in_specs=[pl.BlockSpec((tb, PEER_SLOTS), lambda i: (i, 0), memory_space=pltpu.SMEM),
                  pl.BlockSpec((tb, PEER_SLOTS), lambda i: (i, 0)),
                  pl.BlockSpec((tb, ROW_CHUNKS, LANES), lambda i: (i, 0, 0)),
                  _table_spec(tab)],
        out_specs=pl.BlockSpec((tb, ROW_CHUNKS, LANES), lambda i: (i, 0, 0)),
        out_shape=jax.ShapeDtypeStruct((t, ROW_CHUNKS, LANES), F32),
        scratch_shapes=[pltpu.VMEM((PEER_NSLOT_V, PEER_SLOTS * ROW_WORDS, LANES), jnp.int32), pltpu.VMEM((tb, n), F32)],
        compiler_params=_cparams(("arbitrary",)),
        name="peer_v",
    )(e, w, h3, tab)


def _ple_kernel(h_ref, p_ref, gple_ref, wg_ref, wp_ref, gfin_ref, y_ref):
    dot = functools.partial(jnp.dot, preferred_element_type=F32)
    h = h_ref[...].reshape(h_ref.shape[0], D_MODEL)
    gate = _sigmoid(dot(_rms(h, gple_ref[...]).astype(BF16), wg_ref[...]))
    h = h + dot(p_ref[...].astype(BF16), wp_ref[...]) * gate
    y_ref[...] = _rms(h, gfin_ref[...])


def _ple(h, p, gple, wg, wp, gfin, tm):
    t = h.shape[0]
    full = lambda shape: pl.BlockSpec(shape, lambda i: (0, 0))
    row = lambda n: pl.BlockSpec((tm, n), lambda i: (i, 0))
    return pl.pallas_call(
        _ple_kernel,
        grid=(t // tm,),
        in_specs=[pl.BlockSpec((tm, ROW_CHUNKS, LANES), lambda i: (i, 0, 0)), row(PLE_DIM), full((1, D_MODEL)),
                  full((D_MODEL, D_MODEL)), full((PLE_DIM, D_MODEL)), full((1, D_MODEL))],
        out_specs=row(D_MODEL),
        out_shape=jax.ShapeDtypeStruct((t, D_MODEL), F32),
        compiler_params=_cparams(("parallel",)),
        name="ple_final",
    )(h, p, gple, wg, wp, gfin)


def _row_tile(n, want):
    t = want
    while n % t:
        t //= 2
    return t


def _prep_weights(w_in, conv_w, a_log, dt_bias, gdn_onorm, b_f, fox_onorm, w_out, norm_mix, norm_ffn,
                  peer_wq, peer_keys, peer_u, peer_v, norm_ple, w_ple_proj, w_ple_gate, norm_final):
    w_main = jnp.concatenate([w_in[:, :OFF1], w_in[:, OFF3:OFF4], w_in[:, OFF4:OFF5]], axis=1).astype(BF16)
    w_small = jnp.concatenate([w_in[:, OFF1:OFF3], w_in[:, OFF5:],
                               jnp.zeros((D_MODEL, LANES - 2 * H_A - H_B), w_in.dtype)], axis=1).astype(BF16)
    pad = lambda v, lo: jnp.pad(v.astype(F32), (lo, LANES - lo - v.shape[0])).reshape(1, LANES)
    bias = pad(dt_bias, H_A) + pad(b_f, 2 * H_A)
    alog = pad(a_log, H_A)
    r = lambda v: v.astype(F32).reshape(1, -1)
    return dict(
        w_main=w_main, w_small=w_small, bias=bias, alog=alog, conv_w=conv_w.astype(F32),
        gdn_onorm=r(gdn_onorm), fox_onorm=r(fox_onorm), w_out=w_out.astype(BF16),
        norm_mix=r(norm_mix), norm_ffn=r(norm_ffn), wq=peer_wq.astype(BF16), keys=peer_keys.astype(F32),
        utab=_pack_table(peer_u), vtab=_pack_table(peer_v), norm_ple=r(norm_ple),
        w_proj=w_ple_proj.astype(BF16), w_gate=w_ple_gate.astype(BF16), norm_final=r(norm_final))


def _layer(x, p, prefix, s0, past_k, past_v, past_lf, w):
    b, t, _ = x.shape
    n = b * t
    tm = _row_tile(n, 512)
    h = x.reshape(n, D_MODEL)
    conv_in, zg, qb, kb, vb, small = _inproj(h, w["norm_mix"], w["w_main"], w["w_small"], w["bias"], w["alog"], tm)
    conv3 = conv_in.reshape(b, t, CONV_DIM)
    small3 = small.reshape(b, t, LANES)
    prefix8 = jnp.zeros((b, SUBLANES, CONV_DIM), F32)
    if prefix is not None:
        prefix8 = prefix8.at[:, SUBLANES - (CONV_W - 1):, :].set(prefix.astype(F32))
    if s0 is None:
        s0 = jnp.zeros((b, H_A, DK_A, DV_A), F32)
    o_a, s_new = _gdn(conv3, small3, zg.reshape(b, t, A_WIDTH), prefix8, s0.astype(F32), w["conv_w"], w["gdn_onorm"],
                      nb=4 if b % 4 == 0 else 1)

    q3 = qb.reshape(b, t, B_WIDTH)
    k3 = kb.reshape(b, t, B_WIDTH)
    v3 = vb.reshape(b, t, B_WIDTH)
    if past_k is None:
        tq = _row_tile(t, 512)
        qx, kx, vx = _fox_prep(small3, q3, k3, v3, tq)
        o_b = _fox_prompt(qx, kx, vx, w["fox_onorm"], tq)
    else:
        pl_len = past_k.shape[1]
        lf_past = jnp.pad(past_lf.astype(F32), ((0, 0), (0, 0), (2 * H_A, LANES - 2 * H_A - H_B)))
        ccol, crow = _fox_cumsum(jnp.concatenate([lf_past, small3], axis=1))
        k_all = jnp.concatenate([past_k.reshape(b, pl_len, B_WIDTH).astype(F32), k3], axis=1)
        v_all = jnp.concatenate([past_v.reshape(b, pl_len, B_WIDTH).astype(F32), v3], axis=1)
        o_b = _fox_sample(q3, k_all, v_all, ccol[:, pl_len:, :], crow, w["fox_onorm"])

    h1, xn, qp = _outproj(o_a.reshape(n, A_WIDTH), o_b.reshape(n, B_WIDTH), h, w["w_out"], w["norm_ffn"], w["wq"], tm)
    e, gate = _peer_route(qp, w["keys"], _row_tile(n, 128))
    tb = _row_tile(n, 256)
    wgt = _peer_u(e, xn, gate, w["utab"], tb)
    h2 = _peer_v(e, wgt, h1, w["vtab"], tb)
    y = _ple(h2, p.reshape(n, PLE_DIM), w["norm_ple"], w["w_gate"], w["w_proj"],
             w["norm_final"], tm)
    conv_state = conv3[:, t - (CONV_W - 1):, :]
    return (y.reshape(b, t, D_MODEL), conv_state, s_new, k3.reshape(b, t, H_B, DH_B), v3.reshape(b, t, H_B, DH_B),
            small3[:, :, 2 * H_A:2 * H_A + H_B])


def kernel(x_prompt, x_sample, p_prompt, p_sample, cache_conv, state_gdn, cache_fox_k, cache_fox_v, cache_fox_logf, w_in, conv_w, a_log, dt_bias, gdn_onorm, b_f, fox_onorm, w_out, norm_mix, norm_ffn, peer_wq, peer_keys, peer_u, peer_v, norm_ple, w_ple_proj, w_ple_gate, norm_final):
    assert w_in.shape[0] == 1, "single-layer step"
    w = _prep_weights(w_in[0], conv_w[0], a_log[0], dt_bias[0], gdn_onorm[0], b_f[0], fox_onorm[0], w_out[0],
                      norm_mix[0], norm_ffn[0], peer_wq[0], peer_keys[0], peer_u[0], peer_v[0], norm_ple[0],
                      w_ple_proj[0], w_ple_gate[0], norm_final)
    ys, c2, s2, k2, v2, l2 = _layer(x_sample, p_sample[0], cache_conv[0], state_gdn[0], cache_fox_k[0],
                                    cache_fox_v[0], cache_fox_logf[0], w)
    yp, c1, s1, k1, v1, l1 = _layer(x_prompt, p_prompt[0], None, None, None, None, None, w)
    st = lambda a: a[None]
    return (yp, ys, st(c1), st(s1), st(k1), st(v1), st(l1), st(c2), st(s2), st(k2), st(v2), st(l2))
```

```python
import functools
import math

import jax
import jax.numpy as jnp
from jax import lax
from jax.experimental import pallas as pl
from jax.experimental.pallas import tpu as pltpu

F32 = jnp.float32
BF16 = jnp.bfloat16

D_MODEL = 1024
H_A, DK_A, DV_A = 4, 128, 128
CONV_W = 4
CONV_DIM = H_A * (2 * DK_A + DV_A)
A_WIDTH = H_A * DV_A
H_B, DH_B = 8, 64
B_WIDTH = H_B * DH_B
OFF1 = CONV_DIM
OFF2 = OFF1 + H_A
OFF3 = OFF2 + H_A
OFF4 = OFF3 + A_WIDTH
OFF5 = OFF4 + 3 * B_WIDTH
N_KEYS = 128
PEER_HEADS = 8
PEER_TOPK = 16
PEER_SLOTS = PEER_HEADS * PEER_TOPK
PLE_DIM = 256
EPS = 1e-6
GDN_CHUNK = 64

LANES = 128
SUBLANES = 8
VMEM_LIMIT = 56 * 1024 * 1024

NEG_BIG = -1e30


def _cparams(sem, vmem=VMEM_LIMIT):
    return pltpu.CompilerParams(dimension_semantics=sem, vmem_limit_bytes=vmem)


def _split_bf16(x):
    hi = x.astype(BF16)
    lo = (x - hi.astype(F32)).astype(BF16)
    return hi, lo


def _mm(a, b, passes=1, dims=(((1,), (0,)), ((), ()))):
    dg = functools.partial(lax.dot_general, dimension_numbers=dims, preferred_element_type=F32)
    if passes == 1:
        return dg(a.astype(BF16), b.astype(BF16))
    ah, al = _split_bf16(a)
    bh, bl = _split_bf16(b)
    return dg(ah, bh) + (dg(al, bh) + dg(ah, bl))


NT = (((1,), (1,)), ((), ()))


def _rms(x, g):
    return x * lax.rsqrt(jnp.mean(x * x, axis=-1, keepdims=True) + EPS) * g


def _softplus(x):
    return jnp.maximum(x, 0.0) + jnp.log1p(jnp.exp(-jnp.abs(x)))


def _sigmoid(x):
    return 1.0 / (1.0 + jnp.exp(-x))


def _inproj_kernel(h_ref, g_ref, w_ref, ws_ref, bias_ref, alog_ref,
                   conv_ref, gate_ref, qb_ref, kb_ref, vb_ref, small_ref):
    xn = _rms(h_ref[...], g_ref[...]).astype(BF16)
    dot = functools.partial(jnp.dot, preferred_element_type=F32)
    conv_ref[...] = dot(xn, w_ref[:, 0:CONV_DIM])
    gate_ref[...] = dot(xn, w_ref[:, CONV_DIM:CONV_DIM + A_WIDTH])
    o = CONV_DIM + A_WIDTH
    qb_ref[...] = (dot(xn, w_ref[:, o:o + B_WIDTH]) * (DH_B ** -0.5)).astype(BF16)
    kb_ref[...] = dot(xn, w_ref[:, o + B_WIDTH:o + 2 * B_WIDTH])
    vb_ref[...] = dot(xn, w_ref[:, o + 2 * B_WIDTH:o + 3 * B_WIDTH])
    z = dot(xn, ws_ref[...]) + bias_ref[...]
    lane = lax.broadcasted_iota(jnp.int32, z.shape, 1)
    small_ref[...] = jnp.where(lane < H_A, _sigmoid(z),
                               jnp.where(lane < 2 * H_A, -jnp.exp(alog_ref[...]) * _softplus(z), -_softplus(-z)))


def _inproj(h, g, w_main, w_small, bias, alog, tm):
    t = h.shape[0]
    nm = w_main.shape[1]
    full = lambda shape: pl.BlockSpec(shape, lambda i: (0, 0))
    row = lambda n: pl.BlockSpec((tm, n), lambda i: (i, 0))
    return pl.pallas_call(
        _inproj_kernel,
        grid=(t // tm,),
        in_specs=[row(D_MODEL), full((1, D_MODEL)), full((D_MODEL, nm)), full((D_MODEL, LANES)),
                  full((1, LANES)), full((1, LANES))],
        out_specs=[row(CONV_DIM), row(A_WIDTH), row(B_WIDTH), row(B_WIDTH), row(B_WIDTH), row(LANES)],
        out_shape=[jax.ShapeDtypeStruct((t, CONV_DIM), F32), jax.ShapeDtypeStruct((t, A_WIDTH), F32),
                   jax.ShapeDtypeStruct((t, B_WIDTH), BF16), jax.ShapeDtypeStruct((t, B_WIDTH), F32),
                   jax.ShapeDtypeStruct((t, B_WIDTH), F32), jax.ShapeDtypeStruct((t, LANES), F32)],
        compiler_params=_cparams(("parallel",)),
        name="inproj",
    )(h, g, w_main, w_small, bias, alog)


def _split3_bf16(x):
    h1 = x.astype(BF16)
    r = x - h1.astype(F32)
    h2 = r.astype(BF16)
    h3 = (r - h2.astype(F32)).astype(BF16)
    return h1, h2, h3


def _cumsum_kernel(lf_ref, ccol_ref, crow_ref, *, chunk):
    tk = lf_ref.shape[1]
    r = lax.broadcasted_iota(jnp.int32, (chunk, chunk), 0)
    c = lax.broadcasted_iota(jnp.int32, (chunk, chunk), 1)
    tril = (r >= c).astype(BF16)
    er = lax.broadcasted_iota(jnp.int32, (SUBLANES, LANES), 0)
    ec = lax.broadcasted_iota(jnp.int32, (SUBLANES, LANES), 1)
    pick = (ec == er + SUBLANES).astype(BF16)
    dot = functools.partial(jnp.dot, preferred_element_type=F32)
    carry = jnp.zeros((1, LANES), F32)
    for i in range(tk // chunk):
        sl = pl.ds(i * chunk, chunk)
        parts = _split3_bf16(lf_ref[0, sl, :])
        cs = dot(tril, parts[0]) + (dot(tril, parts[1]) + dot(tril, parts[2])) + carry
        ccol_ref[0, sl, :] = cs
        carry = cs[chunk - 1:chunk, :]
        cparts = _split3_bf16(cs)
        dg = functools.partial(lax.dot_general, dimension_numbers=NT, preferred_element_type=F32)
        crow_ref[0, :, sl] = dg(pick, cparts[0]) + (dg(pick, cparts[1]) + dg(pick, cparts[2]))


def _fox_cumsum(lf):
    b, tk, _ = lf.shape
    chunk = tk if tk <= 2048 else 512
    return pl.pallas_call(
        functools.partial(_cumsum_kernel, chunk=chunk),
        grid=(b,),
        in_specs=[pl.BlockSpec((1, tk, LANES), lambda i: (i, 0, 0))],
        out_specs=[pl.BlockSpec((1, tk, LANES), lambda i: (i, 0, 0)),
                   pl.BlockSpec((1, SUBLANES, tk), lambda i: (i, 0, 0))],
        out_shape=[jax.ShapeDtypeStruct((b, tk, LANES), F32), jax.ShapeDtypeStruct((b, SUBLANES, tk), F32)],
        compiler_params=_cparams(("parallel",)),
        name="fox_cumsum",
    )(lf)


GDN_PASSES = 1
GDN_STATE_PASSES = 3
TN = (((0,), (0,)), ((), ()))


def _unit_lower_inverses(ms, row, col):
    size = ms[0].shape[0]
    mm = functools.partial(_mm, passes=GDN_PASSES)
    eye = (row == col).astype(F32)
    diag = (row // 8) == (col // 8)
    ns = [jnp.where(diag, -m, 0.0) for m in ms]
    n2s = [mm(n, n) for n in ns]
    n4s = [mm(n2, n2) for n2 in n2s]
    ts = [eye + n for n in ns]
    ts = [t + mm(t, n2) for t, n2 in zip(ts, n2s)]
    ts = [t + mm(t, n4) for t, n4 in zip(ts, n4s)]
    blk = 8
    while blk < size:
        sel = ((row // (2 * blk)) == (col // (2 * blk))) & ((row // blk) != (col // blk))
        tls = [mm(t, jnp.where(sel, m, 0.0)) for t, m in zip(ts, ms)]
        ts = [t - mm(tl, t) for t, tl in zip(ts, tls)]
        blk *= 2
    return ts


def _gdn_kernel(u_ref, small_ref, zg_ref, prefix_ref, s0_ref, cw_ref, onorm_ref,
                o_ref, s_ref, ubuf_ref):
    L = GDN_CHUNK
    nb = u_ref.shape[0]
    mm = functools.partial(_mm, passes=GDN_PASSES)
    dot = functools.partial(jnp.dot, preferred_element_type=F32)
    dg = functools.partial(lax.dot_general, dimension_numbers=NT, preferred_element_type=F32)

    @pl.when(pl.program_id(1) == 0)
    def _():
        ubuf_ref[:, 0:SUBLANES, :] = prefix_ref[...]
        s_ref[...] = s0_ref[...]

    row = lax.broadcasted_iota(jnp.int32, (L, L), 0)
    col = lax.broadcasted_iota(jnp.int32, (L, L), 1)
    causal = row >= col
    strict = row > col
    tril = causal.astype(BF16)
    er = lax.broadcasted_iota(jnp.int32, (SUBLANES, LANES), 0)
    ec = lax.broadcasted_iota(jnp.int32, (SUBLANES, LANES), 1)
    pick = (ec == er + H_A).astype(BF16)

    cs, smalls, g_alls, g_rowss = [], [], [], []
    for b in range(nb):
        ubuf_ref[b, SUBLANES:SUBLANES + L, :] = u_ref[b]
        y = ubuf_ref[b, SUBLANES - 3:SUBLANES - 3 + L, :] * cw_ref[0:1, :]
        for j in range(1, CONV_W):
            y = y + ubuf_ref[b, SUBLANES - 3 + j:SUBLANES - 3 + j + L, :] * cw_ref[j:j + 1, :]
        tail = ubuf_ref[b, L:L + SUBLANES, :]
        ubuf_ref[b, 0:SUBLANES, :] = tail
        cs.append(y * _sigmoid(y))
        small = small_ref[b]
        sp = _split3_bf16(small)
        g_all = dot(tril, sp[0]) + (dot(tril, sp[1]) + dot(tril, sp[2]))
        gp = _split3_bf16(g_all)
        smalls.append(small)
        g_alls.append(g_all)
        g_rowss.append(dg(pick, gp[0]) + (dg(pick, gp[1]) + dg(pick, gp[2])))

    chains = [(b, h) for b in range(nb) for h in range(H_A)]
    qs, ks, vs, betas, gcols, decays, kbs = [], [], [], [], [], [], []
    for b, h in chains:
        c = cs[b]
        qh = c[:, h * DK_A:(h + 1) * DK_A]
        kh = c[:, H_A * DK_A + h * DK_A:H_A * DK_A + (h + 1) * DK_A]
        qs.append(qh * lax.rsqrt(jnp.sum(qh * qh, axis=-1, keepdims=True) + EPS) * (DK_A ** -0.5))
        kh = kh * lax.rsqrt(jnp.sum(kh * kh, axis=-1, keepdims=True) + EPS)
        ks.append(kh)
        vs.append(c[:, 2 * H_A * DK_A + h * DV_A:2 * H_A * DK_A + (h + 1) * DV_A])
        beta = smalls[b][:, h:h + 1]
        g_col = g_alls[b][:, H_A + h:H_A + h + 1]
        diff = g_col - g_rowss[b][h:h + 1, :]
        betas.append(beta)
        gcols.append(g_col)
        decays.append(jnp.where(causal, jnp.exp(jnp.where(causal, diff, 0.0)), 0.0))
        kbs.append(kh * beta)
    ms = [jnp.where(strict, mm(kb, k, dims=NT) * d, 0.0) for kb, k, d in zip(kbs, ks, decays)]
    ts = _unit_lower_inverses(ms, row, col)
    egs = [jnp.exp(g) for g in gcols]
    s_olds = [s_ref[b, h] for b, h in chains]
    sol_vs = [mm(t, v * beta) for t, v, beta in zip(ts, vs, betas)]
    sol_ks = [mm(t, kb * eg) for t, kb, eg in zip(ts, kbs, egs)]
    mm_state = functools.partial(_mm, passes=GDN_STATE_PASSES)
    u_news = [sv - mm_state(sk, s) for sv, sk, s in zip(sol_vs, sol_ks, s_olds)]
    attns = [_mm(q, k, dims=NT) * d for q, k, d in zip(qs, ks, decays)]
    outs = [_mm(q * eg, s) + _mm(a, u) for q, eg, s, a, u in zip(qs, egs, s_olds, attns, u_news)]
    for (b, h), k, g, s, u, o in zip(chains, ks, gcols, s_olds, u_news, outs):
        g_last = g[L - 1:L, :]
        s_ref[b, h] = jnp.exp(g_last) * s + mm_state(k * jnp.exp(g_last - g), u, dims=TN)
        zgh = zg_ref[b, :, h * DV_A:(h + 1) * DV_A]
        o_ref[b, :, h * DV_A:(h + 1) * DV_A] = _rms(o, onorm_ref[...]) * (zgh * _sigmoid(zgh))


def _gdn(u, small, zg, prefix8, s0, conv_w, onorm, nb):
    b, t, _ = u.shape
    L = GDN_CHUNK
    tile = lambda n: pl.BlockSpec((nb, L, n), lambda i, j: (i, j, 0))
    return pl.pallas_call(
        _gdn_kernel,
        grid=(b // nb, t // L),
        in_specs=[tile(CONV_DIM), tile(LANES), tile(A_WIDTH),
                  pl.BlockSpec((nb, SUBLANES, CONV_DIM), lambda i, j: (i, 0, 0)),
                  pl.BlockSpec((nb, H_A, DK_A, DV_A), lambda i, j: (i, 0, 0, 0)),
                  pl.BlockSpec((CONV_W, CONV_DIM), lambda i, j: (0, 0)),
                  pl.BlockSpec((1, DV_A), lambda i, j: (0, 0))],
        out_specs=[tile(A_WIDTH), pl.BlockSpec((nb, H_A, DK_A, DV_A), lambda i, j: (i, 0, 0, 0))],
        out_shape=[jax.ShapeDtypeStruct((b, t, A_WIDTH), F32), jax.ShapeDtypeStruct((b, H_A, DK_A, DV_A), F32)],
        scratch_shapes=[pltpu.VMEM((nb, SUBLANES + L, CONV_DIM), F32)],
        compiler_params=_cparams(("parallel", "arbitrary")),
        name="gdn",
    )(u, small, zg, prefix8, s0, conv_w, onorm)


FOX_GROUP = LANES
FOX_XW = H_B * FOX_GROUP
FOX_C0 = DH_B
FOX_C1 = DH_B + 3
FOX_HEAD_GROUP = 4
FOX_DIAG_ROWS = LANES


def _fox_prep_kernel(lf_ref, q_ref, k_ref, v_ref, qx_ref, kx_ref, vx_ref, carry_ref, place_ref, csel_ref):
    L = lf_ref.shape[1]
    dot = functools.partial(jnp.dot, preferred_element_type=F32)

    @pl.when(pl.program_id(1) == 0)
    def _():
        carry_ref[...] = jnp.zeros(carry_ref.shape, F32)
        r = lax.broadcasted_iota(jnp.int32, (B_WIDTH, FOX_XW), 0)
        c = lax.broadcasted_iota(jnp.int32, (B_WIDTH, FOX_XW), 1)
        d = c % FOX_GROUP
        place_ref[...] = ((d < DH_B) & (r == (c // FOX_GROUP) * DH_B + d)).astype(BF16)
        rr = lax.broadcasted_iota(jnp.int32, (3 * LANES, FOX_XW), 0)
        cc = lax.broadcasted_iota(jnp.int32, (3 * LANES, FOX_XW), 1)
        head_lane = rr % LANES == 2 * H_A + cc // FOX_GROUP
        dd = cc % FOX_GROUP - rr // LANES
        csel_ref[0] = (head_lane & (dd == FOX_C0)).astype(BF16)
        csel_ref[1] = (head_lane & (dd == FOX_C1)).astype(BF16)

    r = lax.broadcasted_iota(jnp.int32, (L, L), 0)
    c = lax.broadcasted_iota(jnp.int32, (L, L), 1)
    tril = (r >= c).astype(BF16)
    parts = _split3_bf16(lf_ref[0])
    cs = dot(tril, parts[0]) + (dot(tril, parts[1]) + dot(tril, parts[2])) + carry_ref[...]
    carry_ref[...] = cs[L - 1:L, :]
    cpieces = jnp.concatenate(_split3_bf16(cs), axis=1)

    lane1 = lax.broadcasted_iota(jnp.int32, (1, FOX_XW), 1) % FOX_GROUP
    qx = dot(q_ref[0], place_ref[...]) + dot(cpieces, csel_ref[0])
    kx = dot(k_ref[0].astype(BF16), place_ref[...]) + dot(-cpieces, csel_ref[1])
    vx = dot(v_ref[0].astype(BF16), place_ref[...])
    qx = qx + ((lane1 >= FOX_C1) & (lane1 < FOX_C1 + 3)).astype(F32)
    kx = kx + ((lane1 >= FOX_C0) & (lane1 < FOX_C0 + 3)).astype(F32)
    vx = vx + (lane1 == DH_B).astype(F32)
    qx_ref[0] = qx.astype(BF16)
    kx_ref[0] = kx.astype(BF16)
    vx_ref[0] = vx.astype(BF16)


def _fox_prep(lf, q, k, v, tl):
    b, t, _ = q.shape
    blk = lambda w: pl.BlockSpec((1, tl, w), lambda i, j: (i, j, 0))
    xs = jax.ShapeDtypeStruct((b, t, FOX_XW), BF16)
    return pl.pallas_call(
        _fox_prep_kernel,
        grid=(b, t // tl),
        in_specs=[blk(LANES), blk(B_WIDTH), blk(B_WIDTH), blk(B_WIDTH)],
        out_specs=[blk(FOX_XW), blk(FOX_XW), blk(FOX_XW)],
        out_shape=[xs, xs, xs],
        scratch_shapes=[pltpu.VMEM((1, LANES), F32), pltpu.VMEM((B_WIDTH, FOX_XW), BF16),
                        pltpu.VMEM((2, 3 * LANES, FOX_XW), BF16)],
        compiler_params=_cparams(("parallel", "arbitrary")),
        name="fox_prep",
    )(lf, q, k, v)


def _fox_flash_kernel(qi_ref, kj_ref, qx_ref, kx_ref, vx_ref, onorm_ref, o_ref, m_ref, acc_ref):
    step = pl.program_id(1)
    qi = qi_ref[step]
    kj = kj_ref[step]
    tq = qx_ref.shape[1]
    tk = kx_ref.shape[1]

    @pl.when(kj == 0)
    def _():
        m_ref[...] = jnp.full(m_ref.shape, NEG_BIG, F32)
        acc_ref[...] = jnp.zeros(acc_ref.shape, F32)

    def attend(r0, nr, nc, diagonal):
        rows = slice(r0, r0 + nr)
        if diagonal:
            keep = (lax.broadcasted_iota(jnp.int32, (nr, nr), 0) >= lax.broadcasted_iota(jnp.int32, (nr, nr), 1))
        for h0 in range(0, H_B, FOX_HEAD_GROUP):
            heads = range(h0, h0 + FOX_HEAD_GROUP)
            hss = [slice(h * FOX_GROUP, (h + 1) * FOX_GROUP) for h in heads]
            ss = [lax.dot_general(qx_ref[0, rows, hs], kx_ref[0, 0:nc, hs], NT, preferred_element_type=F32)
                  for hs in hss]
            tiles = [[s[:, j * LANES:(j + 1) * LANES] for j in range(nc // LANES)] for s in ss]
            if diagonal:
                for t in tiles:
                    t[-1] = jnp.where(keep, t[-1], NEG_BIG)
            m_olds = [m_ref[h, rows] for h in heads]
            m_news = []
            for m_old, t in zip(m_olds, tiles):
                tmax = functools.reduce(jnp.maximum, t)
                m_news.append(jnp.maximum(m_old, jnp.broadcast_to(jnp.max(tmax, axis=-1, keepdims=True),
                                                                  m_old.shape)))
            ps = [jnp.concatenate([jnp.exp(x - m_new) for x in t], axis=1).astype(BF16)
                  for t, m_new in zip(tiles, m_news)]
            pvs = [jnp.dot(p, vx_ref[0, 0:nc, hs], preferred_element_type=F32) for p, hs in zip(ps, hss)]
            for h, m_old, m_new, pv in zip(heads, m_olds, m_news, pvs):
                acc_ref[h, rows] = jnp.exp(m_old - m_new) * acc_ref[h, rows] + pv
                m_ref[h, rows] = m_new

    @pl.when(kj < qi)
    def _():
        attend(0, tq, tk, False)

    @pl.when(kj == qi)
    def _():
        for i in range(tq // FOX_DIAG_ROWS):
            attend(i * FOX_DIAG_ROWS, FOX_DIAG_ROWS, (i + 1) * FOX_DIAG_ROWS, True)
        value_lane = lax.broadcasted_iota(jnp.int32, (1, FOX_GROUP), 1) < DH_B
        gain = jnp.concatenate([onorm_ref[...], onorm_ref[...]], axis=1)
        for h in range(0, H_B, 2):
            pair = []
            for hh in (h, h + 1):
                acc = acc_ref[hh]
                o = jnp.where(value_lane, acc, 0.0) / acc[:, DH_B:DH_B + 1]
                ms = jnp.sum(o * o, axis=-1, keepdims=True) * (1.0 / DH_B)
                pair.append(o * lax.rsqrt(ms + EPS) * gain)
            o_ref[0, :, h * DH_B:(h + 2) * DH_B] = pair[0] + pltpu.roll(pair[1], DH_B, axis=1)


def _fox_prompt(qx, kx, vx, onorm, tq):
    b, t, _ = qx.shape
    n = t // tq
    pairs = [(a, c) for a in range(n) for c in range(a + 1)]
    qi_tab = jnp.asarray([p[0] for p in pairs], jnp.int32)
    kj_tab = jnp.asarray([p[1] for p in pairs], jnp.int32)
    qspec = lambda w: pl.BlockSpec((1, tq, w), lambda i, s, qt, kt: (i, qt[s], 0))
    kspec = lambda w: pl.BlockSpec((1, tq, w), lambda i, s, qt, kt: (i, kt[s], 0))
    return pl.pallas_call(
        _fox_flash_kernel,
        grid_spec=pltpu.PrefetchScalarGridSpec(
            num_scalar_prefetch=2,
            grid=(b, len(pairs)),
            in_specs=[qspec(FOX_XW), kspec(FOX_XW), kspec(FOX_XW),
                      pl.BlockSpec((1, DH_B), lambda i, s, qt, kt: (0, 0))],
            out_specs=qspec(B_WIDTH),
            scratch_shapes=[pltpu.VMEM((H_B, tq, FOX_GROUP), F32), pltpu.VMEM((H_B, tq, FOX_GROUP), F32)]),
        out_shape=jax.ShapeDtypeStruct((b, t, B_WIDTH), F32),
        compiler_params=_cparams(("parallel", "arbitrary")),
        name="fox_prompt",
    )(qi_tab, kj_tab, qx, kx, vx, onorm)


def _fox_sample_kernel(q_ref, k_ref, v_ref, ccol_ref, crow_ref, onorm_ref, o_ref):
    tq = q_ref.shape[1]
    tk = k_ref.shape[1]
    row = lax.broadcasted_iota(jnp.int32, (tq, tk), 0) + (tk - tq)
    col = lax.broadcasted_iota(jnp.int32, (tq, tk), 1)
    keep = row >= col
    for h in range(H_B):
        hs = slice(h * DH_B, (h + 1) * DH_B)
        s = lax.dot_general(q_ref[0, :, hs], k_ref[0, :, hs].astype(BF16), NT, preferred_element_type=F32)
        s = s + (ccol_ref[0, :, SUBLANES + h:SUBLANES + h + 1] - crow_ref[0, h:h + 1, :])
        s = jnp.where(keep, s, NEG_BIG)
        p = jnp.exp(s - jnp.max(s, axis=-1, keepdims=True))
        l = jnp.sum(p, axis=-1, keepdims=True)
        o = jnp.dot(p.astype(BF16), v_ref[0, :, hs].astype(BF16), preferred_element_type=F32) / l
        o_ref[0, :, hs] = _rms(o, onorm_ref[...])


def _fox_sample(q, k_all, v_all, ccol_q, crow, onorm):
    b, tq, _ = q.shape
    tk = k_all.shape[1]
    blk = lambda n, w: pl.BlockSpec((1, n, w), lambda i: (i, 0, 0))
    return pl.pallas_call(
        _fox_sample_kernel,
        grid=(b,),
        in_specs=[blk(tq, B_WIDTH), blk(tk, B_WIDTH), blk(tk, B_WIDTH), blk(tq, LANES), blk(SUBLANES, tk),
                  pl.BlockSpec((1, DH_B), lambda i: (0, 0))],
        out_specs=blk(tq, B_WIDTH),
        out_shape=jax.ShapeDtypeStruct((b, tq, B_WIDTH), F32),
        compiler_params=_cparams(("parallel",)),
        name="fox_sample",
    )(q, k_all, v_all, ccol_q, crow, onorm)


def _outproj_kernel(oa_ref, ob_ref, h_ref, wo_ref, g_ref, wq_ref, h1_ref, xn_ref, qp_ref):
    dot = functools.partial(jnp.dot, preferred_element_type=F32)
    h1 = h_ref[...] + dot(oa_ref[...].astype(BF16), wo_ref[0:A_WIDTH, :]) \
        + dot(ob_ref[...].astype(BF16), wo_ref[A_WIDTH:A_WIDTH + B_WIDTH, :])
    h1_ref[...] = h1.reshape(h1_ref.shape)
    xn = _rms(h1, g_ref[...])
    xn_ref[...] = xn.reshape(xn_ref.shape)
    qp_ref[...] = dot(xn.astype(BF16), wq_ref[...])


def _outproj(oa, ob, h, w_out, g, wq, tm):
    t = h.shape[0]
    nq = wq.shape[1]
    full = lambda shape: pl.BlockSpec(shape, lambda i: (0, 0))
    row = lambda n: pl.BlockSpec((tm, n), lambda i: (i, 0))
    tok3 = pl.BlockSpec((tm, ROW_CHUNKS, LANES), lambda i: (i, 0, 0))
    return pl.pallas_call(
        _outproj_kernel,
        grid=(t // tm,),
        in_specs=[row(A_WIDTH), row(B_WIDTH), row(D_MODEL), full((D_MODEL, D_MODEL)), full((1, D_MODEL)),
                  full((D_MODEL, nq))],
        out_specs=[tok3, tok3, row(nq)],
        out_shape=[jax.ShapeDtypeStruct((t, ROW_CHUNKS, LANES), F32), jax.ShapeDtypeStruct((t, ROW_CHUNKS, LANES), F32),
                   jax.ShapeDtypeStruct((t, nq), F32)],
        compiler_params=_cparams(("parallel",)),
        name="outproj",
    )(oa, ob, h, w_out, g, wq)


def _topk_rows(ss, k, payloads=None):
    n = ss[0].shape[0]
    rid = lax.broadcasted_iota(jnp.int32, ss[0].shape, 0).astype(F32)
    rid_blocks = [rid[r:r + SUBLANES] for r in range(0, n, SUBLANES)]
    vals = [[] for _ in ss]
    picks = [[] for _ in ss]

    def block_argmax(s):
        level = [(s[r:r + SUBLANES], rb) for r, rb in zip(range(0, n, SUBLANES), rid_blocks)]
        while len(level) > 1:
            nxt = []
            for a, b in zip(level[0::2], level[1::2]):
                take = b[0] > a[0]
                nxt.append((jnp.maximum(a[0], b[0]), jnp.where(take, b[1], a[1])))
            if len(level) % 2:
                nxt.append(level[-1])
            level = nxt
        return level[0]

    for _ in range(k):
        tops = [block_argmax(s) for s in ss]
        ms = [jnp.max(v8, axis=0, keepdims=True) for v8, _ in tops]
        idxs = [jnp.min(jnp.where(v8 == m, r8, float(n)), axis=0, keepdims=True) for (v8, r8), m in zip(tops, ms)]
        hits = [rid == idx for idx in idxs]
        for i, (m, idx, hit) in enumerate(zip(ms, idxs, hits)):
            vals[i].append(m)
            picks[i].append(idx if payloads is None else
                            jnp.sum(jnp.where(hit, payloads[i], 0.0), axis=0, keepdims=True))
        ss = [jnp.where(hit, -jnp.inf, s) for s, hit in zip(ss, hits)]
    return [jnp.concatenate(v, axis=0) for v in vals], [jnp.concatenate(p, axis=0) for p in picks]


PEER_PAIRS = [(i, j) for i in range(PEER_TOPK) for j in range(PEER_TOPK) if (i + 1) * (j + 1) <= PEER_TOPK]
PEER_PAIR_ROWS = -(-len(PEER_PAIRS) // SUBLANES) * SUBLANES
PEER_ROUTE_GROUP = 8


def _pair_selectors():
    sel = [[[1.0 if (r < len(PEER_PAIRS) and PEER_PAIRS[r][side] == i) else 0.0 for i in range(PEER_TOPK)]
            for r in range(PEER_PAIR_ROWS)] for side in range(2)]
    return jnp.asarray(sel, BF16)


def _select_rows(sel, x, exact_f32):
    dot = functools.partial(jnp.dot, preferred_element_type=F32)
    if not exact_f32:
        return dot(sel, x.astype(BF16))
    p = _split3_bf16(x)
    return dot(sel, p[0]) + dot(sel, p[1]) + dot(sel, p[2])


def _peer_route_kernel(qp_ref, keys_ref, sel_ref, e_ref, gate_ref):
    tb = qp_ref.shape[0]
    dq = LANES
    k = PEER_TOPK
    valid = lax.broadcasted_iota(jnp.int32, (PEER_PAIR_ROWS, tb), 0) < len(PEER_PAIRS)
    es, gs = [], []
    for h0 in range(0, PEER_HEADS, PEER_ROUTE_GROUP):
        heads = range(h0, h0 + PEER_ROUTE_GROUP)
        sts = [_mm(keys_ref[c], qp_ref[:, (2 * h + c) * dq:(2 * h + c + 1) * dq], passes=3, dims=NT)
               for h in heads for c in range(2)]
        svs, sis = _topk_rows(sts, k)
        cands, ecands = [], []
        for g in range(len(heads)):
            sv0, sv1, si0, si1 = svs[2 * g], svs[2 * g + 1], sis[2 * g], sis[2 * g + 1]
            cand = _select_rows(sel_ref[0], sv0, True) + _select_rows(sel_ref[1], sv1, True)
            cands.append(jnp.where(valid, cand, -jnp.inf))
            ecands.append((_select_rows(sel_ref[0], si0, False) * N_KEYS + _select_rows(sel_ref[1], si1, False))
                          * ROW_WORDS)
        cvs, epicks = _topk_rows(cands, k, payloads=ecands)
        for cv, e in zip(cvs, epicks):
            p = jnp.exp(cv - cv[0:1, :])
            gs.append(p / jnp.sum(p, axis=0, keepdims=True))
            es.append(e)
    e_all = jnp.concatenate(es, axis=0)
    g_all = jnp.concatenate(gs, axis=0)
    for j in range(tb // LANES):
        e_ref[j * LANES:(j + 1) * LANES, :] = e_all[:, j * LANES:(j + 1) * LANES].T.astype(jnp.int32)
        gate_ref[j * LANES:(j + 1) * LANES, :] = g_all[:, j * LANES:(j + 1) * LANES].T


def _peer_route(qp, keys, tb):
    t = qp.shape[0]
    sel = _pair_selectors()
    return pl.pallas_call(
        _peer_route_kernel,
        grid=(t // tb,),
        in_specs=[pl.BlockSpec((tb, qp.shape[1]), lambda i: (i, 0)),
                  pl.BlockSpec(keys.shape, lambda i: (0, 0, 0)),
                  pl.BlockSpec(sel.shape, lambda i: (0, 0, 0))],
        out_specs=[pl.BlockSpec((tb, PEER_SLOTS), lambda i: (i, 0)), pl.BlockSpec((tb, PEER_SLOTS), lambda i: (i, 0))],
        out_shape=[jax.ShapeDtypeStruct((t, PEER_SLOTS), jnp.int32), jax.ShapeDtypeStruct((t, PEER_SLOTS), F32)],
        compiler_params=_cparams(("parallel",)),
        name="peer_route",
    )(qp, keys, sel)


ROW_WORDS = D_MODEL // 2 // LANES
ROW_CHUNKS = D_MODEL // LANES
PEER_NSLOT_U, PEER_GROUP_U, PEER_VIEW_U = 4, 4, 8
PEER_NSLOT_V, PEER_GROUP_V, PEER_VIEW_V = 2, 1, 128


def _pack_kernel(t_ref, o_ref):
    o_ref[...] = pltpu.bitcast(t_ref[...].astype(BF16), jnp.int32)


def _pack_table(tab, blk=1024):
    n = tab.shape[0]
    return pl.pallas_call(
        _pack_kernel,
        grid=(n // blk,),
        in_specs=[pl.BlockSpec((blk * ROW_CHUNKS, LANES), lambda i: (i, 0))],
        out_specs=pl.BlockSpec((blk * ROW_WORDS, LANES), lambda i: (i, 0)),
        out_shape=jax.ShapeDtypeStruct((n * ROW_WORDS, LANES), jnp.int32),
        compiler_params=_cparams(("parallel",)),
        name="pack_table",
    )(tab.astype(F32).reshape(n * ROW_CHUNKS, LANES))


def _gather_rows(tab_ref, e_ref, ts, slot_refs, view):
    for j0 in range(0, PEER_SLOTS, view):
        ids = [e_ref.at[t, pl.ds(j0, view)] for t in ts]
        for jj in range(view):
            j = j0 + jj
            for row_ids, slot_ref in zip(ids, slot_refs):
                slot_ref[j * ROW_WORDS:(j + 1) * ROW_WORDS, :] = \
                    tab_ref[pl.ds(pl.multiple_of(row_ids[jj], ROW_WORDS), ROW_WORDS), :]


def _slot_rows(slot_ref):
    return pltpu.bitcast(slot_ref[...], BF16)


def _slotted_token_loop(tb, nslot, group, gather, compute):
    groups = [tuple(range(s, s + group)) for s in range(0, nslot, group)]
    for ss in groups:
        gather(list(ss), ss)

    def body(i, carry):
        t0 = nslot * i
        for ss in groups:
            for s in ss:
                compute(t0 + s, s)
            gather([jnp.minimum(t0 + nslot + s, tb - 1) for s in ss], ss)
        return carry

    lax.fori_loop(0, tb // nslot, body, 0)


def _chunk_mask():
    n = PEER_SLOTS * ROW_CHUNKS
    r = lax.broadcasted_iota(jnp.int32, (ROW_CHUNKS, n), 0)
    c = lax.broadcasted_iota(jnp.int32, (ROW_CHUNKS, n), 1)
    return (c % ROW_CHUNKS == r).astype(F32)


def _gelu(a):
    return 0.5 * a * (1.0 + lax.erf(a * (2.0 ** -0.5)))


def _peer_u_kernel(e_ref, xn_ref, gate_ref, tab_ref, w_ref, slots_ref, drow_ref):
    tb = xn_ref.shape[0]
    mask = _chunk_mask()
    slots = [slots_ref.at[s] for s in range(slots_ref.shape[0])]

    def gather(ts, ss):
        _gather_rows(tab_ref, e_ref, ts, [slots[s] for s in ss], PEER_VIEW_U)

    def compute(t, s):
        xh, xl = _split_bf16(xn_ref[t])
        d = lax.dot_general(jnp.concatenate([xh, xl], axis=0), _slot_rows(slots[s]), NT,
                            preferred_element_type=F32)
        e = (d[0:ROW_CHUNKS] + d[ROW_CHUNKS:2 * ROW_CHUNKS]) * mask
        drow_ref[pl.ds(t, 1), :] = jnp.sum(e, axis=0, keepdims=True)

    _slotted_token_loop(tb, len(slots), PEER_GROUP_U, gather, compute)
    n = PEER_SLOTS * ROW_CHUNKS
    r = lax.broadcasted_iota(jnp.int32, (n, PEER_SLOTS), 0)
    c = lax.broadcasted_iota(jnp.int32, (n, PEER_SLOTS), 1)
    comp = (r // ROW_CHUNKS == c).astype(BF16)
    dp = _split3_bf16(drow_ref[...])
    dot = functools.partial(jnp.dot, preferred_element_type=F32)
    a = dot(dp[0], comp) + (dot(dp[1], comp) + dot(dp[2], comp))
    w_ref[...] = gate_ref[...] * _gelu(a)


def _peer_v_kernel(e_ref, w_ref, h_ref, tab_ref, o_ref, slots_ref, wexp_ref):
    tb = h_ref.shape[0]
    slots = [slots_ref.at[s] for s in range(slots_ref.shape[0])]
    mask = _chunk_mask()
    n = PEER_SLOTS * ROW_CHUNKS
    r = lax.broadcasted_iota(jnp.int32, (PEER_SLOTS, n), 0)
    c = lax.broadcasted_iota(jnp.int32, (PEER_SLOTS, n), 1)
    expand = (c // ROW_CHUNKS == r).astype(BF16)
    wp = _split3_bf16(w_ref[...])
    dot = functools.partial(jnp.dot, preferred_element_type=F32)
    wexp_ref[...] = dot(wp[0], expand) + (dot(wp[1], expand) + dot(wp[2], expand))

    def gather(ts, ss):
        _gather_rows(tab_ref, e_ref, ts, [slots[s] for s in ss], PEER_VIEW_V)

    def compute(t, s):
        wm =wexp_ref[pl.ds(t, 1), :] * mask
        wh, wl = _split_bf16(wm)
        d = jnp.dot(jnp.concatenate([wh, wl], axis=0), _slot_rows(slots[s]), preferred_element_type=F32)
        o_ref[t] = h_ref[t] + (d[0:ROW_CHUNKS] + d[ROW_CHUNKS:2 * ROW_CHUNKS])

    _slotted_token_loop(tb, len(slots), PEER_GROUP_V, gather, compute)


def _table_spec(tab):
    return pl.BlockSpec(tab.shape, lambda i: (0, 0), pipeline_mode=pl.Buffered(1))


def _peer_u(e, xn3, gate, tab, tb):
    t = e.shape[0]
    n = PEER_SLOTS * ROW_CHUNKS
    return pl.pallas_call(
        _peer_u_kernel,
        grid=(t // tb,),
        in_specs=[pl.BlockSpec((tb, PEER_SLOTS), lambda i: (i, 0), memory_space=pltpu.SMEM),
                  pl.BlockSpec((tb, ROW_CHUNKS, LANES), lambda i: (i, 0, 0)),
                  pl.BlockSpec((tb, PEER_SLOTS), lambda i: (i, 0)),
                  _table_spec(tab)],
        out_specs=pl.BlockSpec((tb, PEER_SLOTS), lambda i: (i, 0)),
        out_shape=jax.ShapeDtypeStruct((t, PEER_SLOTS), F32),
        scratch_shapes=[pltpu.VMEM((PEER_NSLOT_U, PEER_SLOTS * ROW_WORDS, LANES), jnp.int32), pltpu.VMEM((tb, n), F32)],
        compiler_params=_cparams(("arbitrary",)),
        name="peer_u",
    )(e, xn3, gate, tab)


def _peer_v(e, w, h3, tab, tb):
    t = e.shape[0]
    n = PEER_SLOTS * ROW_CHUNKS
    return pl.pallas_call(
        _peer_v_kernel,
        grid=(t // tb,),
        in_specs=[pl.BlockSpec((tb, PEER_SLOTS), lambda i: (i, 0), memory_space=pltpu.SMEM),
                  pl.BlockSpec((tb, PEER_SLOTS), lambda i: (i, 0)),
                  pl.BlockSpec((tb, ROW_CHUNKS, LANES), lambda i: (i, 0, 0)),
                  _table_spec(tab)],
        out_specs=pl.BlockSpec((tb, ROW_CHUNKS, LANES), lambda i: (i, 0, 0)),
        out_shape=jax.ShapeDtypeStruct((t, ROW_CHUNKS, LANES), F32),
        scratch_shapes=[pltpu.VMEM((PEER_NSLOT_V, PEER_SLOTS * ROW_WORDS, LANES), jnp.int32), pltpu.VMEM((tb, n), F32)],
        compiler_params=_cparams(("arbitrary",)),
        name="peer_v",
    )(e, w, h3, tab)


def _ple_kernel(h_ref, p_ref, gple_ref, wg_ref, wp_ref, gfin_ref, y_ref):
    dot = functools.partial(jnp.dot, preferred_element_type=F32)
    h = h_ref[...].reshape(h_ref.shape[0], D_MODEL)
    gate = _sigmoid(dot(_rms(h, gple_ref[...]).astype(BF16), wg_ref[...]))
    h = h + dot(p_ref[...].astype(BF16), wp_ref[...]) * gate
    y_ref[...] = _rms(h, gfin_ref[...])


def _ple(h, p, gple, wg, wp, gfin, tm):
    t = h.shape[0]
    full = lambda shape: pl.BlockSpec(shape, lambda i: (0, 0))
    row = lambda n: pl.BlockSpec((tm, n), lambda i: (i, 0))
    return pl.pallas_call(
        _ple_kernel,
        grid=(t // tm,),
        in_specs=[pl.BlockSpec((tm, ROW_CHUNKS, LANES), lambda i: (i, 0, 0)), row(PLE_DIM), full((1, D_MODEL)),
                  full((D_MODEL, D_MODEL)), full((PLE_DIM, D_MODEL)), full((1, D_MODEL))],
        out_specs=row(D_MODEL),
        out_shape=jax.ShapeDtypeStruct((t, D_MODEL), F32),
        compiler_params=_cparams(("parallel",)),
        name="ple_final",
    )(h, p, gple, wg, wp, gfin)


def _row_tile(n, want):
    t = want
    while n % t:
        t //= 2
    return t


def _prep_weights(w_in, conv_w, a_log, dt_bias, gdn_onorm, b_f, fox_onorm, w_out, norm_mix, norm_ffn,
                  peer_wq, peer_keys, peer_u, peer_v, norm_ple, w_ple_proj, w_ple_gate, norm_final):
    w_main = jnp.concatenate([w_in[:, :OFF1], w_in[:, OFF3:OFF4], w_in[:, OFF4:OFF5]], axis=1).astype(BF16)
    w_small = jnp.concatenate([w_in[:, OFF1:OFF3], w_in[:, OFF5:],
                               jnp.zeros((D_MODEL, LANES - 2 * H_A - H_B), w_in.dtype)], axis=1).astype(BF16)
    pad = lambda v, lo: jnp.pad(v.astype(F32), (lo, LANES - lo - v.shape[0])).reshape(1, LANES)
    bias = pad(dt_bias, H_A) + pad(b_f, 2 * H_A)
    alog = pad(a_log, H_A)
    r = lambda v: v.astype(F32).reshape(1, -1)
    return dict(
        w_main=w_main, w_small=w_small, bias=bias, alog=alog, conv_w=conv_w.astype(F32),
        gdn_onorm=r(gdn_onorm), fox_onorm=r(fox_onorm), w_out=w_out.astype(BF16),
        norm_mix=r(norm_mix), norm_ffn=r(norm_ffn), wq=peer_wq.astype(BF16), keys=peer_keys.astype(F32),
        utab=_pack_table(peer_u), vtab=_pack_table(peer_v), norm_ple=r(norm_ple),
        w_proj=w_ple_proj.astype(BF16), w_gate=w_ple_gate.astype(BF16), norm_final=r(norm_final))


def _layer(x, p, prefix, s0, past_k, past_v, past_lf, w):
    b, t, _ = x.shape
    n = b * t
    tm = _row_tile(n, 512)
    h = x.reshape(n, D_MODEL)
    conv_in, zg, qb, kb, vb, small = _inproj(h, w["norm_mix"], w["w_main"], w["w_small"], w["bias"], w["alog"], tm)
    conv3 = conv_in.reshape(b, t, CONV_DIM)
    small3 = small.reshape(b, t, LANES)
    prefix8 = jnp.zeros((b, SUBLANES, CONV_DIM), F32)
    if prefix is not None:
        prefix8 = prefix8.at[:, SUBLANES - (CONV_W - 1):, :].set(prefix.astype(F32))
    if s0 is None:
        s0 = jnp.zeros((b, H_A, DK_A, DV_A), F32)
    o_a, s_new = _gdn(conv3, small3, zg.reshape(b, t, A_WIDTH), prefix8, s0.astype(F32), w["conv_w"], w["gdn_onorm"],
                      nb=8 if b % 8 == 0 else 1)

    q3 = qb.reshape(b, t, B_WIDTH)
    k3 = kb.reshape(b, t, B_WIDTH)
    v3 = vb.reshape(b, t, B_WIDTH)
    if past_k is None:
        tq = _row_tile(t, 512)
        qx, kx, vx = _fox_prep(small3, q3, k3, v3, tq)
        o_b = _fox_prompt(qx, kx, vx, w["fox_onorm"], tq)
    else:
        pl_len = past_k.shape[1]
        lf_past = jnp.pad(past_lf.astype(F32), ((0, 0), (0, 0), (2 * H_A, LANES - 2 * H_A - H_B)))
        ccol, crow = _fox_cumsum(jnp.concatenate([lf_past, small3], axis=1))
        k_all = jnp.concatenate([past_k.reshape(b, pl_len, B_WIDTH).astype(F32), k3], axis=1)
        v_all = jnp.concatenate([past_v.reshape(b, pl_len, B_WIDTH).astype(F32), v3], axis=1)
        o_b = _fox_sample(q3, k_all, v_all, ccol[:, pl_len:, :], crow, w["fox_onorm"])

    h1, xn, qp = _outproj(o_a.reshape(n, A_WIDTH), o_b.reshape(n, B_WIDTH), h, w["w_out"], w["norm_ffn"], w["wq"], tm)
    e, gate = _peer_route(qp, w["keys"], _row_tile(n, 128))
    tb = _row_tile(n, 256)
    wgt = _peer_u(e, xn, gate, w["utab"], tb)
    h2 = _peer_v(e, wgt, h1, w["vtab"], tb)
    y = _ple(h2, p.reshape(n, PLE_DIM), w["norm_ple"], w["w_gate"], w["w_proj"],
             w["norm_final"], tm)
    conv_state = conv3[:, t - (CONV_W - 1):, :]
    return (y.reshape(b, t, D_MODEL), conv_state, s_new, k3.reshape(b, t, H_B, DH_B), v3.reshape(b, t, H_B, DH_B),
            small3[:, :, 2 * H_A:2 * H_A + H_B])


def kernel(x_prompt, x_sample, p_prompt, p_sample, cache_conv, state_gdn, cache_fox_k, cache_fox_v, cache_fox_logf, w_in, conv_w, a_log, dt_bias, gdn_onorm, b_f, fox_onorm, w_out, norm_mix, norm_ffn, peer_wq, peer_keys, peer_u, peer_v, norm_ple, w_ple_proj, w_ple_gate, norm_final):
    assert w_in.shape[0] == 1, "single-layer step"
    w = _prep_weights(w_in[0], conv_w[0], a_log[0], dt_bias[0], gdn_onorm[0], b_f[0], fox_onorm[0], w_out[0],
                      norm_mix[0], norm_ffn[0], peer_wq[0], peer_keys[0], peer_u[0], peer_v[0], norm_ple[0],
                      w_ple_proj[0], w_ple_gate[0], norm_final)
    ys, c2, s2, k2, v2, l2 = _layer(x_sample, p_sample[0], cache_conv[0], state_gdn[0], cache_fox_k[0],
                                    cache_fox_v[0], cache_fox_logf[0], w)
    yp, c1, s1, k1, v1, l1 = _layer(x_prompt, p_prompt[0], None, None, None, None, None, w)
    st = lambda a: a[None]
    return (yp, ys, st(c1), st(s1), st(k1), st(v1), st(l1), st(c2), st(s2), st(k2), st(v2), st(l2))
```
